```python
import math
import jax
import jax.numpy as jnp
from jax import lax
import numpy as np

D_MODEL = 1024
BATCH = 8
SEQ = 4096
DEPTH = 4

GRID_W = 64
N_MEM = 256
HEAD_DIM = 64
MIX_WIDTH = D_MODEL
POOL_WIDTH = MIX_WIDTH // 4
POOL_WINDOWS = (2, 4, 8, 16)
POOL_GROUP = POOL_WIDTH // len(POOL_WINDOWS)
MLSTM_WIDTH = MIX_WIDTH // 4
MLSTM_HEADS = MLSTM_WIDTH // HEAD_DIM
MLSTM_CHUNK = 128
MLSTM_CONV = 5
ATTN_Q_WIDTH = MIX_WIDTH // 2
ATTN_Q_HEADS = ATTN_Q_WIDTH // HEAD_DIM
ATTN_KV_HEADS = ATTN_Q_HEADS // 4
ATTN_KV_WIDTH = ATTN_KV_HEADS * HEAD_DIM
Q_BLOCK = 128
ROPE_THETA = 10000.0
IN_SPLITS = (POOL_WIDTH, MLSTM_WIDTH, MLSTM_WIDTH, MLSTM_WIDTH, MLSTM_WIDTH,
             2 * MLSTM_HEADS, 2 * MLSTM_HEADS, ATTN_Q_WIDTH, ATTN_KV_WIDTH, ATTN_KV_WIDTH)
IN_COLS = sum(IN_SPLITS)
CA_HEADS = 4
CA_WIDTH = D_MODEL // 2
CA_HEAD_DIM = CA_WIDTH // CA_HEADS
N_EXPERTS = 16
EC_CAPACITY_FACTOR = 2
D_FF_EXPERT = 2 * D_MODEL
EPS = 1e-6

kernel_name = 'hybrid_pool_mlstm_gqa_ec_encoder'

F32 = jnp.float32


def rmsnorm(x, g):
    xf = x.astype(F32)
    y = xf * lax.rsqrt(jnp.mean(xf * xf, axis=-1, keepdims=True) + EPS)
    return (y * g.astype(F32)).astype(x.dtype)


def head_rmsnorm(a, g):
    af = a.astype(F32)
    return af * lax.rsqrt(jnp.mean(af * af, axis=-1, keepdims=True) + EPS) * g.astype(F32)


def axial_rope_tables(s):
    rows = s // GRID_W
    row_ids = jnp.repeat(jnp.arange(rows), GRID_W).astype(F32)
    col_ids = jnp.tile(jnp.arange(GRID_W), rows).astype(F32)
    n_freq = HEAD_DIM // 4
    inv_freq = ROPE_THETA ** (-jnp.arange(n_freq, dtype=F32) / n_freq)
    ang = jnp.concatenate([row_ids[:, None] * inv_freq, col_ids[:, None] * inv_freq], axis=-1)
    return jnp.cos(ang), jnp.sin(ang)


def apply_rope(a, cos, sin):
    a2 = a.reshape(*a.shape[:-1], HEAD_DIM // 2, 2)
    x0, x1 = a2[..., 0], a2[..., 1]
    c = cos[None, :, None, :]
    sn = sin[None, :, None, :]
    return jnp.stack([x0 * c - x1 * sn, x0 * sn + x1 * c], axis=-1).reshape(a.shape)


def pool_mixer(u, w, scale):
    b, s, _ = u.shape
    uf = u.astype(F32)
    cs = jnp.concatenate([jnp.zeros((b, 1, POOL_WIDTH), F32), jnp.cumsum(uf, axis=1)], axis=1)
    t = jnp.arange(s)
    diffs = []
    for g, win in enumerate(POOL_WINDOWS):
        lo = jnp.clip(t - win // 2, 0, s)
        hi = jnp.clip(t + win // 2, 0, s)
        c0 = g * POOL_GROUP
        csg = cs[:, :, c0:c0 + POOL_GROUP]
        mean = (csg[:, hi] - csg[:, lo]) / (hi - lo).astype(F32)[None, :, None]
        diffs.append(mean - uf[:, :, c0:c0 + POOL_GROUP])
    d = jnp.stack(diffs, axis=2)
    y = jnp.einsum('bsgc,gce->bsge', d, w.astype(F32)).reshape(b, s, POOL_WIDTH)
    return (y * scale.astype(F32)).astype(u.dtype)


def mlstm_scan(q, k, v, log_i, log_f):
    b, h, s, dh = q.shape
    nc = s // MLSTM_CHUNK

    def chunks(a):
        a = a.reshape(b, h, nc, MLSTM_CHUNK, *a.shape[3:])
        return jnp.moveaxis(a, 2, 0)

    lower = jnp.tril(jnp.ones((MLSTM_CHUNK, MLSTM_CHUNK), dtype=bool))

    def step(carry, inp):
        c_st, n_st, m_st = carry
        qc, kc, vc, li, lf = inp
        bcum = jnp.cumsum(lf, axis=-1)
        dmat = jnp.where(lower, bcum[..., :, None] - bcum[..., None, :] + li[..., None, :], -jnp.inf)
        inter = bcum + m_st[..., None]
        m_j = jnp.maximum(inter, jnp.max(dmat, axis=-1))
        a_mat = jnp.exp(dmat - m_j[..., None]) * jnp.einsum('bhjd,bhsd->bhjs', qc, kc)
        a_int = jnp.exp(inter - m_j)
        num = (jnp.einsum('bhjs,bhse->bhje', a_mat, vc)
               + a_int[..., None] * jnp.einsum('bhed,bhjd->bhje', c_st, qc))
        den = jnp.sum(a_mat, axis=-1) + a_int * jnp.einsum('bhd,bhjd->bhj', n_st, qc)
        h_out = num / jnp.maximum(jnp.abs(den), jnp.exp(-m_j))[..., None]
        b_last = bcum[..., -1]
        g = b_last[..., None] - bcum + li
        m_new = jnp.maximum(b_last + m_st, jnp.max(g, axis=-1))
        w_s = jnp.exp(g - m_new[..., None])
        decay = jnp.exp(b_last + m_st - m_new)
        c_new = decay[..., None, None] * c_st + jnp.einsum('bhs,bhse,bhsd->bhed', w_s, vc, kc)
        n_new = decay[..., None] * n_st + jnp.einsum('bhs,bhsd->bhd', w_s, kc)
        return (c_new, n_new, m_new), h_out

    init = (jnp.zeros((b, h, dh, dh), F32), jnp.zeros((b, h, dh), F32), jnp.zeros((b, h), F32))
    _, hs = lax.scan(step, init, (chunks(q), chunks(k), chunks(v), chunks(log_i), chunks(log_f)))
    return jnp.moveaxis(hs, 0, 2).reshape(b, h, s, dh)


def mlstm_mixer(q, k, v, o, i_pre, f_pre, conv_w, norm_g):
    b, s, _ = q.shape
    qk = jnp.concatenate([q, k], axis=-1)
    qk = lax.conv_general_dilated(qk, conv_w[:, None, :].astype(qk.dtype), window_strides=(1,),
                                  padding=[(MLSTM_CONV // 2, MLSTM_CONV // 2)],
                                  dimension_numbers=('NWC', 'WIO', 'NWC'),
                                  feature_group_count=2 * MLSTM_WIDTH)
    qk = jax.nn.silu(qk)
    q, k = jnp.split(qk, 2, axis=-1)

    def heads(a):
        return a.astype(F32).reshape(b, s, MLSTM_HEADS, HEAD_DIM).transpose(0, 2, 1, 3)

    qh, kh, vh = heads(q), heads(k) / math.sqrt(HEAD_DIM), heads(v)
    log_i = i_pre.astype(F32).reshape(b, s, 2, MLSTM_HEADS).transpose(2, 0, 3, 1)
    log_f = jax.nn.log_sigmoid(f_pre.astype(F32)).reshape(b, s, 2, MLSTM_HEADS).transpose(2, 0, 3, 1)
    flip = lambda a: jnp.flip(a, axis=2)
    h_fwd = mlstm_scan(qh, kh, vh, log_i[0], log_f[0])
    h_bwd = flip(mlstm_scan(flip(qh), flip(kh), flip(vh), flip(log_i[1]), flip(log_f[1])))
    hsum = h_fwd + h_bwd
    hsum = hsum * lax.rsqrt(jnp.mean(hsum * hsum, axis=-1, keepdims=True) + EPS)
    hsum = hsum.transpose(0, 2, 1, 3).reshape(b, s, MLSTM_WIDTH) * norm_g.astype(F32)
    return (hsum * jax.nn.sigmoid(o.astype(F32))).astype(q.dtype)


def attn_mixer(q, k, v, qn_g, kn_g, cos, sin):
    b, s, _ = q.shape
    group = ATTN_Q_HEADS // ATTN_KV_HEADS
    qh = apply_rope(head_rmsnorm(q.reshape(b, s, ATTN_Q_HEADS, HEAD_DIM), qn_g), cos, sin)
    kh = apply_rope(head_rmsnorm(k.reshape(b, s, ATTN_KV_HEADS, HEAD_DIM), kn_g), cos, sin)
    qh = (qh * HEAD_DIM ** -0.5).astype(q.dtype)
    kh = kh.astype(k.dtype).transpose(0, 2, 1, 3)
    vh = v.reshape(b, s, ATTN_KV_HEADS, HEAD_DIM).transpose(0, 2, 1, 3)
    nb = s // Q_BLOCK
    qb = qh.reshape(b, nb, Q_BLOCK, ATTN_KV_HEADS, group, HEAD_DIM).transpose(1, 0, 3, 4, 2, 5)

    def block(qi):
        sc = jnp.einsum('bhgqd,bhkd->bhgqk', qi, kh, preferred_element_type=F32)
        p = jax.nn.softmax(sc, axis=-1).astype(vh.dtype)
        return jnp.einsum('bhgqk,bhkd->bhgqd', p, vh)

    ob = lax.map(block, qb)
    return ob.transpose(1, 0, 4, 2, 3, 5).reshape(b, s, ATTN_Q_WIDTH)


def mem_cross_attn(hn, mem, wq, wkv, wo):
    b, s, _ = hn.shape
    q = (hn @ wq).reshape(b, s, CA_HEADS, CA_HEAD_DIM)
    kv = (mem @ wkv).reshape(b, mem.shape[1], 2, CA_HEADS, CA_HEAD_DIM)
    sc = jnp.einsum('bshd,bmhd->bhsm', q, kv[:, :, 0], preferred_element_type=F32) * CA_HEAD_DIM ** -0.5
    p = jax.nn.softmax(sc, axis=-1).astype(hn.dtype)
    o = jnp.einsum('bhsm,bmhd->bshd', p, kv[:, :, 1]).reshape(b, s, CA_WIDTH)
    return o @ wo


def ec_moe(h, router_w, w_gu, w_down):
    b, n, d = h.shape
    cap = EC_CAPACITY_FACTOR * n // N_EXPERTS
    logits = jnp.einsum('bnd,de->bne', h, router_w, preferred_element_type=F32)
    aff = jax.nn.softmax(logits, axis=-1)
    gates, idx = lax.top_k(jnp.swapaxes(aff, 1, 2), cap)
    xs = jax.vmap(lambda hb, ib: hb[ib])(h, idx)

    def expert(args):
        xe, wgu, wd = args
        gu = jnp.einsum('bcd,df->bcf', xe, wgu)
        ga, up = jnp.split(gu, 2, axis=-1)
        return jnp.einsum('bcf,fd->bcd', jax.nn.silu(ga) * up, wd)

    ys = lax.map(expert, (jnp.swapaxes(xs, 0, 1), w_gu, w_down))
    ys = jnp.swapaxes(ys, 0, 1) * gates[..., None].astype(ys.dtype)
    return jax.vmap(lambda ib, yb: jnp.zeros((n, d), yb.dtype).at[ib.reshape(-1)].add(yb.reshape(-1, d)))(idx, ys)


def setup_inputs(seed: int = 0) -> dict:
    key = jax.random.key(seed)
    ks = jax.random.split(key, 24)
    nrm = lambda k, shape, scale: jax.random.normal(k, shape, F32) * scale
    gain = lambda k, shape: 1.0 + 0.05 * jax.random.normal(k, shape, F32)
    gate_b = jnp.concatenate([nrm(ks[7], (DEPTH, 2 * MLSTM_HEADS), 0.1),
                              3.0 + nrm(ks[8], (DEPTH, 2 * MLSTM_HEADS), 0.5)], axis=-1)
    return {
        'x': nrm(ks[0], (BATCH, SEQ, D_MODEL), 1.0),
        'mem': nrm(ks[1], (BATCH, N_MEM, D_MODEL), 1.0),
        'norm_mix_g': gain(ks[2], (DEPTH, D_MODEL)),
        'w_in': nrm(ks[3], (DEPTH, D_MODEL, IN_COLS), D_MODEL ** -0.5),
        'pool_w': nrm(ks[4], (DEPTH, len(POOL_WINDOWS), POOL_GROUP, POOL_GROUP), POOL_GROUP ** -0.5),
        'pool_scale': gain(ks[5], (DEPTH, POOL_WIDTH)),
        'mlstm_conv_w': nrm(ks[6], (DEPTH, MLSTM_CONV, 2 * MLSTM_WIDTH), MLSTM_CONV ** -0.5),
        'mlstm_gate_b': gate_b,
        'mlstm_norm_g': gain(ks[9], (DEPTH, MLSTM_WIDTH)),
        'q_norm_g': gain(ks[10], (DEPTH, HEAD_DIM)),
        'k_norm_g': gain(ks[11], (DEPTH, HEAD_DIM)),
        'w_out': nrm(ks[12], (DEPTH, MIX_WIDTH, D_MODEL), MIX_WIDTH ** -0.5),
        'norm_mem_g': gain(ks[13], (DEPTH, D_MODEL)),
        'ca_wq': nrm(ks[14], (DEPTH, D_MODEL, CA_WIDTH), D_MODEL ** -0.5),
        'ca_wkv': nrm(ks[15], (DEPTH, D_MODEL, 2 * CA_WIDTH), D_MODEL ** -0.5),
        'ca_wo': nrm(ks[16], (DEPTH, CA_WIDTH, D_MODEL), CA_WIDTH ** -0.5),
        'norm_ffn_g': gain(ks[17], (DEPTH, D_MODEL)),
        'router_w': nrm(ks[18], (DEPTH, D_MODEL, N_EXPERTS), D_MODEL ** -0.5),
        'expert_w_gu': nrm(ks[19], (DEPTH, N_EXPERTS, D_MODEL, 2 * D_FF_EXPERT), D_MODEL ** -0.5),
        'expert_w_down': nrm(ks[20], (DEPTH, N_EXPERTS, D_FF_EXPERT, D_MODEL), D_FF_EXPERT ** -0.5),
        'final_norm_g': gain(ks[21], (D_MODEL,)),
    }


def reference(x, mem, norm_mix_g, w_in, pool_w, pool_scale, mlstm_conv_w, mlstm_gate_b,
              mlstm_norm_g, q_norm_g, k_norm_g, w_out, norm_mem_g, ca_wq, ca_wkv, ca_wo,
              norm_ffn_g, router_w, expert_w_gu, expert_w_down, final_norm_g):
    s = x.shape[1]
    cos, sin = axial_rope_tables(s)
    offsets = np.cumsum(IN_SPLITS)[:-1].tolist()
    n_gate = 2 * MLSTM_HEADS
    for layer in range(DEPTH):
        hn = rmsnorm(x, norm_mix_g[layer])
        proj = hn @ w_in[layer]
        u_pool, m_q, m_k, m_v, m_o, m_i, m_f, a_q, a_k, a_v = jnp.split(proj, offsets, axis=-1)
        gb = mlstm_gate_b[layer]
        y_pool = pool_mixer(u_pool, pool_w[layer], pool_scale[layer])
        y_mlstm = mlstm_mixer(m_q, m_k, m_v, m_o, m_i + gb[:n_gate], m_f + gb[n_gate:],
                              mlstm_conv_w[layer], mlstm_norm_g[layer])
        y_attn = attn_mixer(a_q, a_k, a_v, q_norm_g[layer], k_norm_g[layer], cos, sin)
        x = x + jnp.concatenate([y_pool, y_mlstm, y_attn], axis=-1) @ w_out[layer]
        x = x + mem_cross_attn(rmsnorm(x, norm_mem_g[layer]), mem, ca_wq[layer], ca_wkv[layer], ca_wo[layer])
        x = x + ec_moe(rmsnorm(x, norm_ffn_g[layer]), router_w[layer], expert_w_gu[layer], expert_w_down[layer])
    return rmsnorm(x, final_norm_g)
```

```python
import functools
import math

import jax
import jax.numpy as jnp
from jax import lax
from jax.experimental import pallas as pl
from jax.experimental.pallas import tpu as pltpu

F32 = jnp.float32
I32 = jnp.int32
MXU_DTYPE = jnp.bfloat16

LANES = 128
SUBLANES = 8
VMEM_LIMIT_BYTES = 56 * 1024 * 1024

EPS = 1e-6
GRID_W = 64
HEAD_DIM = 64
POOL_WINDOWS = (2, 4, 8, 16)
POOL_HALO = 8
MLSTM_CHUNK = 128
MLSTM_CONV = 5
CONV_HALO = 8
ROPE_THETA = 10000.0
CA_HEADS = 4
N_EXPERTS = 16
EC_CAPACITY_FACTOR = 2
NEG_INF = float("-inf")


def _cparams(*sem):
    return pltpu.CompilerParams(dimension_semantics=sem, vmem_limit_bytes=VMEM_LIMIT_BYTES)


def _mm(a, b):
    return jnp.dot(a.astype(MXU_DTYPE), b.astype(MXU_DTYPE), preferred_element_type=F32)


def _mm_nt(a, b):
    return lax.dot_general(a.astype(MXU_DTYPE), b.astype(MXU_DTYPE), (((1,), (1,)), ((), ())),
                           preferred_element_type=F32)


def _mm_tn(a, b):
    return lax.dot_general(a.astype(MXU_DTYPE), b.astype(MXU_DTYPE), (((0,), (0,)), ((), ())),
                           preferred_element_type=F32)


def _split3(x):
    hi = x.astype(MXU_DTYPE)
    r1 = x - hi.astype(F32)
    mid = r1.astype(MXU_DTYPE)
    lo = (r1 - mid.astype(F32)).astype(MXU_DTYPE)
    return hi, mid, lo


def _rms(x, g):
    return x * lax.rsqrt(jnp.mean(x * x, axis=-1, keepdims=True) + EPS) * g


def _from_token_tiles(ref, rows):
    return jnp.concatenate([ref[pl.ds(s, rows, stride=SUBLANES), :] for s in range(SUBLANES)], axis=1)


def _to_token_tiles(ref, val, rows):
    for s in range(SUBLANES):
        ref[pl.ds(s, rows, stride=SUBLANES), :] = val[:, s * LANES:(s + 1) * LANES]


def _inproj_body(x_ref, g_ref, w_ref, wg_ref, o_ref, og_ref):
    hb = _rms(x_ref[...], g_ref[...]).astype(MXU_DTYPE)
    o_ref[...] = jnp.dot(hb, w_ref[...], preferred_element_type=F32)
    og_ref[...] = jnp.dot(hb, wg_ref[...], preferred_element_type=F32)


def _inproj_merge_body(x_ref, moe_ref, g_ref, w_ref, wg_ref, xo_ref, o_ref, og_ref, *, tm):
    x = x_ref[...] + _from_token_tiles(moe_ref, tm)
    xo_ref[...] = x
    hb = _rms(x, g_ref[...]).astype(MXU_DTYPE)
    o_ref[...] = jnp.dot(hb, w_ref[...], preferred_element_type=F32)
    og_ref[...] = jnp.dot(hb, wg_ref[...], preferred_element_type=F32)


def _inproj(x2d, moe_tiles, g, w_main, w_gate, tm):
    t, d = x2d.shape
    n = w_main.shape[1]
    grid = (t // tm,)
    row = lambda i: (i, 0)
    fixed = lambda i: (0, 0)
    gw = w_gate.shape[1]
    w_specs = [pl.BlockSpec((1, d), fixed), pl.BlockSpec((d, n), fixed), pl.BlockSpec((d, gw), fixed)]
    o_specs = [pl.BlockSpec((tm, n), row), pl.BlockSpec((tm, gw), row)]
    o_shapes = [jax.ShapeDtypeStruct((t, n), F32), jax.ShapeDtypeStruct((t, gw), F32)]
    if moe_tiles is None:
        proj, gates = pl.pallas_call(
            _inproj_body, grid=grid, in_specs=[pl.BlockSpec((tm, d), row)] + w_specs, out_specs=o_specs,
            out_shape=o_shapes, compiler_params=_cparams("parallel"), name="inproj",
        )(x2d, g, w_main, w_gate)
        return x2d, proj, gates
    xo, proj, gates = pl.pallas_call(
        functools.partial(_inproj_merge_body, tm=tm), grid=grid,
        in_specs=[pl.BlockSpec((tm, d), row), pl.BlockSpec((tm * SUBLANES, LANES), row)] + w_specs,
        out_specs=[pl.BlockSpec((tm, d), row)] + o_specs,
        out_shape=[jax.ShapeDtypeStruct((t, d), F32)] + o_shapes,
        compiler_params=_cparams("parallel"), name="inproj_merge",
    )(x2d, moe_tiles, g, w_main, w_gate)
    return xo, proj, gates


def _pool_body(u_ref, w_ref, sc_ref, o_ref, upad_ref, *, seq, tc):
    i = pl.program_id(1)

    @pl.when(i == 0)
    def _():
        zeros = jnp.zeros((POOL_HALO, u_ref.shape[1]), F32)
        upad_ref[0:POOL_HALO, :] = zeros
        upad_ref[POOL_HALO + seq:POOL_HALO + seq + POOL_HALO, :] = zeros
        upad_ref[POOL_HALO:POOL_HALO + seq, :] = u_ref[...]

    t0 = pl.multiple_of(i * tc, tc)
    win_rows = tc + 2 * POOL_HALO
    t = t0 + lax.broadcasted_iota(I32, (tc, LANES), 0)
    lane = lax.broadcasted_iota(I32, (tc, LANES), 1)
    upper = lane >= (LANES // 2)
    halves = []
    for hf in range(2):
        w_lo, w_hi = POOL_WINDOWS[2 * hf], POOL_WINDOWS[2 * hf + 1]
        win = upad_ref[pl.ds(t0, win_rows), hf * LANES:(hf + 1) * LANES]
        acc = jnp.zeros((tc, LANES), F32)
        for j in range(-(w_hi // 2), w_hi // 2):
            piece = pltpu.roll(win, (-j) % win_rows, 0)[POOL_HALO:POOL_HALO + tc]
            if -(w_lo // 2) <= j < w_lo // 2:
                acc = acc + piece
            else:
                acc = acc + jnp.where(upper, piece, 0.0)
        half_w = jnp.where(upper, w_hi // 2, w_lo // 2)
        cnt = jnp.minimum(t + half_w, seq) - jnp.maximum(t - half_w, 0)
        halves.append(acc / cnt.astype(F32) - win[POOL_HALO:POOL_HALO + tc])
    d = jnp.concatenate(halves, axis=1)
    o_ref[...] = _mm(d, w_ref[...]) * sc_ref[...]


def _pool_mixer(proj3, w_bd, scale, col_block, tc):
    b, seq, _ = proj3.shape
    pw = w_bd.shape[0]
    return pl.pallas_call(
        functools.partial(_pool_body, seq=seq, tc=tc), grid=(b, seq // tc),
        in_specs=[pl.BlockSpec((None, seq, pw), lambda bi, i: (bi, 0, col_block)),
                  pl.BlockSpec((pw, pw), lambda bi, i: (0, 0)),
                  pl.BlockSpec((1, pw), lambda bi, i: (0, 0))],
        out_specs=pl.BlockSpec((None, tc, pw), lambda bi, i: (bi, i, 0)),
        out_shape=jax.ShapeDtypeStruct((b, seq, pw), F32),
        scratch_shapes=[pltpu.VMEM((seq + 2 * POOL_HALO, pw), F32)],
        compiler_params=_cparams("parallel", "arbitrary"), name="pool_mixer",
    )(proj3, w_bd, scale)


def _mlstm_body(q_ref, k_ref, v_ref, o_ref, gt_ref, cwq_ref, cwk_ref, gb_ref, ng_ref, y_ref,
                qpad_ref, kpad_ref, qc_ref, kc_ref, hf_ref, hb_ref, c_ref, n_ref, m_ref, *, seq, n_heads):
    L = MLSTM_CHUNK
    nc = seq // L
    hd = HEAD_DIM

    zeros = jnp.zeros((CONV_HALO, LANES), F32)
    for pad_ref, src_ref in ((qpad_ref, q_ref), (kpad_ref, k_ref)):
        pad_ref[0:CONV_HALO, :] = zeros
        pad_ref[CONV_HALO + seq:CONV_HALO + seq + CONV_HALO, :] = zeros
        pad_ref[CONV_HALO:CONV_HALO + seq, :] = src_ref[...]
    win_rows = L + 2 * CONV_HALO

    def conv_chunk(ci, carry):
        t0 = pl.multiple_of(ci * L, L)
        for pad_ref, cw_ref, dst_ref, post in ((qpad_ref, cwq_ref, qc_ref, 1.0),
                                               (kpad_ref, cwk_ref, kc_ref, 1.0 / math.sqrt(hd))):
            win = pad_ref[pl.ds(t0, win_rows), :]
            acc = jnp.zeros((L, LANES), F32)
            for j in range(MLSTM_CONV):
                off = j - MLSTM_CONV // 2
                piece = pltpu.roll(win, (-off) % win_rows, 0)[CONV_HALO:CONV_HALO + L]
                acc = acc + piece * cw_ref[j:j + 1, :]
            act = acc * jax.nn.sigmoid(acc)
            dst_ref[pl.ds(t0, L), :] = act * post if post != 1.0 else act
        return carry

    lax.fori_loop(0, nc, conv_chunk, 0)

    c_ref[...] = jnp.zeros(c_ref.shape, F32)
    n_ref[...] = jnp.zeros(n_ref.shape, F32)
    m_ref[...] = jnp.zeros(m_ref.shape, F32)

    row = lax.broadcasted_iota(I32, (L, L), 0)
    col = lax.broadcasted_iota(I32, (L, L), 1)
    tri = ((col <= row).astype(MXU_DTYPE), (col >= row).astype(MXU_DTYPE))
    keep = (col <= row, col >= row)
    gate_lane = lax.broadcasted_iota(I32, (L, LANES), 1)
    is_i = gate_lane < 4
    is_f = (gate_lane >= 4) & (gate_lane < 8)
    gbias = gb_ref[...]

    def chunk_step(it, carry):
        for d in range(2):
            ci = it if d == 0 else nc - 1 - it
            t0 = pl.multiple_of(ci * L, L)
            gpre = gt_ref[pl.ds(t0, L), :] + gbias
            lf = jnp.where(is_f, jax.nn.log_sigmoid(gpre), 0.0)
            hi, mid, lo = _split3(lf)
            cum = (jnp.dot(tri[d], hi, preferred_element_type=F32)
                   + jnp.dot(tri[d], mid, preferred_element_type=F32)
                   + jnp.dot(tri[d], lo, preferred_element_type=F32))
            gm = jnp.where(is_i, gpre, cum)
            gmt = gm.T
            last = L - 1 if d == 0 else 0
            qc = qc_ref[pl.ds(t0, L), :]
            kc = kc_ref[pl.ds(t0, L), :]
            vc = v_ref[pl.ds(t0, L), :]
            outs = []
            for hh in range(2):
                sidx = hh * 2 + d
                ic = d * 2 + hh
                fc = 4 + d * 2 + hh
                li_col = gm[:, ic:ic + 1]
                b_col = gm[:, fc:fc + 1]
                li_row = gmt[ic:ic + 1, :]
                b_row = gmt[fc:fc + 1, :]
                m_st = m_ref[sidx][:, 0:1]
                c_st = c_ref[sidx]
                n_st = n_ref[sidx]
                qh = qc[:, hh * hd:(hh + 1) * hd]
                kh = kc[:, hh * hd:(hh + 1) * hd]
                vh = vc[:, hh * hd:(hh + 1) * hd]

                dmat = jnp.where(keep[d], b_col - b_row + li_row, NEG_INF)
                inter = b_col + m_st
                m_j = jnp.maximum(inter, jnp.max(dmat, axis=-1, keepdims=True))
                a_mat = jnp.exp(dmat - m_j) * _mm_nt(qh, kh)
                a_int = jnp.exp(inter - m_j)
                num = _mm(a_mat, vh) + a_int * _mm_nt(qh, c_st)
                den = (jnp.sum(a_mat, axis=-1, keepdims=True)
                       + a_int * jnp.sum(qh * n_st, axis=-1, keepdims=True))
                outs.append(num / jnp.maximum(jnp.abs(den), jnp.exp(-m_j)))

                b_last = b_col[last:last + 1, :]
                g_row = b_last - b_row + li_row
                g_col = b_last - b_col + li_col
                m_new = jnp.maximum(b_last + m_st, jnp.max(g_row, axis=-1, keepdims=True))
                w_col = jnp.exp(g_col - m_new)
                decay = jnp.exp(b_last + m_st - m_new)
                c_ref[sidx] = decay * c_st + _mm_tn(w_col * vh, kh)
                n_ref[sidx] = decay * n_st + jnp.sum(w_col * kh, axis=0, keepdims=True)
                m_ref[sidx] = jnp.broadcast_to(m_new, (1, LANES))
            dst = hf_ref if d == 0 else hb_ref
            dst[pl.ds(t0, L), :] = jnp.concatenate(outs, axis=1)
        return carry

    lax.fori_loop(0, nc, chunk_step, 0)

    ng = ng_ref[...]
    lane = lax.broadcasted_iota(I32, (L, LANES), 1)
    upper = lane >= hd

    def out_chunk(ci, carry):
        t0 = pl.multiple_of(ci * L, L)
        hs = hf_ref[pl.ds(t0, L), :] + hb_ref[pl.ds(t0, L), :]
        sq = hs * hs
        ms_lo = jnp.sum(jnp.where(upper, 0.0, sq), axis=-1, keepdims=True) / hd
        ms_hi = jnp.sum(jnp.where(upper, sq, 0.0), axis=-1, keepdims=True) / hd
        ms = jnp.where(upper, ms_hi, ms_lo)
        y = hs * lax.rsqrt(ms + EPS) * ng
        y_ref[pl.ds(t0, L), :] = y * jax.nn.sigmoid(o_ref[pl.ds(t0, L), :])
        return carry

    lax.fori_loop(0, nc, out_chunk, 0)


def _mlstm(proj3, gates3, conv_w, gate_bias_row, norm_g_row, n_heads, cols):
    b, seq, _ = proj3.shape
    qb, kb, vb, ob = cols
    n_pairs = n_heads // 2
    width = n_heads * HEAD_DIM
    blk = lambda base: pl.BlockSpec((None, seq, LANES), lambda bi, p, base=base: (bi, 0, base + p))
    k_conv_base = width // LANES
    return pl.pallas_call(
        functools.partial(_mlstm_body, seq=seq, n_heads=n_heads), grid=(b, n_pairs),
        in_specs=[blk(qb), blk(kb), blk(vb), blk(ob),
                  pl.BlockSpec((None, seq, LANES), lambda bi, p: (bi, 0, p)),
                  pl.BlockSpec((MLSTM_CONV, LANES), lambda bi, p: (0, p)),
                  pl.BlockSpec((MLSTM_CONV, LANES), lambda bi, p: (0, k_conv_base + p)),
                  pl.BlockSpec((1, LANES), lambda bi, p: (0, p)),
                  pl.BlockSpec((1, LANES), lambda bi, p: (0, p))],
        out_specs=pl.BlockSpec((None, seq, LANES), lambda bi, p: (bi, 0, p)),
        out_shape=jax.ShapeDtypeStruct((b, seq, width), F32),
        scratch_shapes=[pltpu.VMEM((seq + 2 * CONV_HALO, LANES), F32), pltpu.VMEM((seq + 2 * CONV_HALO, LANES), F32),
                        pltpu.VMEM((seq, LANES), F32), pltpu.VMEM((seq, LANES), F32),
                        pltpu.VMEM((seq, LANES), F32), pltpu.VMEM((seq, LANES), F32),
                        pltpu.VMEM((4, HEAD_DIM, HEAD_DIM), F32), pltpu.VMEM((4, 1, HEAD_DIM), F32),
                        pltpu.VMEM((4, 1, LANES), F32)],
        compiler_params=_cparams("parallel", "parallel"), name="mlstm",
    )(proj3, proj3, proj3, proj3, gates3, conv_w, conv_w, gate_bias_row, norm_g_row)


def _group_mean_sq(x, gmat):
    hi = (x * x).astype(MXU_DTYPE)
    lo = (x * x - hi.astype(F32)).astype(MXU_DTYPE)
    ssq = jnp.dot(hi, gmat, preferred_element_type=F32) + jnp.dot(lo, gmat, preferred_element_type=F32)
    return ssq * (1.0 / HEAD_DIM)


def _norm_rope(x, g_row, gmat, cos_t, sin_t):
    width = x.shape[1]
    xn = x * lax.rsqrt(_group_mean_sq(x, gmat) + EPS) * g_row
    lane = lax.broadcasted_iota(I32, x.shape, 1)
    partner = jnp.where(lane % 2 == 0, pltpu.roll(xn, width - 1, 1), pltpu.roll(xn, 1, 1))
    return xn * cos_t + partner * sin_t


def _qkprep_body(q_ref, k_ref, v_ref, qg_ref, kg_ref, gq_ref, gk_ref, cos_ref, sin_ref, qo_ref, kt_ref, vo_ref,
                 *, n_q_heads, n_kv_heads):
    hd = HEAD_DIM
    cos2, sin2 = cos_ref[...], sin_ref[...]
    reps = n_q_heads * hd // LANES
    q = _norm_rope(q_ref[...], qg_ref[...], gq_ref[...],
                   jnp.concatenate([cos2] * reps, axis=1), jnp.concatenate([sin2] * reps, axis=1))
    q = q * (hd ** -0.5)
    for h in range(n_q_heads):
        qo_ref[h] = q[:, h * hd:(h + 1) * hd].astype(qo_ref.dtype)
    k = _norm_rope(k_ref[...], kg_ref[...], gk_ref[...], cos2, sin2)
    kt = k.T
    v = v_ref[...]
    for h in range(n_kv_heads):
        kt_ref[h] = kt[h * hd:(h + 1) * hd, :].astype(kt_ref.dtype)
        vo_ref[h] = v[:, h * hd:(h + 1) * hd].astype(vo_ref.dtype)


def _qkprep(proj3, q_gain_row, k_gain_row, gq, gk, cos2, sin2, n_q_heads, n_kv_heads, cols, ts):
    b, seq, _ = proj3.shape
    qcol, kcol, vcol = cols
    qw, kw = n_q_heads * HEAD_DIM, n_kv_heads * HEAD_DIM
    fixed = lambda bi, i: (0, 0)
    return pl.pallas_call(
        functools.partial(_qkprep_body, n_q_heads=n_q_heads, n_kv_heads=n_kv_heads), grid=(b, seq // ts),
        in_specs=[pl.BlockSpec((None, ts, qw), lambda bi, i: (bi, i, qcol)),
                  pl.BlockSpec((None, ts, kw), lambda bi, i: (bi, i, kcol)),
                  pl.BlockSpec((None, ts, kw), lambda bi, i: (bi, i, vcol)),
                  pl.BlockSpec((1, qw), fixed), pl.BlockSpec((1, kw), fixed),
                  pl.BlockSpec((qw, qw), fixed), pl.BlockSpec((kw, kw), fixed),
                  pl.BlockSpec((ts, kw), lambda bi, i: (i, 0)), pl.BlockSpec((ts, kw), lambda bi, i: (i, 0))],
        out_specs=[pl.BlockSpec((None, n_q_heads, ts, HEAD_DIM), lambda bi, i: (bi, 0, i, 0)),
                   pl.BlockSpec((None, n_kv_heads, HEAD_DIM, ts), lambda bi, i: (bi, 0, 0, i)),
                   pl.BlockSpec((None, n_kv_heads, ts, HEAD_DIM), lambda bi, i: (bi, 0, i, 0))],
        out_shape=[jax.ShapeDtypeStruct((b, n_q_heads, seq, HEAD_DIM), MXU_DTYPE),
                   jax.ShapeDtypeStruct((b, n_kv_heads, HEAD_DIM, seq), MXU_DTYPE),
                   jax.ShapeDtypeStruct((b, n_kv_heads, seq, HEAD_DIM), MXU_DTYPE)],
        compiler_params=_cparams("parallel", "parallel"), name="qk_prep",
    )(proj3, proj3, proj3, q_gain_row, k_gain_row, gq, gk, cos2, sin2)


def _flash_body(q_ref, kt_ref, v_ref, o_ref, *, seq, tq, tk, group):
    hd = HEAD_DIM
    q = q_ref[...].reshape(group * tq, hd)

    def kv_step(kc, carry):
        m, l, acc = carry
        k0 = pl.multiple_of(kc * tk, tk)
        s = jnp.dot(q, kt_ref[:, pl.ds(k0, tk)], preferred_element_type=F32)
        m_new = jnp.maximum(m, jnp.max(s, axis=-1, keepdims=True))
        alpha = jnp.exp(m - m_new)
        p = jnp.exp(s - m_new)
        l = alpha * l + jnp.sum(p, axis=-1, keepdims=True)
        acc = alpha * acc + jnp.dot(p.astype(v_ref.dtype), v_ref[pl.ds(k0, tk), :], preferred_element_type=F32)
        return m_new, l, acc

    init = (jnp.full((group * tq, 1), NEG_INF, F32), jnp.zeros((group * tq, 1), F32),
            jnp.zeros((group * tq, hd), F32))
    _, l, acc = lax.fori_loop(0, seq // tk, kv_step, init)
    o = acc / l
    o_ref[...] = jnp.concatenate([o[g * tq:(g + 1) * tq] for g in range(group)], axis=1)


def _flash_attention(q4, kt4, v4, tq, tk):
    b, n_q_heads, seq, hd = q4.shape
    n_kv_heads = kt4.shape[1]
    group = n_q_heads // n_kv_heads
    return pl.pallas_call(
        functools.partial(_flash_body, seq=seq, tq=tq, tk=tk, group=group), grid=(b, n_kv_heads, seq // tq),
        in_specs=[pl.BlockSpec((None, group, tq, hd), lambda bi, h, i: (bi, h, i, 0)),
                  pl.BlockSpec((None, None, hd, seq), lambda bi, h, i: (bi, h, 0, 0)),
                  pl.BlockSpec((None, None, seq, hd), lambda bi, h, i: (bi, h, 0, 0))],
        out_specs=pl.BlockSpec((None, tq, group * hd), lambda bi, h, i: (bi, i, h)),
        out_shape=jax.ShapeDtypeStruct((b, seq, n_q_heads * hd), F32),
        compiler_params=_cparams("parallel", "parallel", "parallel"), name="flash_gqa",
    )(q4, kt4, v4)


def _outproj_body(x_ref, yp_ref, ym_ref, ya_ref, wp_ref, wm_ref, wa_ref, o_ref):
    o_ref[...] = (x_ref[...] + _mm(yp_ref[...], wp_ref[...]) + _mm(ym_ref[...], wm_ref[...])
                  + _mm(ya_ref[...], wa_ref[...]))


def _outproj(x2d, yp, ym, ya, w_out, tm):
    t, d = x2d.shape
    row = lambda i: (i, 0)
    fixed = lambda i: (0, 0)
    wp, wm, wa = yp.shape[1], ym.shape[1], ya.shape[1]
    return pl.pallas_call(
        _outproj_body, grid=(t // tm,),
        in_specs=[pl.BlockSpec((tm, d), row), pl.BlockSpec((tm, wp), row), pl.BlockSpec((tm, wm), row),
                  pl.BlockSpec((tm, wa), row), pl.BlockSpec((wp, d), fixed), pl.BlockSpec((wm, d), fixed),
                  pl.BlockSpec((wa, d), fixed)],
        out_specs=pl.BlockSpec((tm, d), row), out_shape=jax.ShapeDtypeStruct((t, d), F32),
        compiler_params=_cparams("parallel"), name="outproj",
    )(x2d, yp, ym, ya, w_out[:wp], w_out[wp:wp + wm], w_out[wp + wm:])


def _kvproj_body(m_ref, w_ref, o_ref):
    o_ref[...] = _mm(m_ref[...], w_ref[...]).astype(o_ref.dtype)


def _kvproj(mem2d, wkv, tm):
    t, d = mem2d.shape
    n = wkv.shape[1]
    return pl.pallas_call(
        _kvproj_body, grid=(t // tm,),
        in_specs=[pl.BlockSpec((tm, d), lambda i: (i, 0)), pl.BlockSpec((d, n), lambda i: (0, 0))],
        out_specs=pl.BlockSpec((tm, n), lambda i: (i, 0)), out_shape=jax.ShapeDtypeStruct((t, n), MXU_DTYPE),
        compiler_params=_cparams("parallel"), name="mem_kv_proj",
    )(mem2d, wkv)


def _crossattn_body(x_ref, kv_ref, gm_ref, wq_ref, wo_ref, gf_ref, rwt_ref, xo_ref, hf_ref, lg_ref, *, ts):
    x = x_ref[...]
    q = _mm(_rms(x, gm_ref[...]), wq_ref[...])
    ca_width = q.shape[1]
    dh = ca_width // CA_HEADS
    kv = kv_ref[...]
    heads = []
    for h in range(CA_HEADS):
        s = _mm_nt(q[:, h * dh:(h + 1) * dh], kv[:, h * dh:(h + 1) * dh]) * (dh ** -0.5)
        e = jnp.exp(s - jnp.max(s, axis=-1, keepdims=True))
        p = e / jnp.sum(e, axis=-1, keepdims=True)
        heads.append(_mm(p, kv[:, ca_width + h * dh:ca_width + (h + 1) * dh]))
    x2 = x + _mm(jnp.concatenate(heads, axis=1), wo_ref[...])
    xo_ref[...] = x2
    hf = _rms(x2, gf_ref[...])
    _to_token_tiles(hf_ref, hf, ts)
    lg_ref[...] = _mm_nt(rwt_ref[...], hf)


def _crossattn(x3, kv3, g_mem, wq, wo, g_ffn, router_wt, ts):
    b, seq, d = x3.shape
    n_mem, kvw = kv3.shape[1], kv3.shape[2]
    ne = router_wt.shape[0]
    fixed = lambda bi, i: (0, 0)
    return pl.pallas_call(
        functools.partial(_crossattn_body, ts=ts), grid=(b, seq // ts),
        in_specs=[pl.BlockSpec((None, ts, d), lambda bi, i: (bi, i, 0)),
                  pl.BlockSpec((None, n_mem, kvw), lambda bi, i: (bi, 0, 0)),
                  pl.BlockSpec((1, d), fixed), pl.BlockSpec(wq.shape, fixed), pl.BlockSpec(wo.shape, fixed),
                  pl.BlockSpec((1, d), fixed), pl.BlockSpec((ne, d), fixed)],
        out_specs=[pl.BlockSpec((None, ts, d), lambda bi, i: (bi, i, 0)),
                   pl.BlockSpec((None, ts * SUBLANES, LANES), lambda bi, i: (bi, i, 0)),
                   pl.BlockSpec((None, ne, ts), lambda bi, i: (bi, 0, i))],
        out_shape=[jax.ShapeDtypeStruct((b, seq, d), F32),
                   jax.ShapeDtypeStruct((b, seq * SUBLANES, LANES), F32),
                   jax.ShapeDtypeStruct((b, ne, seq), F32)],
        compiler_params=_cparams("parallel", "parallel"), name="cross_attn",
    )(x3, kv3, g_mem, wq, wo, g_ffn, router_wt)


def _excl_prefix(flags, upper_excl, ones):
    rows, seq = flags.shape
    off = jnp.zeros((rows, LANES), F32)
    out = []
    for c in range(seq // LANES):
        xc = flags[:, c * LANES:(c + 1) * LANES].astype(MXU_DTYPE)
        out.append(jnp.dot(xc, upper_excl, preferred_element_type=F32) + off)
        off = off + jnp.dot(xc, ones, preferred_element_type=F32)
    return jnp.concatenate(out, axis=1)


def _router_body(lg_ref, idx_ref, gate_ref, *, seq, cap):
    lg = lg_ref[...]
    ne = lg.shape[0]
    ex = jnp.exp(lg - jnp.max(lg, axis=0, keepdims=True))
    aff = ex / jnp.sum(ex, axis=0, keepdims=True)
    bits = pltpu.bitcast(aff, I32)

    thr = jnp.zeros((ne, 1), I32)
    for bit in range(30, -1, -1):
        cand = thr | (1 << bit)
        cnt = jnp.sum((bits >= cand).astype(F32), axis=1, keepdims=True)
        thr = jnp.where(cnt >= cap, cand, thr)

    r_i = lax.broadcasted_iota(I32, (LANES, LANES), 0)
    c_i = lax.broadcasted_iota(I32, (LANES, LANES), 1)
    upper_excl = (r_i < c_i).astype(MXU_DTYPE)
    ones = jnp.ones((LANES, LANES), MXU_DTYPE)

    gt = bits > thr
    eq = bits == thr
    n_gt = jnp.sum(gt.astype(F32), axis=1, keepdims=True)
    tie_rank = _excl_prefix(eq.astype(F32), upper_excl, ones)
    sel = gt | (eq & (tie_rank < cap - n_gt))
    pos = _excl_prefix(sel.astype(F32), upper_excl, ones).astype(I32)

    n_iota = lax.broadcasted_iota(I32, (ne, seq), 1)
    seq_bits = (seq - 1).bit_length()
    flag_bit = 2 * seq_bits
    packed = jnp.where(sel, n_iota | ((n_iota - pos) << seq_bits) | (1 << flag_bit), 0)
    gbits = jnp.where(sel, bits, 0)
    for k in range(seq_bits):
        sh = 1 << k
        moved_p = pltpu.roll(packed, seq - sh, 1)
        moved_g = pltpu.roll(gbits, seq - sh, 1)
        take = ((moved_p >> flag_bit) & 1) * ((moved_p >> (seq_bits + k)) & 1) == 1
        stay = ((packed >> flag_bit) & 1) * (1 - ((packed >> (seq_bits + k)) & 1)) == 1
        packed = jnp.where(take, moved_p, jnp.where(stay, packed, 0))
        gbits = jnp.where(take, moved_g, jnp.where(stay, gbits, 0))
    idx_ref[...] = packed[:, :cap] & (seq - 1)
    gate_ref[...] = pltpu.bitcast(gbits[:, :cap], F32)


def _router(logits_t, cap):
    b, ne, seq = logits_t.shape
    return pl.pallas_call(
        functools.partial(_router_body, seq=seq, cap=cap), grid=(b,),
        in_specs=[pl.BlockSpec((None, ne, seq), lambda bi: (bi, 0, 0))],
        out_specs=[pl.BlockSpec((None, ne, cap), lambda bi: (bi, 0, 0)),
                   pl.BlockSpec((None, ne, cap), lambda bi: (bi, 0, 0))],
        out_shape=[jax.ShapeDtypeStruct((b, ne, cap), I32), jax.ShapeDtypeStruct((b, ne, cap), F32)],
        compiler_params=_cparams("parallel"), name="ec_router",
    )(logits_t)


def _moe_body(idx_hbm, gate_hbm, h_hbm, wg_ref, wu_ref, wd_ref, out_hbm,
              idx_s, gate_s, xbuf, xbf, ybuf, yacc, acc, gsem, isem, osem, *, seq, cap, ne, n_fc):
    b = pl.program_id(0)
    e = pl.program_id(1)
    c = pl.program_id(2)
    n_steps = pl.num_programs(0) * ne
    step = b * ne + e
    slot = lax.rem(step, 2)
    tile = SUBLANES

    def idx_copies(s, sl):
        return (pltpu.make_async_copy(idx_hbm.at[s], idx_s.at[sl], isem.at[0, sl]),
                pltpu.make_async_copy(gate_hbm.at[s], gate_s.at[sl], isem.at[1, sl]))

    def row_copy(src_row, k, sl):
        return pltpu.make_async_copy(h_hbm.at[pl.ds(src_row, tile), :],
                                     xbuf.at[sl, pl.ds(k * tile, tile), :], gsem.at[sl])

    def start_gather(s, sl):
        base = (s // ne) * seq

        def body(k, carry):
            row_copy(pl.multiple_of((base + idx_s[sl, 0, k]) * tile, tile), k, sl).start()
            return carry
        lax.fori_loop(0, cap, body, 0, unroll=8)

    def wait_gather(sl):
        def body(k, carry):
            row_copy(0, k, sl).wait()
            return carry
        lax.fori_loop(0, cap, body, 0, unroll=8)

    @pl.when(c == 0)
    def _():
        @pl.when(step == 0)
        def _():
            for cp in idx_copies(0, 0):
                cp.start()
            for cp in idx_copies(0, 0):
                cp.wait()
            start_gather(0, 0)

        @pl.when(step + 1 < n_steps)
        def _():
            for cp in idx_copies(step + 1, 1 - slot):
                cp.start()
            for cp in idx_copies(step + 1, 1 - slot):
                cp.wait()
            start_gather(step + 1, 1 - slot)

        wait_gather(slot)
        xbf[...] = _from_token_tiles(xbuf.at[slot], cap).astype(xbf.dtype)

        @pl.when(e == 0)
        def _():
            acc[...] = jnp.zeros(acc.shape, F32)

    x = xbf[...]
    g = jnp.dot(x, wg_ref[...], preferred_element_type=F32)
    u = jnp.dot(x, wu_ref[...], preferred_element_type=F32)
    y = _mm(g * jax.nn.sigmoid(g) * u, wd_ref[...])

    @pl.when(c == 0)
    def _():
        yacc[...] = y

    @pl.when(c > 0)
    def _():
        yacc[...] += y

    @pl.when(c == n_fc - 1)
    def _():
        _to_token_tiles(ybuf, yacc[...], cap)

        def body(k, carry):
            r = pl.multiple_of(idx_s[slot, 0, k] * tile, tile)
            acc[pl.ds(r, tile), :] += gate_s[slot, 0, k] * ybuf[pl.ds(k * tile, tile), :]
            return carry
        lax.fori_loop(0, cap, body, 0, unroll=8)

        @pl.when(e == ne - 1)
        def _():
            cp = pltpu.make_async_copy(acc, out_hbm.at[b], osem)
            cp.start()
            cp.wait()


def _moe(idx3, gate3, h_tiles, w_gu, w_down, batch, seq, fc):
    n_be, _, cap = idx3.shape
    ne, d, f2 = w_gu.shape
    f = f2 // 2
    n_fc = f // fc
    return pl.pallas_call(
        functools.partial(_moe_body, seq=seq, cap=cap, ne=ne, n_fc=n_fc), grid=(batch, ne, n_fc),
        in_specs=[pl.BlockSpec(memory_space=pl.ANY), pl.BlockSpec(memory_space=pl.ANY),
                  pl.BlockSpec(memory_space=pl.ANY),
                  pl.BlockSpec((None, d, fc), lambda b, e, c: (e, 0, c)),
                  pl.BlockSpec((None, d, fc), lambda b, e, c: (e, 0, n_fc + c)),
                  pl.BlockSpec((None, fc, d), lambda b, e, c: (e, c, 0))],
        out_specs=pl.BlockSpec(memory_space=pl.ANY),
        out_shape=jax.ShapeDtypeStruct((batch, seq * SUBLANES, LANES), F32),
        scratch_shapes=[pltpu.SMEM((2, 1, cap), I32), pltpu.SMEM((2, 1, cap), F32),
                        pltpu.VMEM((2, cap * SUBLANES, LANES), F32), pltpu.VMEM((cap, d), MXU_DTYPE),
                        pltpu.VMEM((cap * SUBLANES, LANES), F32), pltpu.VMEM((cap, d), F32),
                        pltpu.VMEM((seq * SUBLANES, LANES), F32),
                        pltpu.SemaphoreType.DMA((2,)), pltpu.SemaphoreType.DMA((2, 2)), pltpu.SemaphoreType.DMA],
        compiler_params=_cparams("arbitrary", "arbitrary", "arbitrary"), name="ec_moe",
    )(idx3, gate3, h_tiles, w_gu, w_gu, w_down)


def _final_body(x_ref, moe_ref, g_ref, o_ref, *, tm):
    o_ref[...] = _rms(x_ref[...] + _from_token_tiles(moe_ref, tm), g_ref[...])


def _final_norm(x2d, moe_tiles, g, tm):
    t, d = x2d.shape
    row = lambda i: (i, 0)
    return pl.pallas_call(
        functools.partial(_final_body, tm=tm), grid=(t // tm,),
        in_specs=[pl.BlockSpec((tm, d), row), pl.BlockSpec((tm * SUBLANES, LANES), row),
                  pl.BlockSpec((1, d), lambda i: (0, 0))],
        out_specs=pl.BlockSpec((tm, d), row), out_shape=jax.ShapeDtypeStruct((t, d), F32),
        compiler_params=_cparams("parallel"), name="final_norm",
    )(x2d, moe_tiles, g)


def _rope_tables(seq):
    rows = seq // GRID_W
    row_ids = jnp.repeat(jnp.arange(rows), GRID_W).astype(F32)
    col_ids = jnp.tile(jnp.arange(GRID_W), rows).astype(F32)
    n_freq = HEAD_DIM // 4
    inv_freq = ROPE_THETA ** (-jnp.arange(n_freq, dtype=F32) / n_freq)
    ang = jnp.concatenate([row_ids[:, None] * inv_freq, col_ids[:, None] * inv_freq], axis=-1)
    cos, sin = jnp.cos(ang), jnp.sin(ang)
    cos_i = jnp.repeat(cos, 2, axis=1)
    sin_i = jnp.stack([-sin, sin], axis=-1).reshape(seq, HEAD_DIM)
    return jnp.tile(cos_i, (1, LANES // HEAD_DIM)), jnp.tile(sin_i, (1, LANES // HEAD_DIM))


def _group_matrix(width):
    gid = jnp.arange(width) // HEAD_DIM
    return (gid[:, None] == gid[None, :]).astype(MXU_DTYPE)


def _block_diag(w):
    g, c, _ = w.shape
    eye = jnp.eye(g, dtype=w.dtype)
    return (eye[:, None, :, None] * w[:, :, None, :]).reshape(g * c, g * c)


def _pick_tile(n, pref):
    t = min(n, pref)
    while n % t:
        t //= 2
    return t


def kernel(x, mem, norm_mix_g, w_in, pool_w, pool_scale, mlstm_conv_w, mlstm_gate_b, mlstm_norm_g, q_norm_g,
           k_norm_g, w_out, norm_mem_g, ca_wq, ca_wkv, ca_wo, norm_ffn_g, router_w, expert_w_gu, expert_w_down,
           final_norm_g):
    batch, seq, d = x.shape
    depth = w_in.shape[0]
    n_mem = mem.shape[1]
    pool_width = pool_scale.shape[1]
    m_width = mlstm_norm_g.shape[1]
    m_heads = m_width // HEAD_DIM
    n_gate = 2 * m_heads
    kv_width = (w_in.shape[2] - pool_width - 4 * m_width - 2 * n_gate) // 6
    q_width = 4 * kv_width
    n_q_heads, n_kv_heads = q_width // HEAD_DIM, kv_width // HEAD_DIM
    ne = router_w.shape[2]
    cap = EC_CAPACITY_FACTOR * seq // ne
    f_exp = expert_w_down.shape[2]
    t = batch * seq

    o_pool = 0
    o_mq = o_pool + pool_width
    o_mi = o_mq + 4 * m_width
    o_aq = o_mi + 2 * n_gate
    o_ak = o_aq + q_width
    c_pool = q_width // pool_width
    c_mq = (q_width + pool_width) // LANES
    m_cols = tuple(c_mq + j * (m_width // LANES) for j in range(4))
    c_ak = (q_width + pool_width + 4 * m_width) // kv_width
    cos2, sin2 = _rope_tables(seq)
    gq, gk = _group_matrix(q_width), _group_matrix(kv_width)
    n_pairs = m_heads // 2
    gate_cols = [[kind * n_gate + dr * m_heads + 2 * p + hh for kind in range(2) for dr in range(2) for hh in range(2)]
                 for p in range(n_pairs)]

    tm = _pick_tile(t, 512)
    ts = _pick_tile(seq, 512)
    x2d = x.reshape(t, d)
    mem2d = mem.reshape(batch * n_mem, d)
    moe_tiles = None
    for layer in range(depth):
        wl = w_in[layer]
        w_main = jnp.concatenate([wl[:, o_aq:o_ak], wl[:, o_pool:o_mi], wl[:, o_ak:]], axis=1).astype(MXU_DTYPE)
        w_gate = jnp.concatenate(
            [jnp.pad(wl[:, o_mi:o_aq][:, jnp.array(cols)], ((0, 0), (0, LANES - len(cols)))) for cols in gate_cols],
            axis=1).astype(MXU_DTYPE)
        gb_row = jnp.concatenate(
            [jnp.pad(mlstm_gate_b[layer][jnp.array(cols)], (0, LANES - len(cols))) for cols in gate_cols])[None]
        x2d, proj, gates = _inproj(x2d, moe_tiles, norm_mix_g[layer][None], w_main, w_gate, tm)
        proj3 = proj.reshape(batch, seq, proj.shape[1])
        gates3 = gates.reshape(batch, seq, gates.shape[1])

        y_pool = _pool_mixer(proj3, _block_diag(pool_w[layer]).astype(MXU_DTYPE), pool_scale[layer][None], c_pool,
                             _pick_tile(seq, 256))
        y_mlstm = _mlstm(proj3, gates3, mlstm_conv_w[layer], gb_row, mlstm_norm_g[layer][None], m_heads, m_cols)
        q4, kt4, v4 = _qkprep(proj3, jnp.tile(q_norm_g[layer], n_q_heads)[None],
                              jnp.tile(k_norm_g[layer], n_kv_heads)[None], gq, gk, cos2, sin2,
                              n_q_heads, n_kv_heads, (0, c_ak, c_ak + 1), ts)
        y_attn = _flash_attention(q4, kt4, v4, _pick_tile(seq, 128), _pick_tile(seq, 512))

        x2d = _outproj(x2d, y_pool.reshape(t, pool_width), y_mlstm.reshape(t, m_width), y_attn.reshape(t, q_width),
                       w_out[layer].astype(MXU_DTYPE), tm)

        kv = _kvproj(mem2d, ca_wkv[layer].astype(MXU_DTYPE), _pick_tile(batch * n_mem, 512))
        x3, h_tiles, logits_t = _crossattn(
            x2d.reshape(batch, seq, d), kv.reshape(batch, n_mem, kv.shape[1]), norm_mem_g[layer][None],
            ca_wq[layer].astype(MXU_DTYPE), ca_wo[layer].astype(MXU_DTYPE), norm_ffn_g[layer][None],
            router_w[layer].T.astype(MXU_DTYPE), ts)
        x2d = x3.reshape(t, d)

        idx, gate = _router(logits_t, cap)
        moe_tiles = _moe(idx.reshape(batch * ne, 1, cap), gate.reshape(batch * ne, 1, cap),
                         h_tiles.reshape(t * SUBLANES, LANES), expert_w_gu[layer].astype(MXU_DTYPE),
                         expert_w_down[layer].astype(MXU_DTYPE), batch, seq, _pick_tile(f_exp, 1024))
        moe_tiles = moe_tiles.reshape(t * SUBLANES, LANES)
    out = _final_norm(x2d, moe_tiles, final_norm_g[None], tm)
    return out.reshape(batch, seq, d)
```

```python
import functools
import math

import jax
import jax.numpy as jnp
from jax import lax
from jax.experimental import pallas as pl
from jax.experimental.pallas import tpu as pltpu

F32 = jnp.float32
I32 = jnp.int32
MXU_DTYPE = jnp.bfloat16

LANES = 128
SUBLANES = 8
VMEM_LIMIT_BYTES = 56 * 1024 * 1024

EPS = 1e-6
GRID_W = 64
HEAD_DIM = 64
POOL_WINDOWS = (2, 4, 8, 16)
POOL_HALO = 8
MLSTM_CHUNK = 128
MLSTM_CONV = 5
CONV_HALO = 8
ROPE_THETA = 10000.0
CA_HEADS = 4
N_EXPERTS = 16
EC_CAPACITY_FACTOR = 2
SMEM_1D_TILE = 1024
N_IDX_SLOTS = 3
SCATTER_GROUP = 16
NEG_INF = float("-inf")
LOG2_E = math.log2(math.e)


def _cparams(*sem):
    return pltpu.CompilerParams(dimension_semantics=sem, vmem_limit_bytes=VMEM_LIMIT_BYTES)


def _mm(a, b):
    return jnp.dot(a.astype(MXU_DTYPE), b.astype(MXU_DTYPE), preferred_element_type=F32)


def _mm_nt(a, b):
    return lax.dot_general(a.astype(MXU_DTYPE), b.astype(MXU_DTYPE), (((1,), (1,)), ((), ())),
                           preferred_element_type=F32)


def _split3(x):
    hi = x.astype(MXU_DTYPE)
    r1 = x - hi.astype(F32)
    mid = r1.astype(MXU_DTYPE)
    lo = (r1 - mid.astype(F32)).astype(MXU_DTYPE)
    return hi, mid, lo


def _rms(x, g):
    return x * lax.rsqrt(jnp.mean(x * x, axis=-1, keepdims=True) + EPS) * g


def _from_token_tiles(ref, rows):
    return jnp.concatenate([ref[pl.ds(s, rows, stride=SUBLANES), :] for s in range(SUBLANES)], axis=1)


def _to_token_tiles(ref, val, rows):
    for s in range(SUBLANES):
        ref[pl.ds(s, rows, stride=SUBLANES), :] = val[:, s * LANES:(s + 1) * LANES]


def _inproj_body(x_ref, g_ref, w_ref, wg_ref, o_ref, og_ref):
    hb = _rms(x_ref[...], g_ref[...]).astype(MXU_DTYPE)
    o_ref[...] = jnp.dot(hb, w_ref[...], preferred_element_type=F32)
    og_ref[...] = jnp.dot(hb, wg_ref[...], preferred_element_type=F32)


def _inproj_merge_body(x_ref, moe_ref, g_ref, w_ref, wg_ref, xo_ref, o_ref, og_ref, *, tm):
    x = x_ref[...] + _from_token_tiles(moe_ref, tm)
    xo_ref[...] = x
    hb = _rms(x, g_ref[...]).astype(MXU_DTYPE)
    o_ref[...] = jnp.dot(hb, w_ref[...], preferred_element_type=F32)
    og_ref[...] = jnp.dot(hb, wg_ref[...], preferred_element_type=F32)


def _inproj(x2d, moe_tiles, g, w_main, w_gate, tm):
    t, d = x2d.shape
    n = w_main.shape[1]
    grid = (t // tm,)
    row = lambda i: (i, 0)
    fixed = lambda i: (0, 0)
    gw = w_gate.shape[1]
    w_specs = [pl.BlockSpec((1, d), fixed), pl.BlockSpec((d, n), fixed), pl.BlockSpec((d, gw), fixed)]
    o_specs = [pl.BlockSpec((tm, n), row), pl.BlockSpec((tm, gw), row)]
    o_shapes = [jax.ShapeDtypeStruct((t, n), F32), jax.ShapeDtypeStruct((t, gw), F32)]
    if moe_tiles is None:
        proj, gates = pl.pallas_call(
            _inproj_body, grid=grid, in_specs=[pl.BlockSpec((tm, d), row)] + w_specs, out_specs=o_specs,
            out_shape=o_shapes, compiler_params=_cparams("parallel"), name="inproj",
        )(x2d, g, w_main, w_gate)
        return x2d, proj, gates
    xo, proj, gates = pl.pallas_call(
        functools.partial(_inproj_merge_body, tm=tm), grid=grid,
        in_specs=[pl.BlockSpec((tm, d), row), pl.BlockSpec((tm * SUBLANES, LANES), row)] + w_specs,
        out_specs=[pl.BlockSpec((tm, d), row)] + o_specs,
        out_shape=[jax.ShapeDtypeStruct((t, d), F32)] + o_shapes,
        compiler_params=_cparams("parallel"), name="inproj_merge",
    )(x2d, moe_tiles, g, w_main, w_gate)
    return xo, proj, gates


def _pool_body(u_ref, w_ref, sc_ref, o_ref, upad_ref, *, seq, tc):
    i = pl.program_id(1)

    @pl.when(i == 0)
    def _():
        zeros = jnp.zeros((POOL_HALO, u_ref.shape[1]), F32)
        upad_ref[0:POOL_HALO, :] = zeros
        upad_ref[POOL_HALO + seq:POOL_HALO + seq + POOL_HALO, :] = zeros
        upad_ref[POOL_HALO:POOL_HALO + seq, :] = u_ref[...]

    t0 = pl.multiple_of(i * tc, tc)
    win_rows = tc + 2 * POOL_HALO
    t = t0 + lax.broadcasted_iota(I32, (tc, LANES), 0)
    lane = lax.broadcasted_iota(I32, (tc, LANES), 1)
    upper = lane >= (LANES // 2)
    halves = []
    for hf in range(2):
        w_lo, w_hi = POOL_WINDOWS[2 * hf], POOL_WINDOWS[2 * hf + 1]
        win = upad_ref[pl.ds(t0, win_rows), hf * LANES:(hf + 1) * LANES]
        acc = jnp.zeros((tc, LANES), F32)
        for j in range(-(w_hi // 2), w_hi // 2):
            piece = pltpu.roll(win, (-j) % win_rows, 0)[POOL_HALO:POOL_HALO + tc]
            if -(w_lo // 2) <= j < w_lo // 2:
                acc = acc + piece
            else:
                acc = acc + jnp.where(upper, piece, 0.0)
        half_w = jnp.where(upper, w_hi // 2, w_lo // 2)
        cnt = jnp.minimum(t + half_w, seq) - jnp.maximum(t - half_w, 0)
        halves.append(acc / cnt.astype(F32) - win[POOL_HALO:POOL_HALO + tc])
    d = jnp.concatenate(halves, axis=1)
    o_ref[...] = _mm(d, w_ref[...]) * sc_ref[...]


def _pool_mixer(proj3, w_bd, scale, col_block, tc):
    b, seq, _ = proj3.shape
    pw = w_bd.shape[0]
    return pl.pallas_call(
        functools.partial(_pool_body, seq=seq, tc=tc), grid=(b, seq // tc),
        in_specs=[pl.BlockSpec((None, seq, pw), lambda bi, i: (bi, 0, col_block)),
                  pl.BlockSpec((pw, pw), lambda bi, i: (0, 0)),
                  pl.BlockSpec((1, pw), lambda bi, i: (0, 0))],
        out_specs=pl.BlockSpec((None, tc, pw), lambda bi, i: (bi, i, 0)),
        out_shape=jax.ShapeDtypeStruct((b, seq, pw), F32),
        scratch_shapes=[pltpu.VMEM((seq + 2 * POOL_HALO, pw), F32)],
        compiler_params=_cparams("parallel", "arbitrary"), name="pool_mixer",
    )(proj3, w_bd, scale)


def _mlstm_body(q_ref, k_ref, v_ref, o_ref, gt_ref, cwq_ref, cwk_ref, gb_ref, ng_ref, y_ref,
                qpad_ref, kpad_ref, qc_ref, kc_ref, kt_ref, gtt_ref, hf_ref, hb_ref, c_ref, n_ref, m_ref,
                *, seq, n_heads):
    L = MLSTM_CHUNK
    nc = seq // L
    hd = HEAD_DIM
    gbias = gb_ref[...]

    zeros = jnp.zeros((CONV_HALO, LANES), F32)
    for pad_ref, src_ref in ((qpad_ref, q_ref), (kpad_ref, k_ref)):
        pad_ref[0:CONV_HALO, :] = zeros
        pad_ref[CONV_HALO + seq:CONV_HALO + seq + CONV_HALO, :] = zeros
        pad_ref[CONV_HALO:CONV_HALO + seq, :] = src_ref[...]
    win_rows = L + 2 * CONV_HALO

    def conv_chunk(ci, carry):
        t0 = pl.multiple_of(ci * L, L)
        for pad_ref, cw_ref, dst_ref, post in ((qpad_ref, cwq_ref, qc_ref, 1.0),
                                               (kpad_ref, cwk_ref, kc_ref, 1.0 / math.sqrt(hd))):
            win = pad_ref[pl.ds(t0, win_rows), :]
            acc = jnp.zeros((L, LANES), F32)
            for j in range(MLSTM_CONV):
                off = j - MLSTM_CONV // 2
                piece = pltpu.roll(win, (-off) % win_rows, 0)[CONV_HALO:CONV_HALO + L]
                acc = acc + piece * cw_ref[j:j + 1, :]
            act = acc * jax.nn.sigmoid(acc)
            act = act * post if post != 1.0 else act
            dst_ref[pl.ds(t0, L), :] = act
            if dst_ref is kc_ref:
                kt_ref[:, pl.ds(t0, L)] = act.T
        gtt_ref[:, pl.ds(t0, L)] = (gt_ref[pl.ds(t0, L), :] + gbias).T
        return carry

    lax.fori_loop(0, nc, conv_chunk, 0)

    c_ref[...] = jnp.zeros(c_ref.shape, F32)
    n_ref[...] = jnp.zeros(n_ref.shape, F32)
    m_ref[...] = jnp.zeros(m_ref.shape, F32)

    row = lax.broadcasted_iota(I32, (L, L), 0)
    col = lax.broadcasted_iota(I32, (L, L), 1)
    tri = ((col <= row).astype(MXU_DTYPE), (col >= row).astype(MXU_DTYPE))
    tri_t = (tri[1], tri[0])
    keep = (col <= row, col >= row)
    gate_lane = lax.broadcasted_iota(I32, (L, LANES), 1)
    gate_sub = lax.broadcasted_iota(I32, (LANES, L), 0)
    is_i, is_f = gate_lane < 4, (gate_lane >= 4) & (gate_lane < 8)
    is_i_t, is_f_t = gate_sub < 4, (gate_sub >= 4) & (gate_sub < 8)

    def cumulative(lf, left, d):
        hi, mid, lo = _split3(lf)
        mat = tri[d] if left else tri_t[d]
        mm = (lambda x: jnp.dot(mat, x, preferred_element_type=F32)) if left else (
            lambda x: jnp.dot(x, mat, preferred_element_type=F32))
        return mm(hi) + mm(mid) + mm(lo)

    def chunk_step(it, carry):
        for d in range(2):
            ci = it if d == 0 else nc - 1 - it
            t0 = pl.multiple_of(ci * L, L)
            gpre = gt_ref[pl.ds(t0, L), :] + gbias
            gm = jnp.where(is_i, gpre, cumulative(jnp.where(is_f, jax.nn.log_sigmoid(gpre), 0.0), True, d))
            gpre_t = gtt_ref[:, pl.ds(t0, L)]
            gmt = jnp.where(is_i_t, gpre_t, cumulative(jnp.where(is_f_t, jax.nn.log_sigmoid(gpre_t), 0.0), False, d))
            last = L - 1 if d == 0 else 0
            qc = qc_ref[pl.ds(t0, L), :]
            kc = kc_ref[pl.ds(t0, L), :]
            kt = kt_ref[:, pl.ds(t0, L)]
            vc = v_ref[pl.ds(t0, L), :]
            outs = []
            for hh in range(2):
                sidx = hh * 2 + d
                ic = d * 2 + hh
                fc = 4 + d * 2 + hh
                li_col = gm[:, ic:ic + 1]
                b_col = gm[:, fc:fc + 1]
                li_row = gmt[ic:ic + 1, :]
                b_row = gmt[fc:fc + 1, :]
                m_st = m_ref[sidx][:, 0:1]
                c_st = c_ref[sidx]
                n_st = n_ref[sidx]
                qh = qc[:, hh * hd:(hh + 1) * hd]
                kh = kc[:, hh * hd:(hh + 1) * hd]
                kht = kt[hh * hd:(hh + 1) * hd, :]
                vh = vc[:, hh * hd:(hh + 1) * hd]

                dmat = jnp.where(keep[d], b_col - b_row + li_row, NEG_INF)
                inter = b_col + m_st
                m_j = jnp.maximum(inter, jnp.max(dmat, axis=-1, keepdims=True))
                a_mat = jnp.exp(dmat - m_j) * _mm_nt(qh, kh)
                a_int = jnp.exp(inter - m_j)
                num = _mm(a_mat, vh) + a_int * _mm(qh, c_st)
                den = (jnp.sum(a_mat, axis=-1, keepdims=True)
                       + a_int * jnp.sum(qh * n_st, axis=-1, keepdims=True))
                outs.append(num / jnp.maximum(jnp.abs(den), jnp.exp(-m_j)))

                b_last = b_col[last:last + 1, :]
                g_row = b_last - b_row + li_row
                g_col = b_last - b_col + li_col
                m_new = jnp.maximum(b_last + m_st, jnp.max(g_row, axis=-1, keepdims=True))
                w_col = jnp.exp(g_col - m_new)
                decay = jnp.exp(b_last + m_st - m_new)
                c_ref[sidx] = decay * c_st + _mm(kht, w_col * vh)
                n_ref[sidx] = decay * n_st + jnp.sum(w_col * kh, axis=0, keepdims=True)
                m_ref[sidx] = jnp.broadcast_to(m_new, (1, LANES))
            dst = hf_ref if d == 0 else hb_ref
            dst[pl.ds(t0, L), :] = jnp.concatenate(outs, axis=1)
        return carry

    lax.fori_loop(0, nc, chunk_step, 0)

    ng = ng_ref[...]
    lane = lax.broadcasted_iota(I32, (L, LANES), 1)
    upper = lane >= hd

    def out_chunk(ci, carry):
        t0 = pl.multiple_of(ci * L, L)
        hs = hf_ref[pl.ds(t0, L), :] + hb_ref[pl.ds(t0, L), :]
        sq = hs * hs
        ms_lo = jnp.sum(jnp.where(upper, 0.0, sq), axis=-1, keepdims=True) / hd
        ms_hi = jnp.sum(jnp.where(upper, sq, 0.0), axis=-1, keepdims=True) / hd
        ms = jnp.where(upper, ms_hi, ms_lo)
        y = hs * lax.rsqrt(ms + EPS) * ng
        y_ref[pl.ds(t0, L), :] = y * jax.nn.sigmoid(o_ref[pl.ds(t0, L), :])
        return carry

    lax.fori_loop(0, nc, out_chunk, 0)


def _mlstm(proj3, gates3, conv_w, gate_bias_row, norm_g_row, n_heads, cols):
    b, seq, _ = proj3.shape
    qb, kb, vb, ob = cols
    n_pairs = n_heads // 2
    width = n_heads * HEAD_DIM
    blk = lambda base: pl.BlockSpec((None, seq, LANES), lambda bi, p, base=base: (bi, 0, base + p))
    k_conv_base = width // LANES
    return pl.pallas_call(
        functools.partial(_mlstm_body, seq=seq, n_heads=n_heads), grid=(b, n_pairs),
        in_specs=[blk(qb), blk(kb), blk(vb), blk(ob),
                  pl.BlockSpec((None, seq, LANES), lambda bi, p: (bi, 0, p)),
                  pl.BlockSpec((MLSTM_CONV, LANES), lambda bi, p: (0, p)),
                  pl.BlockSpec((MLSTM_CONV, LANES), lambda bi, p: (0, k_conv_base + p)),
                  pl.BlockSpec((1, LANES), lambda bi, p: (0, p)),
                  pl.BlockSpec((1, LANES), lambda bi, p: (0, p))],
        out_specs=pl.BlockSpec((None, seq, LANES), lambda bi, p: (bi, 0, p)),
        out_shape=jax.ShapeDtypeStruct((b, seq, width), F32),
        scratch_shapes=[pltpu.VMEM((seq + 2 * CONV_HALO, LANES), F32), pltpu.VMEM((seq + 2 * CONV_HALO, LANES), F32),
                        pltpu.VMEM((seq, LANES), F32), pltpu.VMEM((seq, LANES), F32),
                        pltpu.VMEM((LANES, seq), F32), pltpu.VMEM((LANES, seq), F32),
                        pltpu.VMEM((seq, LANES), F32), pltpu.VMEM((seq, LANES), F32),
                        pltpu.VMEM((4, HEAD_DIM, HEAD_DIM), F32), pltpu.VMEM((4, 1, HEAD_DIM), F32),
                        pltpu.VMEM((4, 1, LANES), F32)],
        compiler_params=_cparams("parallel", "parallel"), name="mlstm",
    )(proj3, proj3, proj3, proj3, gates3, conv_w, conv_w, gate_bias_row, norm_g_row)


def _group_mean_sq(x, gmat):
    hi = (x * x).astype(MXU_DTYPE)
    lo = (x * x - hi.astype(F32)).astype(MXU_DTYPE)
    ssq = jnp.dot(hi, gmat, preferred_element_type=F32) + jnp.dot(lo, gmat, preferred_element_type=F32)
    return ssq * (1.0 / HEAD_DIM)


def _norm_rope(x, g_row, gmat, cos_t, sin_t):
    width = x.shape[1]
    xn = x * lax.rsqrt(_group_mean_sq(x, gmat) + EPS) * g_row
    lane = lax.broadcasted_iota(I32, x.shape, 1)
    partner = jnp.where(lane % 2 == 0, pltpu.roll(xn, width - 1, 1), pltpu.roll(xn, 1, 1))
    return xn * cos_t + partner * sin_t


def _qkprep_body(q_ref, k_ref, v_ref, qg_ref, kg_ref, gq_ref, gk_ref, cos_ref, sin_ref, qo_ref, kt_ref, vo_ref,
                 *, n_q_heads, n_kv_heads):
    hd = HEAD_DIM
    cos2, sin2 = cos_ref[...], sin_ref[...]
    reps = n_q_heads * hd // LANES
    q = _norm_rope(q_ref[...], qg_ref[...], gq_ref[...],
                   jnp.concatenate([cos2] * reps, axis=1), jnp.concatenate([sin2] * reps, axis=1))
    q = q * (hd ** -0.5 * LOG2_E)
    for h in range(n_q_heads):
        qo_ref[h] = q[:, h * hd:(h + 1) * hd].astype(qo_ref.dtype)
    k = _norm_rope(k_ref[...], kg_ref[...], gk_ref[...], cos2, sin2)
    kt = k.T
    v = v_ref[...]
    kv_width = v.shape[1]
    lane = lax.broadcasted_iota(I32, v.shape, 1)
    for h in range(n_kv_heads):
        kt_ref[h] = kt[h * hd:(h + 1) * hd, :].astype(kt_ref.dtype)
        vh = v if h == 0 else pltpu.roll(v, kv_width - h * hd, 1)
        vo_ref[h] = jnp.where(lane < hd, vh, jnp.where(lane == hd, 1.0, 0.0)).astype(vo_ref.dtype)


def _qkprep(proj3, q_gain_row, k_gain_row, gq, gk, cos2, sin2, n_q_heads, n_kv_heads, cols, ts):
    b, seq, _ = proj3.shape
    qcol, kcol, vcol = cols
    qw, kw = n_q_heads * HEAD_DIM, n_kv_heads * HEAD_DIM
    fixed = lambda bi, i: (0, 0)
    return pl.pallas_call(
        functools.partial(_qkprep_body, n_q_heads=n_q_heads, n_kv_heads=n_kv_heads), grid=(b, seq // ts),
        in_specs=[pl.BlockSpec((None, ts, qw), lambda bi, i: (bi, i, qcol)),
                  pl.BlockSpec((None, ts, kw), lambda bi, i: (bi, i, kcol)),
                  pl.BlockSpec((None, ts, kw), lambda bi, i: (bi, i, vcol)),
                  pl.BlockSpec((1, qw), fixed), pl.BlockSpec((1, kw), fixed),
                  pl.BlockSpec((qw, qw), fixed), pl.BlockSpec((kw, kw), fixed),
                  pl.BlockSpec((ts, kw), lambda bi, i: (i, 0)), pl.BlockSpec((ts, kw), lambda bi, i: (i, 0))],
        out_specs=[pl.BlockSpec((None, n_q_heads, ts, HEAD_DIM), lambda bi, i: (bi, 0, i, 0)),
                   pl.BlockSpec((None, n_kv_heads, HEAD_DIM, ts), lambda bi, i: (bi, 0, 0, i)),
                   pl.BlockSpec((None, n_kv_heads, ts, kw), lambda bi, i: (bi, 0, i, 0))],
        out_shape=[jax.ShapeDtypeStruct((b, n_q_heads, seq, HEAD_DIM), MXU_DTYPE),
                   jax.ShapeDtypeStruct((b, n_kv_heads, HEAD_DIM, seq), MXU_DTYPE),
                   jax.ShapeDtypeStruct((b, n_kv_heads, seq, kw), MXU_DTYPE)],
        compiler_params=_cparams("parallel", "parallel"), name="qk_prep",
    )(proj3, proj3, proj3, q_gain_row, k_gain_row, gq, gk, cos2, sin2)


def _flash_body(q_ref, kt_ref, v_ref, o_ref, *, seq, tq, tk, group):
    hd = HEAD_DIM
    rows = group * tq
    q = q_ref[...].reshape(rows, hd)
    m = jnp.full((rows, 1), NEG_INF, F32)
    acc = jnp.zeros((rows, v_ref.shape[1]), F32)
    for kc in range(seq // tk):
        s = jnp.dot(q, kt_ref[:, kc * tk:(kc + 1) * tk], preferred_element_type=F32)
        m_new = jnp.maximum(m, jnp.max(s, axis=-1, keepdims=True))
        p = jnp.exp2(s - m_new)
        acc = jnp.exp2(m - m_new) * acc + jnp.dot(p.astype(v_ref.dtype), v_ref[kc * tk:(kc + 1) * tk, :],
                                                  preferred_element_type=F32)
        m = m_new
    o = acc[:, :hd] / acc[:, hd:hd + 1]
    o_ref[...] = jnp.concatenate([o[g * tq:(g + 1) * tq] for g in range(group)], axis=1)


def _flash_attention(q4, kt4, v4, tq, tk):
    b, n_q_heads, seq, hd = q4.shape
    n_kv_heads = kt4.shape[1]
    group = n_q_heads // n_kv_heads
    return pl.pallas_call(
        functools.partial(_flash_body, seq=seq, tq=tq, tk=tk, group=group), grid=(b, n_kv_heads, seq // tq),
        in_specs=[pl.BlockSpec((None, group, tq, hd), lambda bi, h, i: (bi, h, i, 0)),
                  pl.BlockSpec((None, None, hd, seq), lambda bi, h, i: (bi, h, 0, 0)),
                  pl.BlockSpec((None, None, seq, v4.shape[3]), lambda bi, h, i: (bi, h, 0, 0))],
        out_specs=pl.BlockSpec((None, tq, group * hd), lambda bi, h, i: (bi, i, h)),
        out_shape=jax.ShapeDtypeStruct((b, seq, n_q_heads * hd), F32),
        compiler_params=_cparams("parallel", "parallel", "parallel"), name="flash_gqa",
    )(q4, kt4, v4)


def _outproj_body(x_ref, yp_ref, ym_ref, ya_ref, wp_ref, wm_ref, wa_ref, o_ref):
    o_ref[...] = (x_ref[...] + _mm(yp_ref[...], wp_ref[...]) + _mm(ym_ref[...], wm_ref[...])
                  + _mm(ya_ref[...], wa_ref[...]))


def _outproj(x2d, yp, ym, ya, w_out, tm):
    t, d = x2d.shape
    row = lambda i: (i, 0)
    fixed = lambda i: (0, 0)
    wp, wm, wa = yp.shape[1], ym.shape[1], ya.shape[1]
    return pl.pallas_call(
        _outproj_body, grid=(t // tm,),
        in_specs=[pl.BlockSpec((tm, d), row), pl.BlockSpec((tm, wp), row), pl.BlockSpec((tm, wm), row),
                  pl.BlockSpec((tm, wa), row), pl.BlockSpec((wp, d), fixed), pl.BlockSpec((wm, d), fixed),
                  pl.BlockSpec((wa, d), fixed)],
        out_specs=pl.BlockSpec((tm, d), row), out_shape=jax.ShapeDtypeStruct((t, d), F32),
        compiler_params=_cparams("parallel"), name="outproj",
    )(x2d, yp, ym, ya, w_out[:wp], w_out[wp:wp + wm], w_out[wp + wm:])


def _kvproj_body(m_ref, w_ref, o_ref):
    o_ref[...] = _mm(m_ref[...], w_ref[...]).astype(o_ref.dtype)


def _kvproj(mem2d, wkv, tm):
    t, d = mem2d.shape
    n = wkv.shape[1]
    return pl.pallas_call(
        _kvproj_body, grid=(t // tm,),
        in_specs=[pl.BlockSpec((tm, d), lambda i: (i, 0)), pl.BlockSpec((d, n), lambda i: (0, 0))],
        out_specs=pl.BlockSpec((tm, n), lambda i: (i, 0)), out_shape=jax.ShapeDtypeStruct((t, n), MXU_DTYPE),
        compiler_params=_cparams("parallel"), name="mem_kv_proj",
    )(mem2d, wkv)


def _crossattn_body(x_ref, kv_ref, gm_ref, wq_ref, wo_ref, gf_ref, rwt_ref, xo_ref, hf_ref, lg_ref, *, ts):
    x = x_ref[...]
    q = _mm(_rms(x, gm_ref[...]), wq_ref[...])
    ca_width = q.shape[1]
    dh = ca_width // CA_HEADS
    kv = kv_ref[...]
    heads = []
    for h in range(CA_HEADS):
        s = _mm_nt(q[:, h * dh:(h + 1) * dh], kv[:, h * dh:(h + 1) * dh]) * (dh ** -0.5)
        e = jnp.exp(s - jnp.max(s, axis=-1, keepdims=True))
        p = e / jnp.sum(e, axis=-1, keepdims=True)
        heads.append(_mm(p, kv[:, ca_width + h * dh:ca_width + (h + 1) * dh]))
    x2 = x + _mm(jnp.concatenate(heads, axis=1), wo_ref[...])
    xo_ref[...] = x2
    hf = _rms(x2, gf_ref[...])
    _to_token_tiles(hf_ref, hf, ts)
    lg_ref[...] = _mm_nt(rwt_ref[...], hf)


def _crossattn(x3, kv3, g_mem, wq, wo, g_ffn, router_wt, ts):
    b, seq, d = x3.shape
    n_mem, kvw = kv3.shape[1], kv3.shape[2]
    ne = router_wt.shape[0]
    fixed = lambda bi, i: (0, 0)
    return pl.pallas_call(
        functools.partial(_crossattn_body, ts=ts), grid=(b, seq // ts),
        in_specs=[pl.BlockSpec((None, ts, d), lambda bi, i: (bi, i, 0)),
                  pl.BlockSpec((None, n_mem, kvw), lambda bi, i: (bi, 0, 0)),
                  pl.BlockSpec((1, d), fixed), pl.BlockSpec(wq.shape, fixed), pl.BlockSpec(wo.shape, fixed),
                  pl.BlockSpec((1, d), fixed), pl.BlockSpec((ne, d), fixed)],
        out_specs=[pl.BlockSpec((None, ts, d), lambda bi, i: (bi, i, 0)),
                   pl.BlockSpec((None, ts * SUBLANES, LANES), lambda bi, i: (bi, i, 0)),
                   pl.BlockSpec((None, ne, ts), lambda bi, i: (bi, 0, i))],
        out_shape=[jax.ShapeDtypeStruct((b, seq, d), F32),
                   jax.ShapeDtypeStruct((b, seq * SUBLANES, LANES), F32),
                   jax.ShapeDtypeStruct((b, ne, seq), F32)],
        compiler_params=_cparams("parallel", "parallel"), name="cross_attn",
    )(x3, kv3, g_mem, wq, wo, g_ffn, router_wt)


def _excl_prefix(flags, upper_excl, ones):
    rows, seq = flags.shape
    off = jnp.zeros((rows, LANES), F32)
    out = []
    for c in range(seq // LANES):
        xc = flags[:, c * LANES:(c + 1) * LANES].astype(MXU_DTYPE)
        out.append(jnp.dot(xc, upper_excl, preferred_element_type=F32) + off)
        off = off + jnp.dot(xc, ones, preferred_element_type=F32)
    return jnp.concatenate(out, axis=1)


def _router_body(lg_ref, idx_ref, gate_ref, *, seq, cap):
    lg = lg_ref[...]
    ne = lg.shape[0]
    ex = jnp.exp(lg - jnp.max(lg, axis=0, keepdims=True))
    aff = ex / jnp.sum(ex, axis=0, keepdims=True)
    bits = pltpu.bitcast(aff, I32)

    thr = jnp.zeros((ne, 1), I32)
    for bit in range(30, -1, -1):
        cand = thr | (1 << bit)
        cnt = jnp.sum((bits >= cand).astype(F32), axis=1, keepdims=True)
        thr = jnp.where(cnt >= cap, cand, thr)

    r_i = lax.broadcasted_iota(I32, (LANES, LANES), 0)
    c_i = lax.broadcasted_iota(I32, (LANES, LANES), 1)
    upper_excl = (r_i < c_i).astype(MXU_DTYPE)
    ones = jnp.ones((LANES, LANES), MXU_DTYPE)

    gt = bits > thr
    eq = bits == thr
    n_gt = jnp.sum(gt.astype(F32), axis=1, keepdims=True)
    tie_rank = _excl_prefix(eq.astype(F32), upper_excl, ones)
    sel = gt | (eq & (tie_rank < cap - n_gt))
    pos = _excl_prefix(sel.astype(F32), upper_excl, ones).astype(I32)

    n_iota = lax.broadcasted_iota(I32, (ne, seq), 1)
    seq_bits = (seq - 1).bit_length()
    flag_bit = 2 * seq_bits
    packed = jnp.where(sel, n_iota | ((n_iota - pos) << seq_bits) | (1 << flag_bit), 0)
    gbits = jnp.where(sel, bits, 0)
    for k in range(seq_bits):
        sh = 1 << k
        moved_p = pltpu.roll(packed, seq - sh, 1)
        moved_g = pltpu.roll(gbits, seq - sh, 1)
        take = ((moved_p >> flag_bit) & 1) * ((moved_p >> (seq_bits + k)) & 1) == 1
        stay = ((packed >> flag_bit) & 1) * (1 - ((packed >> (seq_bits + k)) & 1)) == 1
        packed = jnp.where(take, moved_p, jnp.where(stay, packed, 0))
        gbits = jnp.where(take, moved_g, jnp.where(stay, gbits, 0))
    idx_ref[...] = packed[:, :cap] & (seq - 1)
    gate_ref[...] = pltpu.bitcast(gbits[:, :cap], F32)


def _router(logits_t, cap):
    b, ne, seq = logits_t.shape
    return pl.pallas_call(
        functools.partial(_router_body, seq=seq, cap=cap), grid=(b,),
        in_specs=[pl.BlockSpec((None, ne, seq), lambda bi: (bi, 0, 0))],
        out_specs=[pl.BlockSpec((None, ne, cap), lambda bi: (bi, 0, 0)),
                   pl.BlockSpec((None, ne, cap), lambda bi: (bi, 0, 0))],
        out_shape=[jax.ShapeDtypeStruct((b, ne, cap), I32), jax.ShapeDtypeStruct((b, ne, cap), F32)],
        compiler_params=_cparams("parallel"), name="ec_router",
    )(logits_t)


def _moe_body(idx_hbm, gate_hbm, h_hbm, wg_ref, wu_ref, wd_ref, out_hbm,
              idx_s, gate_s, xbuf, xbf, ybuf, yacc, acc, gsem, isem, osem, *, seq, cap, ne, n_fc, stride):
    b = pl.program_id(0)
    e = pl.program_id(1)
    c = pl.program_id(2)
    last = pl.num_programs(0) * ne - 1
    step = b * ne + e
    nstep = jnp.minimum(step + 1, last)
    xslot, nxslot = lax.rem(step, 2), lax.rem(step + 1, 2)
    islot, nislot = lax.rem(step, N_IDX_SLOTS), lax.rem(step + 1, N_IDX_SLOTS)
    tile = SUBLANES
    per = cap // n_fc

    def idx_copies(s, sl):
        src = pl.ds(pl.multiple_of(s * stride, stride), stride)
        dst = pl.ds(pl.multiple_of(sl * stride, stride), stride)
        return (pltpu.make_async_copy(idx_hbm.at[src], idx_s.at[dst], isem.at[0, sl]),
                pltpu.make_async_copy(gate_hbm.at[src], gate_s.at[dst], isem.at[1, sl]))

    def row_copy(src_row, k, sl):
        return pltpu.make_async_copy(h_hbm.at[pl.ds(src_row, tile), :],
                                     xbuf.at[sl, pl.ds(k * tile, tile), :], gsem.at[sl])

    def start_rows(s, isl, sl, k0, n):
        base = (s // ne) * seq
        ibase = isl * stride + k0
        for k in range(n):
            row_copy(pl.multiple_of((base + idx_s[ibase + k]) * tile, tile), k0 + k, sl).start()

    def wait_gather(sl):
        pltpu.make_async_copy(h_hbm.at[pl.ds(0, cap * tile), :], xbuf.at[sl], gsem.at[sl]).wait()

    @pl.when(c == 0)
    def _():
        @pl.when(step == 0)
        def _():
            for cp in idx_copies(0, 0):
                cp.start()
            for cp in idx_copies(0, 0):
                cp.wait()
            for cp in idx_copies(jnp.minimum(1, last), 1):
                cp.start()

            def body(j, carry):
                start_rows(0, 0, 0, j * SUBLANES, SUBLANES)
                return carry
            lax.fori_loop(0, cap // SUBLANES, body, 0)

        for cp in idx_copies(0, nislot):
            cp.wait()
        for cp in idx_copies(jnp.minimum(step + 2, last), lax.rem(step + 2, N_IDX_SLOTS)):
            cp.start()
        wait_gather(xslot)
        xbf[...] = _from_token_tiles(xbuf.at[xslot], cap).astype(xbf.dtype)

        @pl.when(e == 0)
        def _():
            acc[...] = jnp.zeros(acc.shape, F32)

    start_rows(nstep, nislot, nxslot, pl.multiple_of(c * per, per), per)

    x = xbf[...]
    g = jnp.dot(x, wg_ref[...], preferred_element_type=F32)
    u = jnp.dot(x, wu_ref[...], preferred_element_type=F32)
    y = _mm(g * jax.nn.sigmoid(g) * u, wd_ref[...])

    @pl.when(c == 0)
    def _():
        yacc[...] = y

    @pl.when(c > 0)
    def _():
        yacc[...] += y

    @pl.when(c == n_fc - 1)
    def _():
        _to_token_tiles(ybuf, yacc[...], cap)

        ibase = islot * stride
        for k0 in range(0, cap, SCATTER_GROUP):
            rows = [pl.multiple_of(idx_s[ibase + k0 + j] * tile, tile) for j in range(SCATTER_GROUP)]
            new = [acc[pl.ds(rows[j], tile), :] + gate_s[ibase + k0 + j] * ybuf[(k0 + j) * tile:(k0 + j + 1) * tile, :]
                   for j in range(SCATTER_GROUP)]
            for j in range(SCATTER_GROUP):
                acc[pl.ds(rows[j], tile), :] = new[j]

        @pl.when(e == ne - 1)
        def _():
            cp = pltpu.make_async_copy(acc, out_hbm.at[b], osem)
            cp.start()
            cp.wait()

        @pl.when(step == last)
        def _():
            wait_gather(nxslot)
            for cp in idx_copies(0, lax.rem(step + 2, N_IDX_SLOTS)):
                cp.wait()


def _moe(idx, gate, h_tiles, w_gu, w_down, batch, seq, fc):
    cap = idx.shape[2]
    ne, d, f2 = w_gu.shape
    f = f2 // 2
    n_fc = f // fc
    stride = -(-cap // SMEM_1D_TILE) * SMEM_1D_TILE
    flat = lambda a: jnp.pad(a, ((0, 0), (0, 0), (0, stride - cap))).reshape(-1)
    idx3, gate3 = flat(idx), flat(gate)
    return pl.pallas_call(
        functools.partial(_moe_body, seq=seq, cap=cap, ne=ne, n_fc=n_fc, stride=stride), grid=(batch, ne, n_fc),
        in_specs=[pl.BlockSpec(memory_space=pl.ANY), pl.BlockSpec(memory_space=pl.ANY),
                  pl.BlockSpec(memory_space=pl.ANY),
                  pl.BlockSpec((None, d, fc), lambda b, e, c: (e, 0, c)),
                  pl.BlockSpec((None, d, fc), lambda b, e, c: (e, 0, n_fc + c)),
                  pl.BlockSpec((None, fc, d), lambda b, e, c: (e, c, 0))],
        out_specs=pl.BlockSpec(memory_space=pl.ANY),
        out_shape=jax.ShapeDtypeStruct((batch, seq * SUBLANES, LANES), F32),
        scratch_shapes=[pltpu.SMEM((N_IDX_SLOTS * stride,), I32), pltpu.SMEM((N_IDX_SLOTS * stride,), F32),
                        pltpu.VMEM((2, cap * SUBLANES, LANES), F32), pltpu.VMEM((cap, d), MXU_DTYPE),
                        pltpu.VMEM((cap * SUBLANES, LANES), F32), pltpu.VMEM((cap, d), F32),
                        pltpu.VMEM((seq * SUBLANES, LANES), F32),
                        pltpu.SemaphoreType.DMA((2,)), pltpu.SemaphoreType.DMA((2, N_IDX_SLOTS)),
                        pltpu.SemaphoreType.DMA],
        compiler_params=_cparams("arbitrary", "arbitrary", "arbitrary"), name="ec_moe",
    )(idx3, gate3, h_tiles, w_gu, w_gu, w_down)


def _final_body(x_ref, moe_ref, g_ref, o_ref, *, tm):
    o_ref[...] = _rms(x_ref[...] + _from_token_tiles(moe_ref, tm), g_ref[...])


def _final_norm(x2d, moe_tiles, g, tm):
    t, d = x2d.shape
    row = lambda i: (i, 0)
    return pl.pallas_call(
        functools.partial(_final_body, tm=tm), grid=(t // tm,),
        in_specs=[pl.BlockSpec((tm, d), row), pl.BlockSpec((tm * SUBLANES, LANES), row),
                  pl.BlockSpec((1, d), lambda i: (0, 0))],
        out_specs=pl.BlockSpec((tm, d), row), out_shape=jax.ShapeDtypeStruct((t, d), F32),
        compiler_params=_cparams("parallel"), name="final_norm",
    )(x2d, moe_tiles, g)


def _rope_tables(seq):
    rows = seq // GRID_W
    row_ids = jnp.repeat(jnp.arange(rows), GRID_W).astype(F32)
    col_ids = jnp.tile(jnp.arange(GRID_W), rows).astype(F32)
    n_freq = HEAD_DIM // 4
    inv_freq = ROPE_THETA ** (-jnp.arange(n_freq, dtype=F32) / n_freq)
    ang = jnp.concatenate([row_ids[:, None] * inv_freq, col_ids[:, None] * inv_freq], axis=-1)
    cos, sin = jnp.cos(ang), jnp.sin(ang)
    cos_i = jnp.repeat(cos, 2, axis=1)
    sin_i = jnp.stack([-sin, sin], axis=-1).reshape(seq, HEAD_DIM)
    return jnp.tile(cos_i, (1, LANES // HEAD_DIM)), jnp.tile(sin_i, (1, LANES // HEAD_DIM))


def _group_matrix(width):
    gid = jnp.arange(width) // HEAD_DIM
    return (gid[:, None] == gid[None, :]).astype(MXU_DTYPE)


def _block_diag(w):
    g, c, _ = w.shape
    eye = jnp.eye(g, dtype=w.dtype)
    return (eye[:, None, :, None] * w[:, :, None, :]).reshape(g * c, g * c)


def _pick_tile(n, pref):
    t = min(n, pref)
    while n % t:
        t //= 2
    return t


def kernel(x, mem, norm_mix_g, w_in, pool_w, pool_scale, mlstm_conv_w, mlstm_gate_b, mlstm_norm_g, q_norm_g,
           k_norm_g, w_out, norm_mem_g, ca_wq, ca_wkv, ca_wo, norm_ffn_g, router_w, expert_w_gu, expert_w_down,
           final_norm_g):
    batch, seq, d = x.shape
    depth = w_in.shape[0]
    n_mem = mem.shape[1]
    pool_width = pool_scale.shape[1]
    m_width = mlstm_norm_g.shape[1]
    m_heads = m_width // HEAD_DIM
    n_gate = 2 * m_heads
    kv_width = (w_in.shape[2] - pool_width - 4 * m_width - 2 * n_gate) // 6
    q_width = 4 * kv_width
    n_q_heads, n_kv_heads = q_width // HEAD_DIM, kv_width // HEAD_DIM
    ne = router_w.shape[2]
    cap = EC_CAPACITY_FACTOR * seq // ne
    f_exp = expert_w_down.shape[2]
    t = batch * seq

    o_pool = 0
    o_mq = o_pool + pool_width
    o_mi = o_mq + 4 * m_width
    o_aq = o_mi + 2 * n_gate
    o_ak = o_aq + q_width
    c_pool = q_width // pool_width
    c_mq = (q_width + pool_width) // LANES
    m_cols = tuple(c_mq + j * (m_width // LANES) for j in range(4))
    c_ak = (q_width + pool_width + 4 * m_width) // kv_width
    cos2, sin2 = _rope_tables(seq)
    gq, gk = _group_matrix(q_width), _group_matrix(kv_width)
    n_pairs = m_heads // 2
    gate_cols = [[kind * n_gate + dr * m_heads + 2 * p + hh for kind in range(2) for dr in range(2) for hh in range(2)]
                 for p in range(n_pairs)]

    tm = _pick_tile(t, 512)
    ts = _pick_tile(seq, 512)
    x2d = x.reshape(t, d)
    mem2d = mem.reshape(batch * n_mem, d)
    moe_tiles = None
    for layer in range(depth):
        wl = w_in[layer]
        w_main = jnp.concatenate([wl[:, o_aq:o_ak], wl[:, o_pool:o_mi], wl[:, o_ak:]], axis=1).astype(MXU_DTYPE)
        w_gate = jnp.concatenate(
            [jnp.pad(wl[:, o_mi:o_aq][:, jnp.array(cols)], ((0, 0), (0, LANES - len(cols)))) for cols in gate_cols],
            axis=1).astype(MXU_DTYPE)
        gb_row = jnp.concatenate(
            [jnp.pad(mlstm_gate_b[layer][jnp.array(cols)], (0, LANES - len(cols))) for cols in gate_cols])[None]
        x2d, proj, gates = _inproj(x2d, moe_tiles, norm_mix_g[layer][None], w_main, w_gate, tm)
        proj3 = proj.reshape(batch, seq, proj.shape[1])
        gates3 = gates.reshape(batch, seq, gates.shape[1])

        y_pool = _pool_mixer(proj3, _block_diag(pool_w[layer]).astype(MXU_DTYPE), pool_scale[layer][None], c_pool,
                             _pick_tile(seq, 256))
        y_mlstm = _mlstm(proj3, gates3, mlstm_conv_w[layer], gb_row, mlstm_norm_g[layer][None], m_heads, m_cols)
        q4, kt4, v4 = _qkprep(proj3, jnp.tile(q_norm_g[layer], n_q_heads)[None],
                              jnp.tile(k_norm_g[layer], n_kv_heads)[None], gq, gk, cos2, sin2,
                              n_q_heads, n_kv_heads, (0, c_ak, c_ak + 1), ts)
        y_attn = _flash_attention(q4, kt4, v4, _pick_tile(seq, 256), _pick_tile(seq, 1024))

        x2d = _outproj(x2d, y_pool.reshape(t, pool_width), y_mlstm.reshape(t, m_width), y_attn.reshape(t, q_width),
                       w_out[layer].astype(MXU_DTYPE), tm)

        kv = _kvproj(mem2d, ca_wkv[layer].astype(MXU_DTYPE), _pick_tile(batch * n_mem, 512))
        x3, h_tiles, logits_t = _crossattn(
            x2d.reshape(batch, seq, d), kv.reshape(batch, n_mem, kv.shape[1]), norm_mem_g[layer][None],
            ca_wq[layer].astype(MXU_DTYPE), ca_wo[layer].astype(MXU_DTYPE), norm_ffn_g[layer][None],
            router_w[layer].T.astype(MXU_DTYPE), ts)
        x2d = x3.reshape(t, d)

        idx, gate = _router(logits_t, cap)
        moe_tiles = _moe(idx, gate, h_tiles.reshape(t * SUBLANES, LANES), expert_w_gu[layer].astype(MXU_DTYPE),
                         expert_w_down[layer].astype(MXU_DTYPE), batch, seq, _pick_tile(f_exp, 1024))
        moe_tiles = moe_tiles.reshape(t * SUBLANES, LANES)
    out = _final_norm(x2d, moe_tiles, final_norm_g[None], tm)
    return out.reshape(batch, seq, d)
```

```python
import functools
import math

import jax
import jax.numpy as jnp
from jax import lax
from jax.experimental import pallas as pl
from jax.experimental.pallas import tpu as pltpu

F32 = jnp.float32
I32 = jnp.int32
MXU_DTYPE = jnp.bfloat16

LANES = 128
SUBLANES = 8
VMEM_LIMIT_BYTES = 56 * 1024 * 1024

EPS = 1e-6
GRID_W = 64
HEAD_DIM = 64
POOL_WINDOWS = (2, 4, 8, 16)
POOL_HALO = 8
MLSTM_CHUNK = 128
MLSTM_CONV = 5
CONV_HALO = 8
ROPE_THETA = 10000.0
CA_HEADS = 4
N_EXPERTS = 16
EC_CAPACITY_FACTOR = 2
SMEM_1D_TILE = 1024
N_IDX_SLOTS = 3
SCATTER_GROUP = 16
NEG_INF = float("-inf")
LOG2_E = math.log2(math.e)


def _cparams(*sem):
    return pltpu.CompilerParams(dimension_semantics=sem, vmem_limit_bytes=VMEM_LIMIT_BYTES)


def _mm(a, b):
    return jnp.dot(a.astype(MXU_DTYPE), b.astype(MXU_DTYPE), preferred_element_type=F32)


def _mm_nt(a, b):
    return lax.dot_general(a.astype(MXU_DTYPE), b.astype(MXU_DTYPE), (((1,), (1,)), ((), ())),
                           preferred_element_type=F32)


def _split2(x):
    hi = x.astype(MXU_DTYPE)
    return hi, (x - hi.astype(F32)).astype(MXU_DTYPE)


def _split3(x):
    hi = x.astype(MXU_DTYPE)
    r1 = x - hi.astype(F32)
    mid = r1.astype(MXU_DTYPE)
    lo = (r1 - mid.astype(F32)).astype(MXU_DTYPE)
    return hi, mid, lo


def _rms(x, g):
    return x * lax.rsqrt(jnp.mean(x * x, axis=-1, keepdims=True) + EPS) * g


def _from_token_tiles(ref, rows):
    return jnp.concatenate([ref[pl.ds(s, rows, stride=SUBLANES), :] for s in range(SUBLANES)], axis=1)


def _to_token_tiles(ref, val, rows):
    for s in range(SUBLANES):
        ref[pl.ds(s, rows, stride=SUBLANES), :] = val[:, s * LANES:(s + 1) * LANES]


def _inproj_body(x_ref, g_ref, w_ref, wg_ref, o_ref, og_ref):
    hb = _rms(x_ref[...], g_ref[...]).astype(MXU_DTYPE)
    o_ref[...] = jnp.dot(hb, w_ref[...], preferred_element_type=F32)
    og_ref[...] = jnp.dot(hb, wg_ref[...], preferred_element_type=F32)


def _inproj_merge_body(x_ref, moe_ref, g_ref, w_ref, wg_ref, xo_ref, o_ref, og_ref, *, tm):
    x = x_ref[...] + _from_token_tiles(moe_ref, tm)
    xo_ref[...] = x
    hb = _rms(x, g_ref[...]).astype(MXU_DTYPE)
    o_ref[...] = jnp.dot(hb, w_ref[...], preferred_element_type=F32)
    og_ref[...] = jnp.dot(hb, wg_ref[...], preferred_element_type=F32)


def _inproj(x2d, moe_tiles, g, w_main, w_gate, tm):
    t, d = x2d.shape
    n = w_main.shape[1]
    grid = (t // tm,)
    row = lambda i: (i, 0)
    fixed = lambda i: (0, 0)
    gw = w_gate.shape[1]
    w_specs = [pl.BlockSpec((1, d), fixed), pl.BlockSpec((d, n), fixed), pl.BlockSpec((d, gw), fixed)]
    o_specs = [pl.BlockSpec((tm, n), row), pl.BlockSpec((tm, gw), row)]
    o_shapes = [jax.ShapeDtypeStruct((t, n), F32), jax.ShapeDtypeStruct((t, gw), F32)]
    if moe_tiles is None:
        proj, gates = pl.pallas_call(
            _inproj_body, grid=grid, in_specs=[pl.BlockSpec((tm, d), row)] + w_specs, out_specs=o_specs,
            out_shape=o_shapes, compiler_params=_cparams("parallel"), name="inproj",
        )(x2d, g, w_main, w_gate)
        return x2d, proj, gates
    xo, proj, gates = pl.pallas_call(
        functools.partial(_inproj_merge_body, tm=tm), grid=grid,
        in_specs=[pl.BlockSpec((tm, d), row), pl.BlockSpec((tm * SUBLANES, LANES), row)] + w_specs,
        out_specs=[pl.BlockSpec((tm, d), row)] + o_specs,
        out_shape=[jax.ShapeDtypeStruct((t, d), F32)] + o_shapes,
        compiler_params=_cparams("parallel"), name="inproj_merge",
    )(x2d, moe_tiles, g, w_main, w_gate)
    return xo, proj, gates


def _pool_body(u_ref, w_ref, sc_ref, o_ref, upad_ref, *, seq, tc):
    i = pl.program_id(1)

    @pl.when(i == 0)
    def _():
        zeros = jnp.zeros((POOL_HALO, u_ref.shape[1]), F32)
        upad_ref[0:POOL_HALO, :] = zeros
        upad_ref[POOL_HALO + seq:POOL_HALO + seq + POOL_HALO, :] = zeros
        upad_ref[POOL_HALO:POOL_HALO + seq, :] = u_ref[...]

    t0 = pl.multiple_of(i * tc, tc)
    win_rows = tc + 2 * POOL_HALO
    t = t0 + lax.broadcasted_iota(I32, (tc, LANES), 0)
    lane = lax.broadcasted_iota(I32, (tc, LANES), 1)
    upper = lane >= (LANES // 2)
    halves = []
    for hf in range(2):
        w_lo, w_hi = POOL_WINDOWS[2 * hf], POOL_WINDOWS[2 * hf + 1]
        win = upad_ref[pl.ds(t0, win_rows), hf * LANES:(hf + 1) * LANES]
        acc = jnp.zeros((tc, LANES), F32)
        for j in range(-(w_hi // 2), w_hi // 2):
            piece = pltpu.roll(win, (-j) % win_rows, 0)[POOL_HALO:POOL_HALO + tc]
            if -(w_lo // 2) <= j < w_lo // 2:
                acc = acc + piece
            else:
                acc = acc + jnp.where(upper, piece, 0.0)
        half_w = jnp.where(upper, w_hi // 2, w_lo // 2)
        cnt = jnp.minimum(t + half_w, seq) - jnp.maximum(t - half_w, 0)
        halves.append(acc / cnt.astype(F32) - win[POOL_HALO:POOL_HALO + tc])
    d = jnp.concatenate(halves, axis=1)
    o_ref[...] = _mm(d, w_ref[...]) * sc_ref[...]


def _pool_mixer(proj3, w_bd, scale, col_block, tc):
    b, seq, _ = proj3.shape
    pw = w_bd.shape[0]
    return pl.pallas_call(
        functools.partial(_pool_body, seq=seq, tc=tc), grid=(b, seq // tc),
        in_specs=[pl.BlockSpec((None, seq, pw), lambda bi, i: (bi, 0, col_block)),
                  pl.BlockSpec((pw, pw), lambda bi, i: (0, 0)),
                  pl.BlockSpec((1, pw), lambda bi, i: (0, 0))],
        out_specs=pl.BlockSpec((None, tc, pw), lambda bi, i: (bi, i, 0)),
        out_shape=jax.ShapeDtypeStruct((b, seq, pw), F32),
        scratch_shapes=[pltpu.VMEM((seq + 2 * POOL_HALO, pw), F32)],
        compiler_params=_cparams("parallel", "arbitrary"), name="pool_mixer",
    )(proj3, w_bd, scale)


def _mlstm_body(q_ref, k_ref, v_ref, o_ref, gt_ref, cwq_ref, cwk_ref, gb_ref, ng_ref, y_ref,
                qpad_ref, kpad_ref, qc_ref, kc_ref, kt_ref, gtt_ref, hf_ref, hb_ref, cn_ref, m_ref,
                *, seq, n_heads):
    L = MLSTM_CHUNK
    nc = seq // L
    hd = HEAD_DIM
    gbias = gb_ref[...]

    zeros = jnp.zeros((CONV_HALO, LANES), F32)
    for pad_ref, src_ref in ((qpad_ref, q_ref), (kpad_ref, k_ref)):
        pad_ref[0:CONV_HALO, :] = zeros
        pad_ref[CONV_HALO + seq:CONV_HALO + seq + CONV_HALO, :] = zeros
        pad_ref[CONV_HALO:CONV_HALO + seq, :] = src_ref[...]
    win_rows = L + 2 * CONV_HALO

    def conv_chunk(ci, carry):
        t0 = pl.multiple_of(ci * L, L)
        for pad_ref, cw_ref, dst_ref, post in ((qpad_ref, cwq_ref, qc_ref, 1.0),
                                               (kpad_ref, cwk_ref, kc_ref, 1.0 / math.sqrt(hd))):
            win = pad_ref[pl.ds(t0, win_rows), :]
            acc = jnp.zeros((L, LANES), F32)
            for j in range(MLSTM_CONV):
                off = j - MLSTM_CONV // 2
                piece = pltpu.roll(win, (-off) % win_rows, 0)[CONV_HALO:CONV_HALO + L]
                acc = acc + piece * cw_ref[j:j + 1, :]
            act = acc * jax.nn.sigmoid(acc)
            act = act * post if post != 1.0 else act
            dst_ref[pl.ds(t0, L), :] = act
            if dst_ref is kc_ref:
                kt_ref[:, pl.ds(t0, L)] = act.T
        gtt_ref[:, pl.ds(t0, L)] = (gt_ref[pl.ds(t0, L), :] + gbias).T
        return carry

    lax.fori_loop(0, nc, conv_chunk, 0)

    cn_ref[...] = jnp.zeros(cn_ref.shape, F32)
    m_ref[...] = jnp.zeros(m_ref.shape, F32)

    row = lax.broadcasted_iota(I32, (L, L), 0)
    col = lax.broadcasted_iota(I32, (L, L), 1)
    tri = ((col <= row).astype(MXU_DTYPE), (col >= row).astype(MXU_DTYPE))
    tri_t = (tri[1], tri[0])
    keep = (col <= row, col >= row)
    gate_lane = lax.broadcasted_iota(I32, (L, LANES), 1)
    gate_sub = lax.broadcasted_iota(I32, (LANES, L), 0)
    is_i, is_f = gate_lane < 4, (gate_lane >= 4) & (gate_lane < 8)
    is_i_t, is_f_t = gate_sub < 4, (gate_sub >= 4) & (gate_sub < 8)
    head_lanes = (gate_lane < hd, gate_lane >= hd)
    sel_r = lax.broadcasted_iota(I32, (LANES, 2 * LANES), 0)
    sel_c = lax.broadcasted_iota(I32, (LANES, 2 * LANES), 1)
    sel = [[(sel_r == jnp.where(sel_c < LANES, 4 + d * 2 + hh, d * 2 + hh)).astype(MXU_DTYPE) for hh in range(2)]
           for d in range(2)]
    ones_ext = jnp.ones((L, LANES), MXU_DTYPE)

    def cumulative(lf, left, d):
        hi, lo = _split2(lf)
        mat = tri[d] if left else tri_t[d]
        mm = (lambda x: jnp.dot(mat, x, preferred_element_type=F32)) if left else (
            lambda x: jnp.dot(x, mat, preferred_element_type=F32))
        return mm(hi) + mm(lo)

    chains = [(d, hh) for d in range(2) for hh in range(2)]
    f32dot = functools.partial(jnp.dot, preferred_element_type=F32)

    def chunk_step(it, carry):
        t0s = [pl.multiple_of((it if d == 0 else nc - 1 - it) * L, L) for d in range(2)]
        gpre = [gt_ref[pl.ds(t0, L), :] + gbias for t0 in t0s]
        gpre_t = [gtt_ref[:, pl.ds(t0, L)] for t0 in t0s]
        gm = [jnp.where(is_i, gpre[d], cumulative(jnp.where(is_f, jax.nn.log_sigmoid(gpre[d]), 0.0), True, d))
              for d in range(2)]
        gmt = [jnp.where(is_i_t, gpre_t[d], cumulative(jnp.where(is_f_t, jax.nn.log_sigmoid(gpre_t[d]), 0.0), False, d))
               for d in range(2)]
        gm2 = [_split2(g) for g in gm]
        qc = [qc_ref[pl.ds(t0, L), :] for t0 in t0s]
        kc = [kc_ref[pl.ds(t0, L), :].astype(MXU_DTYPE) for t0 in t0s]
        kt = [kt_ref[:, pl.ds(t0, L)].astype(MXU_DTYPE) for t0 in t0s]
        vc = [v_ref[pl.ds(t0, L), :] for t0 in t0s]
        v_ext = [jnp.concatenate([v.astype(MXU_DTYPE), ones_ext], axis=1) for v in vc]

        rep = [f32dot(gm2[d][0], sel[d][hh]) + f32dot(gm2[d][1], sel[d][hh]) for d, hh in chains]
        b_rep = [r[:, :LANES] for r in rep]
        li_rep = [r[:, LANES:] for r in rep]
        li_row = [gmt[d][d * 2 + hh:d * 2 + hh + 1, :] for d, hh in chains]
        b_row = [gmt[d][4 + d * 2 + hh:5 + d * 2 + hh, :] for d, hh in chains]
        m_st = [m_ref[hh * 2 + d] for d, hh in chains]
        cn = [cn_ref[hh * 2 + d] for d, hh in chains]
        qm = [jnp.where(head_lanes[hh], qc[d], 0.0).astype(MXU_DTYPE) for d, hh in chains]
        s_qk = [lax.dot_general(qm[c], kc[d], (((1,), (1,)), ((), ())), preferred_element_type=F32)
                for c, (d, hh) in enumerate(chains)]
        qcn = [f32dot(qm[c], cn[c].astype(MXU_DTYPE)) for c in range(4)]

        dmat = [jnp.where(keep[d], b_rep[c] - b_row[c] + li_row[c], NEG_INF) for c, (d, hh) in enumerate(chains)]
        inter = [b_rep[c] + m_st[c] for c in range(4)]
        m_j = [jnp.maximum(inter[c], jnp.max(dmat[c], axis=-1, keepdims=True)) for c in range(4)]
        a_mat = [jnp.exp(dmat[c] - m_j[c]) * s_qk[c] for c in range(4)]
        a_int = [jnp.exp(inter[c] - m_j[c]) for c in range(4)]
        av = [f32dot(a_mat[c].astype(MXU_DTYPE), v_ext[d]) for c, (d, hh) in enumerate(chains)]
        num = [av[c][:, :LANES] + a_int[c] * qcn[c][:, :LANES] for c in range(4)]
        den = [av[c][:, LANES:] + a_int[c] * qcn[c][:, LANES:] for c in range(4)]
        h_out = [num[c] / jnp.maximum(jnp.abs(den[c]), jnp.exp(-m_j[c])) for c in range(4)]

        b_last = [b_rep[c][(L - 1 if d == 0 else 0):(L if d == 0 else 1), :] for c, (d, hh) in enumerate(chains)]
        g_row = [b_last[c] - b_row[c] + li_row[c] for c in range(4)]
        g_rep = [b_last[c] - b_rep[c] + li_rep[c] for c in range(4)]
        m_new = [jnp.maximum(b_last[c] + m_st[c], jnp.max(g_row[c], axis=-1, keepdims=True)) for c in range(4)]
        w_rep = [jnp.exp(g_rep[c] - m_new[c]) for c in range(4)]
        decay = [jnp.exp(b_last[c] + m_st[c] - m_new[c]) for c in range(4)]
        wv_ext = [jnp.concatenate([w_rep[c] * vc[d], w_rep[c]], axis=1).astype(MXU_DTYPE)
                  for c, (d, hh) in enumerate(chains)]
        upd = [f32dot(kt[d], wv_ext[c]) for c, (d, hh) in enumerate(chains)]
        for c, (d, hh) in enumerate(chains):
            cn_ref[hh * 2 + d] = jnp.concatenate([decay[c], decay[c]], axis=1) * cn[c] + upd[c]
            m_ref[hh * 2 + d] = m_new[c]
        hf_ref[pl.ds(t0s[0], L), :] = jnp.where(head_lanes[0], h_out[0], h_out[1])
        hb_ref[pl.ds(t0s[1], L), :] = jnp.where(head_lanes[0], h_out[2], h_out[3])
        return carry

    lax.fori_loop(0, nc, chunk_step, 0)

    ng = ng_ref[...]
    lane = lax.broadcasted_iota(I32, (L, LANES), 1)
    upper = lane >= hd

    def out_chunk(ci, carry):
        t0 = pl.multiple_of(ci * L, L)
        hs = hf_ref[pl.ds(t0, L), :] + hb_ref[pl.ds(t0, L), :]
        sq = hs * hs
        ms_lo = jnp.sum(jnp.where(upper, 0.0, sq), axis=-1, keepdims=True) / hd
        ms_hi = jnp.sum(jnp.where(upper, sq, 0.0), axis=-1, keepdims=True) / hd
        ms = jnp.where(upper, ms_hi, ms_lo)
        y = hs * lax.rsqrt(ms + EPS) * ng
        y_ref[pl.ds(t0, L), :] = y * jax.nn.sigmoid(o_ref[pl.ds(t0, L), :])
        return carry

    lax.fori_loop(0, nc, out_chunk, 0)


def _mlstm(proj3, gates3, conv_w, gate_bias_row, norm_g_row, n_heads, cols):
    b, seq, _ = proj3.shape
    qb, kb, vb, ob = cols
    n_pairs = n_heads // 2
    width = n_heads * HEAD_DIM
    blk = lambda base: pl.BlockSpec((None, seq, LANES), lambda bi, p, base=base: (bi, 0, base + p))
    k_conv_base = width // LANES
    return pl.pallas_call(
        functools.partial(_mlstm_body, seq=seq, n_heads=n_heads), grid=(b, n_pairs),
        in_specs=[blk(qb), blk(kb), blk(vb), blk(ob),
                  pl.BlockSpec((None, seq, LANES), lambda bi, p: (bi, 0, p)),
                  pl.BlockSpec((MLSTM_CONV, LANES), lambda bi, p: (0, p)),
                  pl.BlockSpec((MLSTM_CONV, LANES), lambda bi, p: (0, k_conv_base + p)),
                  pl.BlockSpec((1, LANES), lambda bi, p: (0, p)),
                  pl.BlockSpec((1, LANES), lambda bi, p: (0, p))],
        out_specs=pl.BlockSpec((None, seq, LANES), lambda bi, p: (bi, 0, p)),
        out_shape=jax.ShapeDtypeStruct((b, seq, width), F32),
        scratch_shapes=[pltpu.VMEM((seq + 2 * CONV_HALO, LANES), F32), pltpu.VMEM((seq + 2 * CONV_HALO, LANES), F32),
                        pltpu.VMEM((seq, LANES), F32), pltpu.VMEM((seq, LANES), F32),
                        pltpu.VMEM((LANES, seq), F32), pltpu.VMEM((LANES, seq), F32),
                        pltpu.VMEM((seq, LANES), F32), pltpu.VMEM((seq, LANES), F32),
                        pltpu.VMEM((4, LANES, 2 * LANES), F32),
                        pltpu.VMEM((4, 1, LANES), F32)],
        compiler_params=_cparams("parallel", "parallel"), name="mlstm",
    )(proj3, proj3, proj3, proj3, gates3, conv_w, conv_w, gate_bias_row, norm_g_row)


def _group_mean_sq(x, gmat):
    hi = (x * x).astype(MXU_DTYPE)
    lo = (x * x - hi.astype(F32)).astype(MXU_DTYPE)
    ssq = jnp.dot(hi, gmat, preferred_element_type=F32) + jnp.dot(lo, gmat, preferred_element_type=F32)
    return ssq * (1.0 / HEAD_DIM)


def _norm_rope(x, g_row, gmat, cos_t, sin_t):
    width = x.shape[1]
    xn = x * lax.rsqrt(_group_mean_sq(x, gmat) + EPS) * g_row
    lane = lax.broadcasted_iota(I32, x.shape, 1)
    partner = jnp.where(lane % 2 == 0, pltpu.roll(xn, width - 1, 1), pltpu.roll(xn, 1, 1))
    return xn * cos_t + partner * sin_t


def _qkprep_body(q_ref, k_ref, v_ref, qg_ref, kg_ref, gq_ref, gk_ref, cos_ref, sin_ref, qo_ref, kt_ref, vo_ref,
                 *, n_q_heads, n_kv_heads):
    hd = HEAD_DIM
    cos2, sin2 = cos_ref[...], sin_ref[...]
    reps = n_q_heads * hd // LANES
    q = _norm_rope(q_ref[...], qg_ref[...], gq_ref[...],
                   jnp.concatenate([cos2] * reps, axis=1), jnp.concatenate([sin2] * reps, axis=1))
    q = q * (hd ** -0.5 * LOG2_E)
    for h in range(n_q_heads):
        qo_ref[h] = q[:, h * hd:(h + 1) * hd].astype(qo_ref.dtype)
    k = _norm_rope(k_ref[...], kg_ref[...], gk_ref[...], cos2, sin2)
    kt = k.T
    v = v_ref[...]
    kv_width = v.shape[1]
    lane = lax.broadcasted_iota(I32, v.shape, 1)
    for h in range(n_kv_heads):
        kt_ref[h] = kt[h * hd:(h + 1) * hd, :].astype(kt_ref.dtype)
        vh = v if h == 0 else pltpu.roll(v, kv_width - h * hd, 1)
        vo_ref[h] = jnp.where(lane < hd, vh, jnp.where(lane == hd, 1.0, 0.0)).astype(vo_ref.dtype)


def _qkprep(proj3, q_gain_row, k_gain_row, gq, gk, cos2, sin2, n_q_heads, n_kv_heads, cols, ts):
    b, seq, _ = proj3.shape
    qcol, kcol, vcol = cols
    qw, kw = n_q_heads * HEAD_DIM, n_kv_heads * HEAD_DIM
    fixed = lambda bi, i: (0, 0)
    return pl.pallas_call(
        functools.partial(_qkprep_body, n_q_heads=n_q_heads, n_kv_heads=n_kv_heads), grid=(b, seq // ts),
        in_specs=[pl.BlockSpec((None, ts, qw), lambda bi, i: (bi, i, qcol)),
                  pl.BlockSpec((None, ts, kw), lambda bi, i: (bi, i, kcol)),
                  pl.BlockSpec((None, ts, kw), lambda bi, i: (bi, i, vcol)),
                  pl.BlockSpec((1, qw), fixed), pl.BlockSpec((1, kw), fixed),
                  pl.BlockSpec((qw, qw), fixed), pl.BlockSpec((kw, kw), fixed),
                  pl.BlockSpec((ts, kw), lambda bi, i: (i, 0)), pl.BlockSpec((ts, kw), lambda bi, i: (i, 0))],
        out_specs=[pl.BlockSpec((None, n_q_heads, ts, HEAD_DIM), lambda bi, i: (bi, 0, i, 0)),
                   pl.BlockSpec((None, n_kv_heads, HEAD_DIM, ts), lambda bi, i: (bi, 0, 0, i)),
                   pl.BlockSpec((None, n_kv_heads, ts, kw), lambda bi, i: (bi, 0, i, 0))],
        out_shape=[jax.ShapeDtypeStruct((b, n_q_heads, seq, HEAD_DIM), MXU_DTYPE),
                   jax.ShapeDtypeStruct((b, n_kv_heads, HEAD_DIM, seq), MXU_DTYPE),
                   jax.ShapeDtypeStruct((b, n_kv_heads, seq, kw), MXU_DTYPE)],
        compiler_params=_cparams("parallel", "parallel"), name="qk_prep",
    )(proj3, proj3, proj3, q_gain_row, k_gain_row, gq, gk, cos2, sin2)


def _flash_body(q_ref, kt_ref, v_ref, o_ref, *, seq, tq, tk, group):
    hd = HEAD_DIM
    rows = group * tq
    q = q_ref[...].reshape(rows, hd)
    m = jnp.full((rows, 1), NEG_INF, F32)
    acc = jnp.zeros((rows, v_ref.shape[1]), F32)
    for kc in range(seq // tk):
        s = jnp.dot(q, kt_ref[:, kc * tk:(kc + 1) * tk], preferred_element_type=F32)
        m_new = jnp.maximum(m, jnp.max(s, axis=-1, keepdims=True))
        p = jnp.exp2(s - m_new)
        acc = jnp.exp2(m - m_new) * acc + jnp.dot(p.astype(v_ref.dtype), v_ref[kc * tk:(kc + 1) * tk, :],
                                                  preferred_element_type=F32)
        m = m_new
    o = acc[:, :hd] / acc[:, hd:hd + 1]
    o_ref[...] = jnp.concatenate([o[g * tq:(g + 1) * tq] for g in range(group)], axis=1)


def _flash_attention(q4, kt4, v4, tq, tk):
    b, n_q_heads, seq, hd = q4.shape
    n_kv_heads = kt4.shape[1]
    group = n_q_heads // n_kv_heads
    return pl.pallas_call(
        functools.partial(_flash_body, seq=seq, tq=tq, tk=tk, group=group), grid=(b, n_kv_heads, seq // tq),
        in_specs=[pl.BlockSpec((None, group, tq, hd), lambda bi, h, i: (bi, h, i, 0)),
                  pl.BlockSpec((None, None, hd, seq), lambda bi, h, i: (bi, h, 0, 0)),
                  pl.BlockSpec((None, None, seq, v4.shape[3]), lambda bi, h, i: (bi, h, 0, 0))],
        out_specs=pl.BlockSpec((None, tq, group * hd), lambda bi, h, i: (bi, i, h)),
        out_shape=jax.ShapeDtypeStruct((b, seq, n_q_heads * hd), F32),
        compiler_params=_cparams("parallel", "parallel", "parallel"), name="flash_gqa",
    )(q4, kt4, v4)


def _kvproj_body(m_ref, w_ref, o_ref):
    o_ref[...] = _mm(m_ref[...], w_ref[...]).astype(o_ref.dtype)


def _kvproj(mem2d, wkv, tm):
    t, d = mem2d.shape
    n = wkv.shape[1]
    return pl.pallas_call(
        _kvproj_body, grid=(t // tm,),
        in_specs=[pl.BlockSpec((tm, d), lambda i: (i, 0)), pl.BlockSpec((d, n), lambda i: (0, 0))],
        out_specs=pl.BlockSpec((tm, n), lambda i: (i, 0)), out_shape=jax.ShapeDtypeStruct((t, n), MXU_DTYPE),
        compiler_params=_cparams("parallel"), name="mem_kv_proj",
    )(mem2d, wkv)


def _crossattn_body(x_ref, yp_ref, ym_ref, ya_ref, wp_ref, wm_ref, wa_ref, kv_ref, gm_ref, wq_ref, wo_ref, gf_ref,
                    rwt_ref, xo_ref, hf_ref, lg_ref, *, ts):
    x = (x_ref[...] + _mm(yp_ref[...], wp_ref[...]) + _mm(ym_ref[...], wm_ref[...])
         + _mm(ya_ref[...], wa_ref[...]))
    q = _mm(_rms(x, gm_ref[...]), wq_ref[...])
    ca_width = q.shape[1]
    dh = ca_width // CA_HEADS
    kv = kv_ref[...]
    heads = []
    for h in range(CA_HEADS):
        s = _mm_nt(q[:, h * dh:(h + 1) * dh], kv[:, h * dh:(h + 1) * dh]) * (dh ** -0.5)
        e = jnp.exp(s - jnp.max(s, axis=-1, keepdims=True))
        p = e / jnp.sum(e, axis=-1, keepdims=True)
        heads.append(_mm(p, kv[:, ca_width + h * dh:ca_width + (h + 1) * dh]))
    x2 = x + _mm(jnp.concatenate(heads, axis=1), wo_ref[...])
    xo_ref[...] = x2
    hf = _rms(x2, gf_ref[...])
    _to_token_tiles(hf_ref, hf, ts)
    lg_ref[...] = _mm_nt(rwt_ref[...], hf)


def _crossattn(x3, yp, ym, ya, w_out, kv3, g_mem, wq, wo, g_ffn, router_wt, ts):
    b, seq, d = x3.shape
    n_mem, kvw = kv3.shape[1], kv3.shape[2]
    ne = router_wt.shape[0]
    fixed = lambda bi, i: (0, 0)
    tile = lambda w: pl.BlockSpec((None, ts, w), lambda bi, i: (bi, i, 0))
    wp, wm, wa = yp.shape[2], ym.shape[2], ya.shape[2]
    return pl.pallas_call(
        functools.partial(_crossattn_body, ts=ts), grid=(b, seq // ts),
        in_specs=[tile(d), tile(wp), tile(wm), tile(wa),
                  pl.BlockSpec((wp, d), fixed), pl.BlockSpec((wm, d), fixed), pl.BlockSpec((wa, d), fixed),
                  pl.BlockSpec((None, n_mem, kvw), lambda bi, i: (bi, 0, 0)),
                  pl.BlockSpec((1, d), fixed), pl.BlockSpec(wq.shape, fixed), pl.BlockSpec(wo.shape, fixed),
                  pl.BlockSpec((1, d), fixed), pl.BlockSpec((ne, d), fixed)],
        out_specs=[pl.BlockSpec((None, ts, d), lambda bi, i: (bi, i, 0)),
                   pl.BlockSpec((None, ts * SUBLANES, LANES), lambda bi, i: (bi, i, 0)),
                   pl.BlockSpec((None, ne, ts), lambda bi, i: (bi, 0, i))],
        out_shape=[jax.ShapeDtypeStruct((b, seq, d), F32),
                   jax.ShapeDtypeStruct((b, seq * SUBLANES, LANES), F32),
                   jax.ShapeDtypeStruct((b, ne, seq), F32)],
        compiler_params=_cparams("parallel", "parallel"), name="cross_attn",
    )(x3, yp, ym, ya, w_out[:wp], w_out[wp:wp + wm], w_out[wp + wm:], kv3, g_mem, wq, wo, g_ffn, router_wt)


def _excl_prefix(flags, upper_excl, ones):
    rows, seq = flags.shape
    off = jnp.zeros((rows, LANES), F32)
    out = []
    for c in range(seq // LANES):
        xc = flags[:, c * LANES:(c + 1) * LANES].astype(MXU_DTYPE)
        out.append(jnp.dot(xc, upper_excl, preferred_element_type=F32) + off)
        off = off + jnp.dot(xc, ones, preferred_element_type=F32)
    return jnp.concatenate(out, axis=1)


def _router_body(lg_ref, idx_ref, gate_ref, *, seq, cap):
    lg = lg_ref[...]
    ne = lg.shape[0]
    ex = jnp.exp(lg - jnp.max(lg, axis=0, keepdims=True))
    aff = ex / jnp.sum(ex, axis=0, keepdims=True)
    bits = pltpu.bitcast(aff, I32)

    thr = jnp.zeros((ne, 1), I32)
    for bit in range(30, -1, -1):
        cand = thr | (1 << bit)
        cnt = jnp.sum((bits >= cand).astype(F32), axis=1, keepdims=True)
        thr = jnp.where(cnt >= cap, cand, thr)

    r_i = lax.broadcasted_iota(I32, (LANES, LANES), 0)
    c_i = lax.broadcasted_iota(I32, (LANES, LANES), 1)
    upper_excl = (r_i < c_i).astype(MXU_DTYPE)
    ones = jnp.ones((LANES, LANES), MXU_DTYPE)

    gt = bits > thr
    eq = bits == thr
    n_gt = jnp.sum(gt.astype(F32), axis=1, keepdims=True)
    tie_rank = _excl_prefix(eq.astype(F32), upper_excl, ones)
    sel = gt | (eq & (tie_rank < cap - n_gt))
    pos = _excl_prefix(sel.astype(F32), upper_excl, ones).astype(I32)

    n_iota = lax.broadcasted_iota(I32, (ne, seq), 1)
    seq_bits = (seq - 1).bit_length()
    flag_bit = 2 * seq_bits
    packed = jnp.where(sel, n_iota | ((n_iota - pos) << seq_bits) | (1 << flag_bit), 0)
    gbits = jnp.where(sel, bits, 0)
    for k in range(seq_bits):
        sh = 1 << k
        moved_p = pltpu.roll(packed, seq - sh, 1)
        moved_g = pltpu.roll(gbits, seq - sh, 1)
        take = ((moved_p >> flag_bit) & 1) * ((moved_p >> (seq_bits + k)) & 1) == 1
        stay = ((packed >> flag_bit) & 1) * (1 - ((packed >> (seq_bits + k)) & 1)) == 1
        packed = jnp.where(take, moved_p, jnp.where(stay, packed, 0))
        gbits = jnp.where(take, moved_g, jnp.where(stay, gbits, 0))
    idx_ref[...] = packed[:, :cap] & (seq - 1)
    gate_ref[...] = pltpu.bitcast(gbits[:, :cap], F32)


def _router(logits_t, cap):
    b, ne, seq = logits_t.shape
    return pl.pallas_call(
        functools.partial(_router_body, seq=seq, cap=cap), grid=(b,),
        in_specs=[pl.BlockSpec((None, ne, seq), lambda bi: (bi, 0, 0))],
        out_specs=[pl.BlockSpec((None, ne, cap), lambda bi: (bi, 0, 0)),
                   pl.BlockSpec((None, ne, cap), lambda bi: (bi, 0, 0))],
        out_shape=[jax.ShapeDtypeStruct((b, ne, cap), I32), jax.ShapeDtypeStruct((b, ne, cap), F32)],
        compiler_params=_cparams("parallel"), name="ec_router",
    )(logits_t)


def _moe_body(idx_hbm, gate_hbm, h_hbm, wg_ref, wu_ref, wd_ref, out_hbm,
              idx_s, gate_s, xbuf, xbf, ybuf, yacc, acc, gsem, isem, osem, *, seq, cap, ne, n_fc, stride):
    b = pl.program_id(0)
    e = pl.program_id(1)
    c = pl.program_id(2)
    last = pl.num_programs(0) * ne - 1
    step = b * ne + e
    nstep = jnp.minimum(step + 1, last)
    xslot, nxslot = lax.rem(step, 2), lax.rem(step + 1, 2)
    islot, nislot = lax.rem(step, N_IDX_SLOTS), lax.rem(step + 1, N_IDX_SLOTS)
    tile = SUBLANES
    per = cap // n_fc

    def idx_copies(s, sl):
        src = pl.ds(pl.multiple_of(s * stride, stride), stride)
        dst = pl.ds(pl.multiple_of(sl * stride, stride), stride)
        return (pltpu.make_async_copy(idx_hbm.at[src], idx_s.at[dst], isem.at[0, sl]),
                pltpu.make_async_copy(gate_hbm.at[src], gate_s.at[dst], isem.at[1, sl]))

    def row_copy(src_row, k, sl):
        return pltpu.make_async_copy(h_hbm.at[pl.ds(src_row, tile), :],
                                     xbuf.at[sl, pl.ds(k * tile, tile), :], gsem.at[sl])

    def start_rows(s, isl, sl, k0, n):
        base = (s // ne) * seq
        ibase = isl * stride + k0
        for k in range(n):
            row_copy(pl.multiple_of((base + idx_s[ibase + k]) * tile, tile), k0 + k, sl).start()

    def wait_gather(sl):
        pltpu.make_async_copy(h_hbm.at[pl.ds(0, cap * tile), :], xbuf.at[sl], gsem.at[sl]).wait()

    @pl.when(c == 0)
    def _():
        @pl.when(step == 0)
        def _():
            for cp in idx_copies(0, 0):
                cp.start()
            for cp in idx_copies(0, 0):
                cp.wait()
            for cp in idx_copies(jnp.minimum(1, last), 1):
                cp.start()

            def body(j, carry):
                start_rows(0, 0, 0, j * SUBLANES, SUBLANES)
                return carry
            lax.fori_loop(0, cap // SUBLANES, body, 0)

        for cp in idx_copies(0, nislot):
            cp.wait()
        for cp in idx_copies(jnp.minimum(step + 2, last), lax.rem(step + 2, N_IDX_SLOTS)):
            cp.start()
        wait_gather(xslot)
        xbf[...] = _from_token_tiles(xbuf.at[xslot], cap).astype(xbf.dtype)

        @pl.when(e == 0)
        def _():
            acc[...] = jnp.zeros(acc.shape, F32)

    start_rows(nstep, nislot, nxslot, pl.multiple_of(c * per, per), per)

    x = xbf[...]
    g = jnp.dot(x, wg_ref[...], preferred_element_type=F32)
    u = jnp.dot(x, wu_ref[...], preferred_element_type=F32)
    y = _mm(g * jax.nn.sigmoid(g) * u, wd_ref[...])

    @pl.when(c == 0)
    def _():
        yacc[...] = y

    @pl.when(c > 0)
    def _():
        yacc[...] += y

    @pl.when(c == n_fc - 1)
    def _():
        _to_token_tiles(ybuf, yacc[...], cap)

        ibase = islot * stride
        for k0 in range(0, cap, SCATTER_GROUP):
            rows = [pl.multiple_of(idx_s[ibase + k0 + j] * tile, tile) for j in range(SCATTER_GROUP)]
            new = [acc[pl.ds(rows[j], tile), :] + gate_s[ibase + k0 + j] * ybuf[(k0 + j) * tile:(k0 + j + 1) * tile, :]
                   for j in range(SCATTER_GROUP)]
            for j in range(SCATTER_GROUP):
                acc[pl.ds(rows[j], tile), :] = new[j]

        @pl.when(e == ne - 1)
        def _():
            cp = pltpu.make_async_copy(acc, out_hbm.at[b], osem)
            cp.start()
            cp.wait()

        @pl.when(step == last)
        def _():
            wait_gather(nxslot)
            for cp in idx_copies(0, lax.rem(step + 2, N_IDX_SLOTS)):
                cp.wait()


def _moe(idx, gate, h_tiles, w_gu, w_down, batch, seq, fc):
    cap = idx.shape[2]
    ne, d, f2 = w_gu.shape
    f = f2 // 2
    n_fc = f // fc
    stride = -(-cap // SMEM_1D_TILE) * SMEM_1D_TILE
    flat = lambda a: jnp.pad(a, ((0, 0), (0, 0), (0, stride - cap))).reshape(-1)
    idx3, gate3 = flat(idx), flat(gate)
    return pl.pallas_call(
        functools.partial(_moe_body, seq=seq, cap=cap, ne=ne, n_fc=n_fc, stride=stride), grid=(batch, ne, n_fc),
        in_specs=[pl.BlockSpec(memory_space=pl.ANY), pl.BlockSpec(memory_space=pl.ANY),
                  pl.BlockSpec(memory_space=pl.ANY),
                  pl.BlockSpec((None, d, fc), lambda b, e, c: (e, 0, c)),
                  pl.BlockSpec((None, d, fc), lambda b, e, c: (e, 0, n_fc + c)),
                  pl.BlockSpec((None, fc, d), lambda b, e, c: (e, c, 0))],
        out_specs=pl.BlockSpec(memory_space=pl.ANY),
        out_shape=jax.ShapeDtypeStruct((batch, seq * SUBLANES, LANES), F32),
        scratch_shapes=[pltpu.SMEM((N_IDX_SLOTS * stride,), I32), pltpu.SMEM((N_IDX_SLOTS * stride,), F32),
                        pltpu.VMEM((2, cap * SUBLANES, LANES), F32), pltpu.VMEM((cap, d), MXU_DTYPE),
                        pltpu.VMEM((cap * SUBLANES, LANES), F32), pltpu.VMEM((cap, d), F32),
                        pltpu.VMEM((seq * SUBLANES, LANES), F32),
                        pltpu.SemaphoreType.DMA((2,)), pltpu.SemaphoreType.DMA((2, N_IDX_SLOTS)),
                        pltpu.SemaphoreType.DMA],
        compiler_params=_cparams("arbitrary", "arbitrary", "arbitrary"), name="ec_moe",
    )(idx3, gate3, h_tiles, w_gu, w_gu, w_down)


def _final_body(x_ref, moe_ref, g_ref, o_ref, *, tm):
    o_ref[...] = _rms(x_ref[...] + _from_token_tiles(moe_ref, tm), g_ref[...])


def _final_norm(x2d, moe_tiles, g, tm):
    t, d = x2d.shape
    row = lambda i: (i, 0)
    return pl.pallas_call(
        functools.partial(_final_body, tm=tm), grid=(t // tm,),
        in_specs=[pl.BlockSpec((tm, d), row), pl.BlockSpec((tm * SUBLANES, LANES), row),
                  pl.BlockSpec((1, d), lambda i: (0, 0))],
        out_specs=pl.BlockSpec((tm, d), row), out_shape=jax.ShapeDtypeStruct((t, d), F32),
        compiler_params=_cparams("parallel"), name="final_norm",
    )(x2d, moe_tiles, g)


def _rope_tables(seq):
    rows = seq // GRID_W
    row_ids = jnp.repeat(jnp.arange(rows), GRID_W).astype(F32)
    col_ids = jnp.tile(jnp.arange(GRID_W), rows).astype(F32)
    n_freq = HEAD_DIM // 4
    inv_freq = ROPE_THETA ** (-jnp.arange(n_freq, dtype=F32) / n_freq)
    ang = jnp.concatenate([row_ids[:, None] * inv_freq, col_ids[:, None] * inv_freq], axis=-1)
    cos, sin = jnp.cos(ang), jnp.sin(ang)
    cos_i = jnp.repeat(cos, 2, axis=1)
    sin_i = jnp.stack([-sin, sin], axis=-1).reshape(seq, HEAD_DIM)
    return jnp.tile(cos_i, (1, LANES // HEAD_DIM)), jnp.tile(sin_i, (1, LANES // HEAD_DIM))


def _group_matrix(width):
    gid = jnp.arange(width) // HEAD_DIM
    return (gid[:, None] == gid[None, :]).astype(MXU_DTYPE)


def _block_diag(w):
    g, c, _ = w.shape
    eye = jnp.eye(g, dtype=w.dtype)
    return (eye[:, None, :, None] * w[:, :, None, :]).reshape(g * c, g * c)


def _pick_tile(n, pref):
    t = min(n, pref)
    while n % t:
        t //= 2
    return t


def kernel(x, mem, norm_mix_g, w_in, pool_w, pool_scale, mlstm_conv_w, mlstm_gate_b, mlstm_norm_g, q_norm_g,
           k_norm_g, w_out, norm_mem_g, ca_wq, ca_wkv, ca_wo, norm_ffn_g, router_w, expert_w_gu, expert_w_down,
           final_norm_g):
    batch, seq, d = x.shape
    depth = w_in.shape[0]
    n_mem = mem.shape[1]
    pool_width = pool_scale.shape[1]
    m_width = mlstm_norm_g.shape[1]
    m_heads = m_width // HEAD_DIM
    n_gate = 2 * m_heads
    kv_width = (w_in.shape[2] - pool_width - 4 * m_width - 2 * n_gate) // 6
    q_width = 4 * kv_width
    n_q_heads, n_kv_heads = q_width // HEAD_DIM, kv_width // HEAD_DIM
    ne = router_w.shape[2]
    cap = EC_CAPACITY_FACTOR * seq // ne
    f_exp = expert_w_down.shape[2]
    t = batch * seq

    o_pool = 0
    o_mq = o_pool + pool_width
    o_mi = o_mq + 4 * m_width
    o_aq = o_mi + 2 * n_gate
    o_ak = o_aq + q_width
    c_pool = q_width // pool_width
    c_mq = (q_width + pool_width) // LANES
    m_cols = tuple(c_mq + j * (m_width // LANES) for j in range(4))
    c_ak = (q_width + pool_width + 4 * m_width) // kv_width
    cos2, sin2 = _rope_tables(seq)
    gq, gk = _group_matrix(q_width), _group_matrix(kv_width)
    n_pairs = m_heads // 2
    gate_cols = [[kind * n_gate + dr * m_heads + 2 * p + hh for kind in range(2) for dr in range(2) for hh in range(2)]
                 for p in range(n_pairs)]

    tm = _pick_tile(t, 512)
    ts = _pick_tile(seq, 512)
    x2d = x.reshape(t, d)
    mem2d = mem.reshape(batch * n_mem, d)
    moe_tiles = None
    for layer in range(depth):
        wl = w_in[layer]
        w_main = jnp.concatenate([wl[:, o_aq:o_ak], wl[:, o_pool:o_mi], wl[:, o_ak:]], axis=1).astype(MXU_DTYPE)
        w_gate = jnp.concatenate(
            [jnp.pad(wl[:, o_mi:o_aq][:, jnp.array(cols)], ((0, 0), (0, LANES - len(cols)))) for cols in gate_cols],
            axis=1).astype(MXU_DTYPE)
        gb_row = jnp.concatenate(
            [jnp.pad(mlstm_gate_b[layer][jnp.array(cols)], (0, LANES - len(cols))) for cols in gate_cols])[None]
        x2d, proj, gates = _inproj(x2d, moe_tiles, norm_mix_g[layer][None], w_main, w_gate, tm)
        proj3 = proj.reshape(batch, seq, proj.shape[1])
        gates3 = gates.reshape(batch, seq, gates.shape[1])

        y_pool = _pool_mixer(proj3, _block_diag(pool_w[layer]).astype(MXU_DTYPE), pool_scale[layer][None], c_pool,
                             _pick_tile(seq, 256))
        y_mlstm = _mlstm(proj3, gates3, mlstm_conv_w[layer], gb_row, mlstm_norm_g[layer][None], m_heads, m_cols)
        q4, kt4, v4 = _qkprep(proj3, jnp.tile(q_norm_g[layer], n_q_heads)[None],
                              jnp.tile(k_norm_g[layer], n_kv_heads)[None], gq, gk, cos2, sin2,
                              n_q_heads, n_kv_heads, (0, c_ak, c_ak + 1), ts)
        y_attn = _flash_attention(q4, kt4, v4, _pick_tile(seq, 256), _pick_tile(seq, 1024))

        kv = _kvproj(mem2d, ca_wkv[layer].astype(MXU_DTYPE), _pick_tile(batch * n_mem, 512))
        x3, h_tiles, logits_t = _crossattn(
            x2d.reshape(batch, seq, d), y_pool, y_mlstm, y_attn, w_out[layer].astype(MXU_DTYPE),
            kv.reshape(batch, n_mem, kv.shape[1]), norm_mem_g[layer][None],
            ca_wq[layer].astype(MXU_DTYPE), ca_wo[layer].astype(MXU_DTYPE), norm_ffn_g[layer][None],
            router_w[layer].T.astype(MXU_DTYPE), ts)
        x2d = x3.reshape(t, d)

        idx, gate = _router(logits_t, cap)
        moe_tiles = _moe(idx, gate, h_tiles.reshape(t * SUBLANES, LANES), expert_w_gu[layer].astype(MXU_DTYPE),
                         expert_w_down[layer].astype(MXU_DTYPE), batch, seq, _pick_tile(f_exp, 1024))
        moe_tiles = moe_tiles.reshape(t * SUBLANES, LANES)
    out = _final_norm(x2d, moe_tiles, final_norm_g[None], tm)
    return out.reshape(batch, seq, d)
```

```python
import functools
import math

import jax
import jax.numpy as jnp
from jax import lax
from jax.experimental import pallas as pl
from jax.experimental.pallas import tpu as pltpu

F32 = jnp.float32
I32 = jnp.int32
MXU_DTYPE = jnp.bfloat16

LANES = 128
SUBLANES = 8
VMEM_LIMIT_BYTES = 56 * 1024 * 1024

EPS = 1e-6
GRID_W = 64
HEAD_DIM = 64
POOL_WINDOWS = (2, 4, 8, 16)
POOL_HALO = 8
MLSTM_CHUNK = 128
MLSTM_CONV = 5
CONV_HALO = 8
ROPE_THETA = 10000.0
CA_HEADS = 4
N_EXPERTS = 16
EC_CAPACITY_FACTOR = 2
SMEM_1D_TILE = 1024
N_IDX_SLOTS = 3
MOE_SUB_COLS = 512
SCATTER_GROUP = 16
NEG_INF = float("-inf")
LOG2_E = math.log2(math.e)


def _cparams(*sem):
    return pltpu.CompilerParams(dimension_semantics=sem, vmem_limit_bytes=VMEM_LIMIT_BYTES)


def _mm(a, b):
    return jnp.dot(a.astype(MXU_DTYPE), b.astype(MXU_DTYPE), preferred_element_type=F32)


def _mm_nt(a, b):
    return lax.dot_general(a.astype(MXU_DTYPE), b.astype(MXU_DTYPE), (((1,), (1,)), ((), ())),
                           preferred_element_type=F32)


def _split2(x):
    hi = x.astype(MXU_DTYPE)
    return hi, (x - hi.astype(F32)).astype(MXU_DTYPE)


def _split3(x):
    hi = x.astype(MXU_DTYPE)
    r1 = x - hi.astype(F32)
    mid = r1.astype(MXU_DTYPE)
    lo = (r1 - mid.astype(F32)).astype(MXU_DTYPE)
    return hi, mid, lo


def _rms(x, g):
    return x * lax.rsqrt(jnp.mean(x * x, axis=-1, keepdims=True) + EPS) * g


def _from_token_tiles(ref, rows):
    return jnp.concatenate([ref[pl.ds(s, rows, stride=SUBLANES), :] for s in range(SUBLANES)], axis=1)


def _to_token_tiles(ref, val, rows):
    for s in range(SUBLANES):
        ref[pl.ds(s, rows, stride=SUBLANES), :] = val[:, s * LANES:(s + 1) * LANES]


def _inproj_body(x_ref, g_ref, w_ref, wg_ref, o_ref, og_ref):
    hb = _rms(x_ref[...], g_ref[...]).astype(MXU_DTYPE)
    o_ref[...] = jnp.dot(hb, w_ref[...], preferred_element_type=F32)
    og_ref[...] = jnp.dot(hb, wg_ref[...], preferred_element_type=F32)


def _inproj_merge_body(x_ref, moe_ref, g_ref, w_ref, wg_ref, xo_ref, o_ref, og_ref, *, tm):
    x = x_ref[...] + _from_token_tiles(moe_ref, tm)
    xo_ref[...] = x
    hb = _rms(x, g_ref[...]).astype(MXU_DTYPE)
    o_ref[...] = jnp.dot(hb, w_ref[...], preferred_element_type=F32)
    og_ref[...] = jnp.dot(hb, wg_ref[...], preferred_element_type=F32)


def _inproj(x2d, moe_tiles, g, w_main, w_gate, tm):
    t, d = x2d.shape
    n = w_main.shape[1]
    grid = (t // tm,)
    row = lambda i: (i, 0)
    fixed = lambda i: (0, 0)
    gw = w_gate.shape[1]
    w_specs = [pl.BlockSpec((1, d), fixed), pl.BlockSpec((d, n), fixed), pl.BlockSpec((d, gw), fixed)]
    o_specs = [pl.BlockSpec((tm, n), row), pl.BlockSpec((tm, gw), row)]
    o_shapes = [jax.ShapeDtypeStruct((t, n), F32), jax.ShapeDtypeStruct((t, gw), F32)]
    if moe_tiles is None:
        proj, gates = pl.pallas_call(
            _inproj_body, grid=grid, in_specs=[pl.BlockSpec((tm, d), row)] + w_specs, out_specs=o_specs,
            out_shape=o_shapes, compiler_params=_cparams("parallel"), name="inproj",
        )(x2d, g, w_main, w_gate)
        return x2d, proj, gates
    xo, proj, gates = pl.pallas_call(
        functools.partial(_inproj_merge_body, tm=tm), grid=grid,
        in_specs=[pl.BlockSpec((tm, d), row), pl.BlockSpec((tm * SUBLANES, LANES), row)] + w_specs,
        out_specs=[pl.BlockSpec((tm, d), row)] + o_specs,
        out_shape=[jax.ShapeDtypeStruct((t, d), F32)] + o_shapes,
        compiler_params=_cparams("parallel"), name="inproj_merge",
    )(x2d, moe_tiles, g, w_main, w_gate)
    return xo, proj, gates


def _pool_body(u_ref, w_ref, sc_ref, o_ref, upad_ref, *, seq, tc):
    i = pl.program_id(1)

    @pl.when(i == 0)
    def _():
        zeros = jnp.zeros((POOL_HALO, u_ref.shape[1]), F32)
        upad_ref[0:POOL_HALO, :] = zeros
        upad_ref[POOL_HALO + seq:POOL_HALO + seq + POOL_HALO, :] = zeros
        upad_ref[POOL_HALO:POOL_HALO + seq, :] = u_ref[...]

    t0 = pl.multiple_of(i * tc, tc)
    win_rows = tc + 2 * POOL_HALO
    t = t0 + lax.broadcasted_iota(I32, (tc, LANES), 0)
    lane = lax.broadcasted_iota(I32, (tc, LANES), 1)
    upper = lane >= (LANES // 2)
    halves = []
    for hf in range(2):
        w_lo, w_hi = POOL_WINDOWS[2 * hf], POOL_WINDOWS[2 * hf + 1]
        win = upad_ref[pl.ds(t0, win_rows), hf * LANES:(hf + 1) * LANES]
        acc = jnp.zeros((tc, LANES), F32)
        for j in range(-(w_hi // 2), w_hi // 2):
            piece = pltpu.roll(win, (-j) % win_rows, 0)[POOL_HALO:POOL_HALO + tc]
            if -(w_lo // 2) <= j < w_lo // 2:
                acc = acc + piece
            else:
                acc = acc + jnp.where(upper, piece, 0.0)
        half_w = jnp.where(upper, w_hi // 2, w_lo // 2)
        cnt = jnp.minimum(t + half_w, seq) - jnp.maximum(t - half_w, 0)
        halves.append(acc / cnt.astype(F32) - win[POOL_HALO:POOL_HALO + tc])
    d = jnp.concatenate(halves, axis=1)
    o_ref[...] = _mm(d, w_ref[...]) * sc_ref[...]


def _pool_mixer(proj3, w_bd, scale, col_block, tc):
    b, seq, _ = proj3.shape
    pw = w_bd.shape[0]
    return pl.pallas_call(
        functools.partial(_pool_body, seq=seq, tc=tc), grid=(b, seq // tc),
        in_specs=[pl.BlockSpec((None, seq, pw), lambda bi, i: (bi, 0, col_block)),
                  pl.BlockSpec((pw, pw), lambda bi, i: (0, 0)),
                  pl.BlockSpec((1, pw), lambda bi, i: (0, 0))],
        out_specs=pl.BlockSpec((None, tc, pw), lambda bi, i: (bi, i, 0)),
        out_shape=jax.ShapeDtypeStruct((b, seq, pw), F32),
        scratch_shapes=[pltpu.VMEM((seq + 2 * POOL_HALO, pw), F32)],
        compiler_params=_cparams("parallel", "arbitrary"), name="pool_mixer",
    )(proj3, w_bd, scale)


def _mlstm_body(q_ref, k_ref, v_ref, o_ref, gt_ref, cwq_ref, cwk_ref, gb_ref, ng_ref, y_ref,
                qpad_ref, kpad_ref, qc_ref, kc_ref, kt_ref, gtt_ref, hf_ref, hb_ref, cn_ref, m_ref,
                *, seq, n_heads):
    L = MLSTM_CHUNK
    nc = seq // L
    hd = HEAD_DIM
    gbias = gb_ref[...]

    zeros = jnp.zeros((CONV_HALO, LANES), F32)
    for pad_ref, src_ref in ((qpad_ref, q_ref), (kpad_ref, k_ref)):
        pad_ref[0:CONV_HALO, :] = zeros
        pad_ref[CONV_HALO + seq:CONV_HALO + seq + CONV_HALO, :] = zeros
        pad_ref[CONV_HALO:CONV_HALO + seq, :] = src_ref[...]
    win_rows = L + 2 * CONV_HALO

    def conv_chunk(ci, carry):
        t0 = pl.multiple_of(ci * L, L)
        for pad_ref, cw_ref, dst_ref, post in ((qpad_ref, cwq_ref, qc_ref, 1.0),
                                               (kpad_ref, cwk_ref, kc_ref, 1.0 / math.sqrt(hd))):
            win = pad_ref[pl.ds(t0, win_rows), :]
            acc = jnp.zeros((L, LANES), F32)
            for j in range(MLSTM_CONV):
                off = j - MLSTM_CONV // 2
                piece = pltpu.roll(win, (-off) % win_rows, 0)[CONV_HALO:CONV_HALO + L]
                acc = acc + piece * cw_ref[j:j + 1, :]
            act = acc * jax.nn.sigmoid(acc)
            act = act * post if post != 1.0 else act
            dst_ref[pl.ds(t0, L), :] = act
            if dst_ref is kc_ref:
                kt_ref[:, pl.ds(t0, L)] = act.T
        gtt_ref[:, pl.ds(t0, L)] = (gt_ref[pl.ds(t0, L), :] + gbias).T
        return carry

    lax.fori_loop(0, nc, conv_chunk, 0)

    cn_ref[...] = jnp.zeros(cn_ref.shape, F32)
    m_ref[...] = jnp.zeros(m_ref.shape, F32)

    row = lax.broadcasted_iota(I32, (L, L), 0)
    col = lax.broadcasted_iota(I32, (L, L), 1)
    tri = ((col <= row).astype(MXU_DTYPE), (col >= row).astype(MXU_DTYPE))
    tri_t = (tri[1], tri[0])
    keep = (col <= row, col >= row)
    gate_lane = lax.broadcasted_iota(I32, (L, LANES), 1)
    gate_sub = lax.broadcasted_iota(I32, (LANES, L), 0)
    is_i, is_f = gate_lane < 4, (gate_lane >= 4) & (gate_lane < 8)
    is_i_t, is_f_t = gate_sub < 4, (gate_sub >= 4) & (gate_sub < 8)
    head_lanes = (gate_lane < hd, gate_lane >= hd)
    sel_r = lax.broadcasted_iota(I32, (LANES, 2 * LANES), 0)
    sel_c = lax.broadcasted_iota(I32, (LANES, 2 * LANES), 1)
    sel = [[(sel_r == jnp.where(sel_c < LANES, 4 + d * 2 + hh, d * 2 + hh)).astype(MXU_DTYPE) for hh in range(2)]
           for d in range(2)]
    ones_ext = jnp.ones((L, LANES), MXU_DTYPE)

    def cumulative(lf, left, d):
        hi, lo = _split2(lf)
        mat = tri[d] if left else tri_t[d]
        mm = (lambda x: jnp.dot(mat, x, preferred_element_type=F32)) if left else (
            lambda x: jnp.dot(x, mat, preferred_element_type=F32))
        return mm(hi) + mm(lo)

    chains = [(d, hh) for d in range(2) for hh in range(2)]
    f32dot = functools.partial(jnp.dot, preferred_element_type=F32)

    def chunk_step(it, carry):
        t0s = [pl.multiple_of((it if d == 0 else nc - 1 - it) * L, L) for d in range(2)]
        gpre = [gt_ref[pl.ds(t0, L), :] + gbias for t0 in t0s]
        gpre_t = [gtt_ref[:, pl.ds(t0, L)] for t0 in t0s]
        gm = [jnp.where(is_i, gpre[d], cumulative(jnp.where(is_f, jax.nn.log_sigmoid(gpre[d]), 0.0), True, d))
              for d in range(2)]
        gmt = [jnp.where(is_i_t, gpre_t[d], cumulative(jnp.where(is_f_t, jax.nn.log_sigmoid(gpre_t[d]), 0.0), False, d))
               for d in range(2)]
        gm2 = [_split2(g) for g in gm]
        qc = [qc_ref[pl.ds(t0, L), :] for t0 in t0s]
        kc = [kc_ref[pl.ds(t0, L), :].astype(MXU_DTYPE) for t0 in t0s]
        kt = [kt_ref[:, pl.ds(t0, L)].astype(MXU_DTYPE) for t0 in t0s]
        vc = [v_ref[pl.ds(t0, L), :] for t0 in t0s]
        v_ext = [jnp.concatenate([v.astype(MXU_DTYPE), ones_ext], axis=1) for v in vc]

        rep = [f32dot(gm2[d][0], sel[d][hh]) + f32dot(gm2[d][1], sel[d][hh]) for d, hh in chains]
        b_rep = [r[:, :LANES] for r in rep]
        li_rep = [r[:, LANES:] for r in rep]
        li_row = [gmt[d][d * 2 + hh:d * 2 + hh + 1, :] for d, hh in chains]
        b_row = [gmt[d][4 + d * 2 + hh:5 + d * 2 + hh, :] for d, hh in chains]
        m_st = [m_ref[hh * 2 + d] for d, hh in chains]
        cn = [cn_ref[hh * 2 + d] for d, hh in chains]
        qm = [jnp.where(head_lanes[hh], qc[d], 0.0).astype(MXU_DTYPE) for d, hh in chains]
        s_qk = [lax.dot_general(qm[c], kc[d], (((1,), (1,)), ((), ())), preferred_element_type=F32)
                for c, (d, hh) in enumerate(chains)]
        qcn = [f32dot(qm[c], cn[c].astype(MXU_DTYPE)) for c in range(4)]

        dmat = [jnp.where(keep[d], b_rep[c] - b_row[c] + li_row[c], NEG_INF) for c, (d, hh) in enumerate(chains)]
        inter = [b_rep[c] + m_st[c] for c in range(4)]
        m_j = [jnp.maximum(inter[c], jnp.max(dmat[c], axis=-1, keepdims=True)) for c in range(4)]
        a_mat = [jnp.exp(dmat[c] - m_j[c]) * s_qk[c] for c in range(4)]
        a_int = [jnp.exp(inter[c] - m_j[c]) for c in range(4)]
        av = [f32dot(a_mat[c].astype(MXU_DTYPE), v_ext[d]) for c, (d, hh) in enumerate(chains)]
        num = [av[c][:, :LANES] + a_int[c] * qcn[c][:, :LANES] for c in range(4)]
        den = [av[c][:, LANES:] + a_int[c] * qcn[c][:, LANES:] for c in range(4)]
        h_out = [num[c] / jnp.maximum(jnp.abs(den[c]), jnp.exp(-m_j[c])) for c in range(4)]

        b_last = [b_rep[c][(L - 1 if d == 0 else 0):(L if d == 0 else 1), :] for c, (d, hh) in enumerate(chains)]
        g_row = [b_last[c] - b_row[c] + li_row[c] for c in range(4)]
        g_rep = [b_last[c] - b_rep[c] + li_rep[c] for c in range(4)]
        m_new = [jnp.maximum(b_last[c] + m_st[c], jnp.max(g_row[c], axis=-1, keepdims=True)) for c in range(4)]
        w_rep = [jnp.exp(g_rep[c] - m_new[c]) for c in range(4)]
        decay = [jnp.exp(b_last[c] + m_st[c] - m_new[c]) for c in range(4)]
        wv_ext = [jnp.concatenate([w_rep[c] * vc[d], w_rep[c]], axis=1).astype(MXU_DTYPE)
                  for c, (d, hh) in enumerate(chains)]
        upd = [f32dot(kt[d], wv_ext[c]) for c, (d, hh) in enumerate(chains)]
        for c, (d, hh) in enumerate(chains):
            cn_ref[hh * 2 + d] = jnp.concatenate([decay[c], decay[c]], axis=1) * cn[c] + upd[c]
            m_ref[hh * 2 + d] = m_new[c]
        hf_ref[pl.ds(t0s[0], L), :] = jnp.where(head_lanes[0], h_out[0], h_out[1])
        hb_ref[pl.ds(t0s[1], L), :] = jnp.where(head_lanes[0], h_out[2], h_out[3])
        return carry

    lax.fori_loop(0, nc, chunk_step, 0)

    ng = ng_ref[...]
    lane = lax.broadcasted_iota(I32, (L, LANES), 1)
    upper = lane >= hd

    def out_chunk(ci, carry):
        t0 = pl.multiple_of(ci * L, L)
        hs = hf_ref[pl.ds(t0, L), :] + hb_ref[pl.ds(t0, L), :]
        sq = hs * hs
        ms_lo = jnp.sum(jnp.where(upper, 0.0, sq), axis=-1, keepdims=True) / hd
        ms_hi = jnp.sum(jnp.where(upper, sq, 0.0), axis=-1, keepdims=True) / hd
        ms = jnp.where(upper, ms_hi, ms_lo)
        y = hs * lax.rsqrt(ms + EPS) * ng
        y_ref[pl.ds(t0, L), :] = y * jax.nn.sigmoid(o_ref[pl.ds(t0, L), :])
        return carry

    lax.fori_loop(0, nc, out_chunk, 0)


def _mlstm(proj3, gates3, conv_w, gate_bias_row, norm_g_row, n_heads, cols):
    b, seq, _ = proj3.shape
    qb, kb, vb, ob = cols
    n_pairs = n_heads // 2
    width = n_heads * HEAD_DIM
    blk = lambda base: pl.BlockSpec((None, seq, LANES), lambda bi, p, base=base: (bi, 0, base + p))
    k_conv_base = width // LANES
    return pl.pallas_call(
        functools.partial(_mlstm_body, seq=seq, n_heads=n_heads), grid=(b, n_pairs),
        in_specs=[blk(qb), blk(kb), blk(vb), blk(ob),
                  pl.BlockSpec((None, seq, LANES), lambda bi, p: (bi, 0, p)),
                  pl.BlockSpec((MLSTM_CONV, LANES), lambda bi, p: (0, p)),
                  pl.BlockSpec((MLSTM_CONV, LANES), lambda bi, p: (0, k_conv_base + p)),
                  pl.BlockSpec((1, LANES), lambda bi, p: (0, p)),
                  pl.BlockSpec((1, LANES), lambda bi, p: (0, p))],
        out_specs=pl.BlockSpec((None, seq, LANES), lambda bi, p: (bi, 0, p)),
        out_shape=jax.ShapeDtypeStruct((b, seq, width), F32),
        scratch_shapes=[pltpu.VMEM((seq + 2 * CONV_HALO, LANES), F32), pltpu.VMEM((seq + 2 * CONV_HALO, LANES), F32),
                        pltpu.VMEM((seq, LANES), F32), pltpu.VMEM((seq, LANES), F32),
                        pltpu.VMEM((LANES, seq), F32), pltpu.VMEM((LANES, seq), F32),
                        pltpu.VMEM((seq, LANES), F32), pltpu.VMEM((seq, LANES), F32),
                        pltpu.VMEM((4, LANES, 2 * LANES), F32),
                        pltpu.VMEM((4, 1, LANES), F32)],
        compiler_params=_cparams("parallel", "parallel"), name="mlstm",
    )(proj3, proj3, proj3, proj3, gates3, conv_w, conv_w, gate_bias_row, norm_g_row)


def _group_mean_sq(x, gmat):
    hi = (x * x).astype(MXU_DTYPE)
    lo = (x * x - hi.astype(F32)).astype(MXU_DTYPE)
    ssq = jnp.dot(hi, gmat, preferred_element_type=F32) + jnp.dot(lo, gmat, preferred_element_type=F32)
    return ssq * (1.0 / HEAD_DIM)


def _norm_rope(x, g_row, gmat, cos_t, sin_t):
    width = x.shape[1]
    xn = x * lax.rsqrt(_group_mean_sq(x, gmat) + EPS) * g_row
    lane = lax.broadcasted_iota(I32, x.shape, 1)
    partner = jnp.where(lane % 2 == 0, pltpu.roll(xn, width - 1, 1), pltpu.roll(xn, 1, 1))
    return xn * cos_t + partner * sin_t


def _qkprep_body(q_ref, k_ref, v_ref, qg_ref, kg_ref, gq_ref, gk_ref, cos_ref, sin_ref, qo_ref, kt_ref, vo_ref,
                 *, n_q_heads, n_kv_heads):
    hd = HEAD_DIM
    cos2, sin2 = cos_ref[...], sin_ref[...]
    reps = n_q_heads * hd // LANES
    q = _norm_rope(q_ref[...], qg_ref[...], gq_ref[...],
                   jnp.concatenate([cos2] * reps, axis=1), jnp.concatenate([sin2] * reps, axis=1))
    q = q * (hd ** -0.5 * LOG2_E)
    for h in range(n_q_heads):
        qo_ref[h] = q[:, h * hd:(h + 1) * hd].astype(qo_ref.dtype)
    k = _norm_rope(k_ref[...], kg_ref[...], gk_ref[...], cos2, sin2)
    kt = k.T
    v = v_ref[...]
    kv_width = v.shape[1]
    lane = lax.broadcasted_iota(I32, v.shape, 1)
    for h in range(n_kv_heads):
        kt_ref[h] = kt[h * hd:(h + 1) * hd, :].astype(kt_ref.dtype)
        vh = v if h == 0 else pltpu.roll(v, kv_width - h * hd, 1)
        vo_ref[h] = jnp.where(lane < hd, vh, jnp.where(lane == hd, 1.0, 0.0)).astype(vo_ref.dtype)


def _qkprep(proj3, q_gain_row, k_gain_row, gq, gk, cos2, sin2, n_q_heads, n_kv_heads, cols, ts):
    b, seq, _ = proj3.shape
    qcol, kcol, vcol = cols
    qw, kw = n_q_heads * HEAD_DIM, n_kv_heads * HEAD_DIM
    fixed = lambda bi, i: (0, 0)
    return pl.pallas_call(
        functools.partial(_qkprep_body, n_q_heads=n_q_heads, n_kv_heads=n_kv_heads), grid=(b, seq // ts),
        in_specs=[pl.BlockSpec((None, ts, qw), lambda bi, i: (bi, i, qcol)),
                  pl.BlockSpec((None, ts, kw), lambda bi, i: (bi, i, kcol)),
                  pl.BlockSpec((None, ts, kw), lambda bi, i: (bi, i, vcol)),
                  pl.BlockSpec((1, qw), fixed), pl.BlockSpec((1, kw), fixed),
                  pl.BlockSpec((qw, qw), fixed), pl.BlockSpec((kw, kw), fixed),
                  pl.BlockSpec((ts, kw), lambda bi, i: (i, 0)), pl.BlockSpec((ts, kw), lambda bi, i: (i, 0))],
        out_specs=[pl.BlockSpec((None, n_q_heads, ts, HEAD_DIM), lambda bi, i: (bi, 0, i, 0)),
                   pl.BlockSpec((None, n_kv_heads, HEAD_DIM, ts), lambda bi, i: (bi, 0, 0, i)),
                   pl.BlockSpec((None, n_kv_heads, ts, kw), lambda bi, i: (bi, 0, i, 0))],
        out_shape=[jax.ShapeDtypeStruct((b, n_q_heads, seq, HEAD_DIM), MXU_DTYPE),
                   jax.ShapeDtypeStruct((b, n_kv_heads, HEAD_DIM, seq), MXU_DTYPE),
                   jax.ShapeDtypeStruct((b, n_kv_heads, seq, kw), MXU_DTYPE)],
        compiler_params=_cparams("parallel", "parallel"), name="qk_prep",
    )(proj3, proj3, proj3, q_gain_row, k_gain_row, gq, gk, cos2, sin2)


def _flash_body(q_ref, kt_ref, v_ref, o_ref, *, seq, tq, tk, group):
    hd = HEAD_DIM
    rows = group * tq
    q = q_ref[...].reshape(rows, hd)
    m = jnp.full((rows, 1), NEG_INF, F32)
    acc = jnp.zeros((rows, v_ref.shape[1]), F32)
    for kc in range(seq // tk):
        s = jnp.dot(q, kt_ref[:, kc * tk:(kc + 1) * tk], preferred_element_type=F32)
        m_new = jnp.maximum(m, jnp.max(s, axis=-1, keepdims=True))
        p = jnp.exp2(s - m_new)
        acc = jnp.exp2(m - m_new) * acc + jnp.dot(p.astype(v_ref.dtype), v_ref[kc * tk:(kc + 1) * tk, :],
                                                  preferred_element_type=F32)
        m = m_new
    o = acc[:, :hd] / acc[:, hd:hd + 1]
    o_ref[...] = jnp.concatenate([o[g * tq:(g + 1) * tq] for g in range(group)], axis=1)


def _flash_attention(q4, kt4, v4, tq, tk):
    b, n_q_heads, seq, hd = q4.shape
    n_kv_heads = kt4.shape[1]
    group = n_q_heads // n_kv_heads
    return pl.pallas_call(
        functools.partial(_flash_body, seq=seq, tq=tq, tk=tk, group=group), grid=(b, n_kv_heads, seq // tq),
        in_specs=[pl.BlockSpec((None, group, tq, hd), lambda bi, h, i: (bi, h, i, 0)),
                  pl.BlockSpec((None, None, hd, seq), lambda bi, h, i: (bi, h, 0, 0)),
                  pl.BlockSpec((None, None, seq, v4.shape[3]), lambda bi, h, i: (bi, h, 0, 0))],
        out_specs=pl.BlockSpec((None, tq, group * hd), lambda bi, h, i: (bi, i, h)),
        out_shape=jax.ShapeDtypeStruct((b, seq, n_q_heads * hd), F32),
        compiler_params=_cparams("parallel", "parallel", "parallel"), name="flash_gqa",
    )(q4, kt4, v4)


def _kvproj_body(m_ref, w_ref, o_ref):
    o_ref[...] = _mm(m_ref[...], w_ref[...]).astype(o_ref.dtype)


def _kvproj(mem2d, wkv, tm):
    t, d = mem2d.shape
    n = wkv.shape[1]
    return pl.pallas_call(
        _kvproj_body, grid=(t // tm,),
        in_specs=[pl.BlockSpec((tm, d), lambda i: (i, 0)), pl.BlockSpec((d, n), lambda i: (0, 0))],
        out_specs=pl.BlockSpec((tm, n), lambda i: (i, 0)), out_shape=jax.ShapeDtypeStruct((t, n), MXU_DTYPE),
        compiler_params=_cparams("parallel"), name="mem_kv_proj",
    )(mem2d, wkv)


def _crossattn_body(x_ref, yp_ref, ym_ref, ya_ref, wp_ref, wm_ref, wa_ref, kv_ref, gm_ref, wq_ref, wo_ref, gf_ref,
                    rwt_ref, xo_ref, hf_ref, lg_ref, *, ts):
    x = (x_ref[...] + _mm(yp_ref[...], wp_ref[...]) + _mm(ym_ref[...], wm_ref[...])
         + _mm(ya_ref[...], wa_ref[...]))
    q = _mm(_rms(x, gm_ref[...]), wq_ref[...])
    ca_width = q.shape[1]
    dh = ca_width // CA_HEADS
    kv = kv_ref[...]
    heads = []
    for h in range(CA_HEADS):
        s = _mm_nt(q[:, h * dh:(h + 1) * dh], kv[:, h * dh:(h + 1) * dh]) * (dh ** -0.5)
        e = jnp.exp(s - jnp.max(s, axis=-1, keepdims=True))
        p = e / jnp.sum(e, axis=-1, keepdims=True)
        heads.append(_mm(p, kv[:, ca_width + h * dh:ca_width + (h + 1) * dh]))
    x2 = x + _mm(jnp.concatenate(heads, axis=1), wo_ref[...])
    xo_ref[...] = x2
    hf = _rms(x2, gf_ref[...])
    _to_token_tiles(hf_ref, hf, ts)
    lg_ref[...] = _mm_nt(rwt_ref[...], hf)


def _crossattn(x3, yp, ym, ya, w_out, kv3, g_mem, wq, wo, g_ffn, router_wt, ts):
    b, seq, d = x3.shape
    n_mem, kvw = kv3.shape[1], kv3.shape[2]
    ne = router_wt.shape[0]
    fixed = lambda bi, i: (0, 0)
    tile = lambda w: pl.BlockSpec((None, ts, w), lambda bi, i: (bi, i, 0))
    wp, wm, wa = yp.shape[2], ym.shape[2], ya.shape[2]
    return pl.pallas_call(
        functools.partial(_crossattn_body, ts=ts), grid=(b, seq // ts),
        in_specs=[tile(d), tile(wp), tile(wm), tile(wa),
                  pl.BlockSpec((wp, d), fixed), pl.BlockSpec((wm, d), fixed), pl.BlockSpec((wa, d), fixed),
                  pl.BlockSpec((None, n_mem, kvw), lambda bi, i: (bi, 0, 0)),
                  pl.BlockSpec((1, d), fixed), pl.BlockSpec(wq.shape, fixed), pl.BlockSpec(wo.shape, fixed),
                  pl.BlockSpec((1, d), fixed), pl.BlockSpec((ne, d), fixed)],
        out_specs=[pl.BlockSpec((None, ts, d), lambda bi, i: (bi, i, 0)),
                   pl.BlockSpec((None, ts * SUBLANES, LANES), lambda bi, i: (bi, i, 0)),
                   pl.BlockSpec((None, ne, ts), lambda bi, i: (bi, 0, i))],
        out_shape=[jax.ShapeDtypeStruct((b, seq, d), F32),
                   jax.ShapeDtypeStruct((b, seq * SUBLANES, LANES), F32),
                   jax.ShapeDtypeStruct((b, ne, seq), F32)],
        compiler_params=_cparams("parallel", "parallel"), name="cross_attn",
    )(x3, yp, ym, ya, w_out[:wp], w_out[wp:wp + wm], w_out[wp + wm:], kv3, g_mem, wq, wo, g_ffn, router_wt)


def _excl_prefix(flags, upper_excl, ones):
    rows, seq = flags.shape
    off = jnp.zeros((rows, LANES), F32)
    out = []
    for c in range(seq // LANES):
        xc = flags[:, c * LANES:(c + 1) * LANES].astype(MXU_DTYPE)
        out.append(jnp.dot(xc, upper_excl, preferred_element_type=F32) + off)
        off = off + jnp.dot(xc, ones, preferred_element_type=F32)
    return jnp.concatenate(out, axis=1)


def _router_body(lg_ref, idx_ref, gate_ref, *, seq, cap):
    lg = lg_ref[...]
    ne = lg.shape[0]
    ex = jnp.exp(lg - jnp.max(lg, axis=0, keepdims=True))
    aff = ex / jnp.sum(ex, axis=0, keepdims=True)
    bits = pltpu.bitcast(aff, I32)

    thr = jnp.zeros((ne, 1), I32)
    for bit in range(30, -1, -1):
        cand = thr | (1 << bit)
        cnt = jnp.sum((bits >= cand).astype(F32), axis=1, keepdims=True)
        thr = jnp.where(cnt >= cap, cand, thr)

    r_i = lax.broadcasted_iota(I32, (LANES, LANES), 0)
    c_i = lax.broadcasted_iota(I32, (LANES, LANES), 1)
    upper_excl = (r_i < c_i).astype(MXU_DTYPE)
    ones = jnp.ones((LANES, LANES), MXU_DTYPE)

    gt = bits > thr
    eq = bits == thr
    n_gt = jnp.sum(gt.astype(F32), axis=1, keepdims=True)
    tie_rank = _excl_prefix(eq.astype(F32), upper_excl, ones)
    sel = gt | (eq & (tie_rank < cap - n_gt))
    pos = _excl_prefix(sel.astype(F32), upper_excl, ones).astype(I32)

    n_iota = lax.broadcasted_iota(I32, (ne, seq), 1)
    seq_bits = (seq - 1).bit_length()
    flag_bit = 2 * seq_bits
    packed = jnp.where(sel, n_iota | ((n_iota - pos) << seq_bits) | (1 << flag_bit), 0)
    gbits = jnp.where(sel, bits, 0)
    for k in range(seq_bits):
        sh = 1 << k
        moved_p = pltpu.roll(packed, seq - sh, 1)
        moved_g = pltpu.roll(gbits, seq - sh, 1)
        take = ((moved_p >> flag_bit) & 1) * ((moved_p >> (seq_bits + k)) & 1) == 1
        stay = ((packed >> flag_bit) & 1) * (1 - ((packed >> (seq_bits + k)) & 1)) == 1
        packed = jnp.where(take, moved_p, jnp.where(stay, packed, 0))
        gbits = jnp.where(take, moved_g, jnp.where(stay, gbits, 0))
    idx_ref[...] = packed[:, :cap] & (seq - 1)
    gate_ref[...] = pltpu.bitcast(gbits[:, :cap], F32)


def _router(logits_t, cap):
    b, ne, seq = logits_t.shape
    return pl.pallas_call(
        functools.partial(_router_body, seq=seq, cap=cap), grid=(b,),
        in_specs=[pl.BlockSpec((None, ne, seq), lambda bi: (bi, 0, 0))],
        out_specs=[pl.BlockSpec((None, ne, cap), lambda bi: (bi, 0, 0)),
                   pl.BlockSpec((None, ne, cap), lambda bi: (bi, 0, 0))],
        out_shape=[jax.ShapeDtypeStruct((b, ne, cap), I32), jax.ShapeDtypeStruct((b, ne, cap), F32)],
        compiler_params=_cparams("parallel"), name="ec_router",
    )(logits_t)


def _moe_body(*refs, seq, cap, ne, n_fc, stride, b_off, cast, chained):
    idx_hbm, gate_hbm, h_hbm, wg_ref, wu_ref, wd_ref = refs[:6]
    prev_hbm = refs[6] if chained else None
    refs = refs[6 + (1 if chained else 0):]
    out_hbm = refs[0]
    w_outs = refs[1:4] if cast else ()
    idx_s, gate_s, xbuf, xbf, ybuf, yacc, acc, gsem, isem, osem, psem = refs[1 + len(w_outs):]

    def prev_copy():
        return pltpu.make_async_copy(prev_hbm, out_hbm.at[pl.ds(0, b_off)], psem)
    b = pl.program_id(0)
    e = pl.program_id(1)
    c = pl.program_id(2)
    last = pl.num_programs(0) * ne - 1
    step = b * ne + e
    nstep = jnp.minimum(step + 1, last)
    xslot, nxslot = lax.rem(step, 2), lax.rem(step + 1, 2)
    islot, nislot = lax.rem(step, N_IDX_SLOTS), lax.rem(step + 1, N_IDX_SLOTS)
    tile = SUBLANES
    per = cap // n_fc

    def idx_copies(s, sl):
        src = pl.ds(pl.multiple_of((b_off * ne + s) * stride, stride), stride)
        dst = pl.ds(pl.multiple_of(sl * stride, stride), stride)
        return (pltpu.make_async_copy(idx_hbm.at[src], idx_s.at[dst], isem.at[0, sl]),
                pltpu.make_async_copy(gate_hbm.at[src], gate_s.at[dst], isem.at[1, sl]))

    def row_copy(src_row, k, sl):
        return pltpu.make_async_copy(h_hbm.at[pl.ds(src_row, tile), :],
                                     xbuf.at[sl, pl.ds(k * tile, tile), :], gsem.at[sl])

    def start_rows(s, isl, sl, k0, n):
        base = (b_off + s // ne) * seq
        ibase = isl * stride + k0
        for k in range(n):
            row_copy(pl.multiple_of((base + idx_s[ibase + k]) * tile, tile), k0 + k, sl).start()

    def wait_gather(sl):
        pltpu.make_async_copy(h_hbm.at[pl.ds(0, cap * tile), :], xbuf.at[sl], gsem.at[sl]).wait()

    @pl.when(c == 0)
    def _():
        @pl.when(step == 0)
        def _():
            for cp in idx_copies(0, 0):
                cp.start()
            for cp in idx_copies(0, 0):
                cp.wait()
            for cp in idx_copies(jnp.minimum(1, last), 1):
                cp.start()
            if chained:
                prev_copy().start()

            def body(j, carry):
                start_rows(0, 0, 0, j * SUBLANES, SUBLANES)
                return carry
            lax.fori_loop(0, cap // SUBLANES, body, 0)

        for cp in idx_copies(0, nislot):
            cp.wait()
        for cp in idx_copies(jnp.minimum(step + 2, last), lax.rem(step + 2, N_IDX_SLOTS)):
            cp.start()
        wait_gather(xslot)
        xbf[...] = _from_token_tiles(xbuf.at[xslot], cap).astype(xbf.dtype)

        @pl.when(e == 0)
        def _():
            acc[...] = jnp.zeros(acc.shape, F32)

    start_rows(nstep, nislot, nxslot, pl.multiple_of(c * per, per), per)

    x = xbf[...]
    fc = wg_ref.shape[1]
    sub = min(fc, MOE_SUB_COLS)
    y = None
    for j in range(fc // sub):
        wg, wu, wd = wg_ref[:, j * sub:(j + 1) * sub], wu_ref[:, j * sub:(j + 1) * sub], wd_ref[j * sub:(j + 1) * sub, :]
        if cast:
            wg, wu, wd = wg.astype(MXU_DTYPE), wu.astype(MXU_DTYPE), wd.astype(MXU_DTYPE)
            w_outs[0][:, j * sub:(j + 1) * sub] = wg
            w_outs[1][:, j * sub:(j + 1) * sub] = wu
            w_outs[2][j * sub:(j + 1) * sub, :] = wd
        g = jnp.dot(x, wg, preferred_element_type=F32)
        u = jnp.dot(x, wu, preferred_element_type=F32)
        yj = _mm(g * jax.nn.sigmoid(g) * u, wd)
        y = yj if y is None else y + yj

    if n_fc == 1:
        _to_token_tiles(ybuf, y, cap)
    else:
        @pl.when(c == 0)
        def _():
            yacc[...] = y

        @pl.when(c > 0)
        def _():
            yacc[...] += y

    @pl.when(c == n_fc - 1)
    def _():
        if n_fc > 1:
            _to_token_tiles(ybuf, yacc[...], cap)

        ibase = islot * stride
        for k0 in range(0, cap, SCATTER_GROUP):
            rows = [pl.multiple_of(idx_s[ibase + k0 + j] * tile, tile) for j in range(SCATTER_GROUP)]
            new = [acc[pl.ds(rows[j], tile), :] + gate_s[ibase + k0 + j] * ybuf[(k0 + j) * tile:(k0 + j + 1) * tile, :]
                   for j in range(SCATTER_GROUP)]
            for j in range(SCATTER_GROUP):
                acc[pl.ds(rows[j], tile), :] = new[j]

        @pl.when(e == ne - 1)
        def _():
            cp = pltpu.make_async_copy(acc, out_hbm.at[b_off + b], osem)
            cp.start()
            cp.wait()

        @pl.when(step == last)
        def _():
            wait_gather(nxslot)
            for cp in idx_copies(0, lax.rem(step + 2, N_IDX_SLOTS)):
                cp.wait()
            if chained:
                prev_copy().wait()


def _moe(idx, gate, h_tiles, w_gate, w_up, w_down, up_block_off, seq, fc, b_off, n_b, prev_out, cast):
    _, ne, cap = idx.shape
    d, f = w_down.shape[2], w_down.shape[1]
    n_fc = f // fc
    chained = prev_out is not None
    stride = -(-cap // SMEM_1D_TILE) * SMEM_1D_TILE
    flat = lambda a: jnp.pad(a, ((0, 0), (0, 0), (0, stride - cap))).reshape(-1)
    any_spec = pl.BlockSpec(memory_space=pl.ANY)
    assert chained == (b_off > 0)
    out_shape = [jax.ShapeDtypeStruct((b_off + n_b, seq * SUBLANES, LANES), F32)]
    out_specs = [any_spec]
    if cast:
        out_shape += [jax.ShapeDtypeStruct((ne, d, f), MXU_DTYPE), jax.ShapeDtypeStruct((ne, d, f), MXU_DTYPE),
                      jax.ShapeDtypeStruct((ne, f, d), MXU_DTYPE)]
        out_specs += [pl.BlockSpec((None, d, fc), lambda b, e, c: (e, 0, c)),
                      pl.BlockSpec((None, d, fc), lambda b, e, c: (e, 0, c)),
                      pl.BlockSpec((None, fc, d), lambda b, e, c: (e, c, 0))]
    return pl.pallas_call(
        functools.partial(_moe_body, seq=seq, cap=cap, ne=ne, n_fc=n_fc, stride=stride, b_off=b_off, cast=cast,
                          chained=chained),
        grid=(n_b, ne, n_fc),
        in_specs=[any_spec, any_spec, any_spec,
                  pl.BlockSpec((None, d, fc), lambda b, e, c: (e, 0, c)),
                  pl.BlockSpec((None, d, fc), lambda b, e, c: (e, 0, up_block_off + c)),
                  pl.BlockSpec((None, fc, d), lambda b, e, c: (e, c, 0))] + ([any_spec] if chained else []),
        out_specs=out_specs, out_shape=out_shape,
        scratch_shapes=[pltpu.SMEM((N_IDX_SLOTS * stride,), I32), pltpu.SMEM((N_IDX_SLOTS * stride,), F32),
                        pltpu.VMEM((2, cap * SUBLANES, LANES), F32), pltpu.VMEM((cap, d), MXU_DTYPE),
                        pltpu.VMEM((cap * SUBLANES, LANES), F32), pltpu.VMEM((cap, d), F32),
                        pltpu.VMEM((seq * SUBLANES, LANES), F32),
                        pltpu.SemaphoreType.DMA((2,)), pltpu.SemaphoreType.DMA((2, N_IDX_SLOTS)),
                        pltpu.SemaphoreType.DMA, pltpu.SemaphoreType.DMA],
        compiler_params=_cparams("arbitrary", "arbitrary", "arbitrary"),
        name="ec_moe_cast" if cast else "ec_moe",
    )(flat(idx), flat(gate), h_tiles, w_gate, w_up, w_down, *([prev_out] if chained else []))


def _final_body(x_ref, moe_ref, g_ref, o_ref, *, tm):
    o_ref[...] = _rms(x_ref[...] + _from_token_tiles(moe_ref, tm), g_ref[...])


def _final_norm(x2d, moe_tiles, g, tm):
    t, d = x2d.shape
    row = lambda i: (i, 0)
    return pl.pallas_call(
        functools.partial(_final_body, tm=tm), grid=(t // tm,),
        in_specs=[pl.BlockSpec((tm, d), row), pl.BlockSpec((tm * SUBLANES, LANES), row),
                  pl.BlockSpec((1, d), lambda i: (0, 0))],
        out_specs=pl.BlockSpec((tm, d), row), out_shape=jax.ShapeDtypeStruct((t, d), F32),
        compiler_params=_cparams("parallel"), name="final_norm",
    )(x2d, moe_tiles, g)


def _rope_tables(seq):
    rows = seq // GRID_W
    row_ids = jnp.repeat(jnp.arange(rows), GRID_W).astype(F32)
    col_ids = jnp.tile(jnp.arange(GRID_W), rows).astype(F32)
    n_freq = HEAD_DIM // 4
    inv_freq = ROPE_THETA ** (-jnp.arange(n_freq, dtype=F32) / n_freq)
    ang = jnp.concatenate([row_ids[:, None] * inv_freq, col_ids[:, None] * inv_freq], axis=-1)
    cos, sin = jnp.cos(ang), jnp.sin(ang)
    cos_i = jnp.repeat(cos, 2, axis=1)
    sin_i = jnp.stack([-sin, sin], axis=-1).reshape(seq, HEAD_DIM)
    return jnp.tile(cos_i, (1, LANES // HEAD_DIM)), jnp.tile(sin_i, (1, LANES // HEAD_DIM))


def _group_matrix(width):
    gid = jnp.arange(width) // HEAD_DIM
    return (gid[:, None] == gid[None, :]).astype(MXU_DTYPE)


def _block_diag(w):
    g, c, _ = w.shape
    eye = jnp.eye(g, dtype=w.dtype)
    return (eye[:, None, :, None] * w[:, :, None, :]).reshape(g * c, g * c)


def _pick_tile(n, pref):
    t = min(n, pref)
    while n % t:
        t //= 2
    return t


def kernel(x, mem, norm_mix_g, w_in, pool_w, pool_scale, mlstm_conv_w, mlstm_gate_b, mlstm_norm_g, q_norm_g,
           k_norm_g, w_out, norm_mem_g, ca_wq, ca_wkv, ca_wo, norm_ffn_g, router_w, expert_w_gu, expert_w_down,
           final_norm_g):
    batch, seq, d = x.shape
    depth = w_in.shape[0]
    n_mem = mem.shape[1]
    pool_width = pool_scale.shape[1]
    m_width = mlstm_norm_g.shape[1]
    m_heads = m_width // HEAD_DIM
    n_gate = 2 * m_heads
    kv_width = (w_in.shape[2] - pool_width - 4 * m_width - 2 * n_gate) // 6
    q_width = 4 * kv_width
    n_q_heads, n_kv_heads = q_width // HEAD_DIM, kv_width // HEAD_DIM
    ne = router_w.shape[2]
    cap = EC_CAPACITY_FACTOR * seq // ne
    f_exp = expert_w_down.shape[2]
    t = batch * seq

    o_pool = 0
    o_mq = o_pool + pool_width
    o_mi = o_mq + 4 * m_width
    o_aq = o_mi + 2 * n_gate
    o_ak = o_aq + q_width
    c_pool = q_width // pool_width
    c_mq = (q_width + pool_width) // LANES
    m_cols = tuple(c_mq + j * (m_width // LANES) for j in range(4))
    c_ak = (q_width + pool_width + 4 * m_width) // kv_width
    cos2, sin2 = _rope_tables(seq)
    gq, gk = _group_matrix(q_width), _group_matrix(kv_width)
    n_pairs = m_heads // 2
    gate_cols = [[kind * n_gate + dr * m_heads + 2 * p + hh for kind in range(2) for dr in range(2) for hh in range(2)]
                 for p in range(n_pairs)]

    tm = _pick_tile(t, 512)
    ts = _pick_tile(seq, 512)
    x2d = x.reshape(t, d)
    mem2d = mem.reshape(batch * n_mem, d)
    moe_tiles = None
    for layer in range(depth):
        wl = w_in[layer]
        w_main = jnp.concatenate([wl[:, o_aq:o_ak], wl[:, o_pool:o_mi], wl[:, o_ak:]], axis=1).astype(MXU_DTYPE)
        w_gate = jnp.concatenate(
            [jnp.pad(wl[:, o_mi:o_aq][:, jnp.array(cols)], ((0, 0), (0, LANES - len(cols)))) for cols in gate_cols],
            axis=1).astype(MXU_DTYPE)
        gb_row = jnp.concatenate(
            [jnp.pad(mlstm_gate_b[layer][jnp.array(cols)], (0, LANES - len(cols))) for cols in gate_cols])[None]
        x2d, proj, gates = _inproj(x2d, moe_tiles, norm_mix_g[layer][None], w_main, w_gate, tm)
        proj3 = proj.reshape(batch, seq, proj.shape[1])
        gates3 = gates.reshape(batch, seq, gates.shape[1])

        y_pool = _pool_mixer(proj3, _block_diag(pool_w[layer]).astype(MXU_DTYPE), pool_scale[layer][None], c_pool,
                             _pick_tile(seq, 256))
        y_mlstm = _mlstm(proj3, gates3, mlstm_conv_w[layer], gb_row, mlstm_norm_g[layer][None], m_heads, m_cols)
        q4, kt4, v4 = _qkprep(proj3, jnp.tile(q_norm_g[layer], n_q_heads)[None],
                              jnp.tile(k_norm_g[layer], n_kv_heads)[None], gq, gk, cos2, sin2,
                              n_q_heads, n_kv_heads, (0, c_ak, c_ak + 1), ts)
        y_attn = _flash_attention(q4, kt4, v4, _pick_tile(seq, 256), _pick_tile(seq, 1024))

        kv = _kvproj(mem2d, ca_wkv[layer].astype(MXU_DTYPE), _pick_tile(batch * n_mem, 512))
        x3, h_tiles, logits_t = _crossattn(
            x2d.reshape(batch, seq, d), y_pool, y_mlstm, y_attn, w_out[layer].astype(MXU_DTYPE),
            kv.reshape(batch, n_mem, kv.shape[1]), norm_mem_g[layer][None],
            ca_wq[layer].astype(MXU_DTYPE), ca_wo[layer].astype(MXU_DTYPE), norm_ffn_g[layer][None],
            router_w[layer].T.astype(MXU_DTYPE), ts)
        x2d = x3.reshape(t, d)

        idx, gate = _router(logits_t, cap)
        h_flat = h_tiles.reshape(t * SUBLANES, LANES)
        fc_cast = _pick_tile(f_exp, 512)
        w_gu_l, w_down_l = expert_w_gu[layer], expert_w_down[layer]
        moe_tiles, wg_c, wu_c, wd_c = _moe(idx, gate, h_flat, w_gu_l, w_gu_l, w_down_l, f_exp // fc_cast, seq,
                                           fc_cast, 0, 1, None, True)
        if batch > 1:
            moe_tiles, = _moe(idx, gate, h_flat, wg_c, wu_c, wd_c, 0, seq, f_exp, 1, batch - 1, moe_tiles, False)
        moe_tiles = moe_tiles.reshape(t * SUBLANES, LANES)
    out = _final_norm(x2d, moe_tiles, final_norm_g[None], tm)
    return out.reshape(batch, seq, d)
```

```python
import functools
import math

import jax
import jax.numpy as jnp
from jax import lax
from jax.experimental import pallas as pl
from jax.experimental.pallas import tpu as pltpu

F32 = jnp.float32
I32 = jnp.int32
MXU_DTYPE = jnp.bfloat16

LANES = 128
SUBLANES = 8
VMEM_LIMIT_BYTES = 56 * 1024 * 1024

EPS = 1e-6
GRID_W = 64
HEAD_DIM = 64
POOL_WINDOWS = (2, 4, 8, 16)
POOL_HALO = 8
MLSTM_CHUNK = 128
MLSTM_CONV = 5
CONV_HALO = 8
ROPE_THETA = 10000.0
CA_HEADS = 4
N_EXPERTS = 16
EC_CAPACITY_FACTOR = 2
SMEM_1D_TILE = 1024
N_IDX_SLOTS = 3
MOE_SUB_COLS = 512
SCATTER_GROUP = 16
NEG_INF = float("-inf")
LOG2_E = math.log2(math.e)


def _cparams(*sem):
    return pltpu.CompilerParams(dimension_semantics=sem, vmem_limit_bytes=VMEM_LIMIT_BYTES)


def _mm(a, b):
    return jnp.dot(a.astype(MXU_DTYPE), b.astype(MXU_DTYPE), preferred_element_type=F32)


def _mm_nt(a, b):
    return lax.dot_general(a.astype(MXU_DTYPE), b.astype(MXU_DTYPE), (((1,), (1,)), ((), ())),
                           preferred_element_type=F32)


def _split2(x):
    hi = x.astype(MXU_DTYPE)
    return hi, (x - hi.astype(F32)).astype(MXU_DTYPE)


def _split3(x):
    hi = x.astype(MXU_DTYPE)
    r1 = x - hi.astype(F32)
    mid = r1.astype(MXU_DTYPE)
    lo = (r1 - mid.astype(F32)).astype(MXU_DTYPE)
    return hi, mid, lo


def _rms(x, g):
    return x * lax.rsqrt(jnp.mean(x * x, axis=-1, keepdims=True) + EPS) * g


def _from_token_tiles(ref, rows):
    return jnp.concatenate([ref[pl.ds(s, rows, stride=SUBLANES), :] for s in range(SUBLANES)], axis=1)


def _to_token_tiles(ref, val, rows):
    for s in range(SUBLANES):
        ref[pl.ds(s, rows, stride=SUBLANES), :] = val[:, s * LANES:(s + 1) * LANES]


def _inproj_body(x_ref, g_ref, w_ref, wg_ref, o_ref, og_ref):
    hb = _rms(x_ref[...], g_ref[...]).astype(MXU_DTYPE)
    o_ref[...] = jnp.dot(hb, w_ref[...], preferred_element_type=F32)
    og_ref[...] = jnp.dot(hb, wg_ref[...], preferred_element_type=F32)


def _inproj_merge_body(x_ref, moe_ref, g_ref, w_ref, wg_ref, xo_ref, o_ref, og_ref, *, tm):
    x = x_ref[...] + _from_token_tiles(moe_ref, tm)
    xo_ref[...] = x
    hb = _rms(x, g_ref[...]).astype(MXU_DTYPE)
    o_ref[...] = jnp.dot(hb, w_ref[...], preferred_element_type=F32)
    og_ref[...] = jnp.dot(hb, wg_ref[...], preferred_element_type=F32)


def _inproj(x2d, moe_tiles, g, w_main, w_gate, tm):
    t, d = x2d.shape
    n = w_main.shape[1]
    grid = (t // tm,)
    row = lambda i: (i, 0)
    fixed = lambda i: (0, 0)
    gw = w_gate.shape[1]
    w_specs = [pl.BlockSpec((1, d), fixed), pl.BlockSpec((d, n), fixed), pl.BlockSpec((d, gw), fixed)]
    o_specs = [pl.BlockSpec((tm, n), row), pl.BlockSpec((tm, gw), row)]
    o_shapes = [jax.ShapeDtypeStruct((t, n), F32), jax.ShapeDtypeStruct((t, gw), F32)]
    if moe_tiles is None:
        proj, gates = pl.pallas_call(
            _inproj_body, grid=grid, in_specs=[pl.BlockSpec((tm, d), row)] + w_specs, out_specs=o_specs,
            out_shape=o_shapes, compiler_params=_cparams("parallel"), name="inproj",
        )(x2d, g, w_main, w_gate)
        return x2d, proj, gates
    xo, proj, gates = pl.pallas_call(
        functools.partial(_inproj_merge_body, tm=tm), grid=grid,
        in_specs=[pl.BlockSpec((tm, d), row), pl.BlockSpec((tm * SUBLANES, LANES), row)] + w_specs,
        out_specs=[pl.BlockSpec((tm, d), row)] + o_specs,
        out_shape=[jax.ShapeDtypeStruct((t, d), F32)] + o_shapes,
        compiler_params=_cparams("parallel"), name="inproj_merge",
    )(x2d, moe_tiles, g, w_main, w_gate)
    return xo, proj, gates


def _pool_body(u_ref, w_ref, sc_ref, o_ref, upad_ref, *, seq, tc):
    i = pl.program_id(1)

    @pl.when(i == 0)
    def _():
        zeros = jnp.zeros((POOL_HALO, u_ref.shape[1]), F32)
        upad_ref[0:POOL_HALO, :] = zeros
        upad_ref[POOL_HALO + seq:POOL_HALO + seq + POOL_HALO, :] = zeros
        upad_ref[POOL_HALO:POOL_HALO + seq, :] = u_ref[...]

    t0 = pl.multiple_of(i * tc, tc)
    win_rows = tc + 2 * POOL_HALO
    t = t0 + lax.broadcasted_iota(I32, (tc, LANES), 0)
    lane = lax.broadcasted_iota(I32, (tc, LANES), 1)
    upper = lane >= (LANES // 2)
    halves = []
    for hf in range(2):
        w_lo, w_hi = POOL_WINDOWS[2 * hf], POOL_WINDOWS[2 * hf + 1]
        win = upad_ref[pl.ds(t0, win_rows), hf * LANES:(hf + 1) * LANES]
        acc = jnp.zeros((tc, LANES), F32)
        for j in range(-(w_hi // 2), w_hi // 2):
            piece = pltpu.roll(win, (-j) % win_rows, 0)[POOL_HALO:POOL_HALO + tc]
            if -(w_lo // 2) <= j < w_lo // 2:
                acc = acc + piece
            else:
                acc = acc + jnp.where(upper, piece, 0.0)
        half_w = jnp.where(upper, w_hi // 2, w_lo // 2)
        cnt = jnp.minimum(t + half_w, seq) - jnp.maximum(t - half_w, 0)
        halves.append(acc / cnt.astype(F32) - win[POOL_HALO:POOL_HALO + tc])
    d = jnp.concatenate(halves, axis=1)
    o_ref[...] = _mm(d, w_ref[...]) * sc_ref[...]


def _pool_mixer(proj3, w_bd, scale, col_block, tc):
    b, seq, _ = proj3.shape
    pw = w_bd.shape[0]
    return pl.pallas_call(
        functools.partial(_pool_body, seq=seq, tc=tc), grid=(b, seq // tc),
        in_specs=[pl.BlockSpec((None, seq, pw), lambda bi, i: (bi, 0, col_block)),
                  pl.BlockSpec((pw, pw), lambda bi, i: (0, 0)),
                  pl.BlockSpec((1, pw), lambda bi, i: (0, 0))],
        out_specs=pl.BlockSpec((None, tc, pw), lambda bi, i: (bi, i, 0)),
        out_shape=jax.ShapeDtypeStruct((b, seq, pw), F32),
        scratch_shapes=[pltpu.VMEM((seq + 2 * POOL_HALO, pw), F32)],
        compiler_params=_cparams("parallel", "arbitrary"), name="pool_mixer",
    )(proj3, w_bd, scale)


def _mlstm_body(q_ref, k_ref, v_ref, o_ref, gt_ref, cwq_ref, cwk_ref, gb_ref, ng_ref, y_ref,
                qpad_ref, kpad_ref, qc_ref, kc_ref, kt_ref, gtt_ref, hf_ref, hb_ref, cn_ref, m_ref,
                *, seq, n_heads):
    L = MLSTM_CHUNK
    nc = seq // L
    hd = HEAD_DIM
    gbias = gb_ref[...]

    zeros = jnp.zeros((CONV_HALO, LANES), F32)
    for pad_ref, src_ref in ((qpad_ref, q_ref), (kpad_ref, k_ref)):
        pad_ref[0:CONV_HALO, :] = zeros
        pad_ref[CONV_HALO + seq:CONV_HALO + seq + CONV_HALO, :] = zeros
        pad_ref[CONV_HALO:CONV_HALO + seq, :] = src_ref[...]
    win_rows = L + 2 * CONV_HALO

    def conv_chunk(ci, carry):
        t0 = pl.multiple_of(ci * L, L)
        for pad_ref, cw_ref, dst_ref, post in ((qpad_ref, cwq_ref, qc_ref, 1.0),
                                               (kpad_ref, cwk_ref, kc_ref, 1.0 / math.sqrt(hd))):
            win = pad_ref[pl.ds(t0, win_rows), :]
            acc = jnp.zeros((L, LANES), F32)
            for j in range(MLSTM_CONV):
                off = j - MLSTM_CONV // 2
                piece = pltpu.roll(win, (-off) % win_rows, 0)[CONV_HALO:CONV_HALO + L]
                acc = acc + piece * cw_ref[j:j + 1, :]
            act = acc * jax.nn.sigmoid(acc)
            act = act * post if post != 1.0 else act
            dst_ref[pl.ds(t0, L), :] = act
            if dst_ref is kc_ref:
                kt_ref[:, pl.ds(t0, L)] = act.T
        gtt_ref[:, pl.ds(t0, L)] = (gt_ref[pl.ds(t0, L), :] + gbias).T
        return carry

    lax.fori_loop(0, nc, conv_chunk, 0)

    cn_ref[...] = jnp.zeros(cn_ref.shape, F32)
    m_ref[...] = jnp.zeros(m_ref.shape, F32)

    row = lax.broadcasted_iota(I32, (L, L), 0)
    col = lax.broadcasted_iota(I32, (L, L), 1)
    tri = ((col <= row).astype(MXU_DTYPE), (col >= row).astype(MXU_DTYPE))
    tri_t = (tri[1], tri[0])
    keep = (col <= row, col >= row)
    gate_lane = lax.broadcasted_iota(I32, (L, LANES), 1)
    gate_sub = lax.broadcasted_iota(I32, (LANES, L), 0)
    is_i, is_f = gate_lane < 4, (gate_lane >= 4) & (gate_lane < 8)
    is_i_t, is_f_t = gate_sub < 4, (gate_sub >= 4) & (gate_sub < 8)
    head_lanes = (gate_lane < hd, gate_lane >= hd)
    sel_r = lax.broadcasted_iota(I32, (LANES, 2 * LANES), 0)
    sel_c = lax.broadcasted_iota(I32, (LANES, 2 * LANES), 1)
    sel = [[(sel_r == jnp.where(sel_c < LANES, 4 + d * 2 + hh, d * 2 + hh)).astype(MXU_DTYPE) for hh in range(2)]
           for d in range(2)]
    ones_ext = jnp.ones((L, LANES), MXU_DTYPE)

    def cumulative(lf, left, d):
        hi, lo = _split2(lf)
        mat = tri[d] if left else tri_t[d]
        mm = (lambda x: jnp.dot(mat, x, preferred_element_type=F32)) if left else (
            lambda x: jnp.dot(x, mat, preferred_element_type=F32))
        return mm(hi) + mm(lo)

    chains = [(d, hh) for d in range(2) for hh in range(2)]
    f32dot = functools.partial(jnp.dot, preferred_element_type=F32)

    def chunk_step(it, carry):
        t0s = [pl.multiple_of((it if d == 0 else nc - 1 - it) * L, L) for d in range(2)]
        gpre = [gt_ref[pl.ds(t0, L), :] + gbias for t0 in t0s]
        gpre_t = [gtt_ref[:, pl.ds(t0, L)] for t0 in t0s]
        gm = [jnp.where(is_i, gpre[d], cumulative(jnp.where(is_f, jax.nn.log_sigmoid(gpre[d]), 0.0), True, d))
              for d in range(2)]
        gmt = [jnp.where(is_i_t, gpre_t[d], cumulative(jnp.where(is_f_t, jax.nn.log_sigmoid(gpre_t[d]), 0.0), False, d))
               for d in range(2)]
        gm2 = [_split2(g) for g in gm]
        qc = [qc_ref[pl.ds(t0, L), :] for t0 in t0s]
        kc = [kc_ref[pl.ds(t0, L), :].astype(MXU_DTYPE) for t0 in t0s]
        kt = [kt_ref[:, pl.ds(t0, L)].astype(MXU_DTYPE) for t0 in t0s]
        vc = [v_ref[pl.ds(t0, L), :] for t0 in t0s]
        v_ext = [jnp.concatenate([v.astype(MXU_DTYPE), ones_ext], axis=1) for v in vc]

        rep = [f32dot(gm2[d][0], sel[d][hh]) + f32dot(gm2[d][1], sel[d][hh]) for d, hh in chains]
        b_rep = [r[:, :LANES] for r in rep]
        li_rep = [r[:, LANES:] for r in rep]
        li_row = [gmt[d][d * 2 + hh:d * 2 + hh + 1, :] for d, hh in chains]
        b_row = [gmt[d][4 + d * 2 + hh:5 + d * 2 + hh, :] for d, hh in chains]
        m_st = [m_ref[hh * 2 + d] for d, hh in chains]
        cn = [cn_ref[hh * 2 + d] for d, hh in chains]
        qm = [jnp.where(head_lanes[hh], qc[d], 0.0).astype(MXU_DTYPE) for d, hh in chains]
        s_qk = [lax.dot_general(qm[c], kc[d], (((1,), (1,)), ((), ())), preferred_element_type=F32)
                for c, (d, hh) in enumerate(chains)]
        qcn = [f32dot(qm[c], cn[c].astype(MXU_DTYPE)) for c in range(4)]

        dmat = [jnp.where(keep[d], b_rep[c] - b_row[c] + li_row[c], NEG_INF) for c, (d, hh) in enumerate(chains)]
        inter = [b_rep[c] + m_st[c] for c in range(4)]
        m_j = [jnp.maximum(inter[c], jnp.max(dmat[c], axis=-1, keepdims=True)) for c in range(4)]
        a_mat = [jnp.exp(dmat[c] - m_j[c]) * s_qk[c] for c in range(4)]
        a_int = [jnp.exp(inter[c] - m_j[c]) for c in range(4)]
        av = [f32dot(a_mat[c].astype(MXU_DTYPE), v_ext[d]) for c, (d, hh) in enumerate(chains)]
        num = [av[c][:, :LANES] + a_int[c] * qcn[c][:, :LANES] for c in range(4)]
        den = [av[c][:, LANES:] + a_int[c] * qcn[c][:, LANES:] for c in range(4)]
        h_out = [num[c] / jnp.maximum(jnp.abs(den[c]), jnp.exp(-m_j[c])) for c in range(4)]

        b_last = [b_rep[c][(L - 1 if d == 0 else 0):(L if d == 0 else 1), :] for c, (d, hh) in enumerate(chains)]
        g_row = [b_last[c] - b_row[c] + li_row[c] for c in range(4)]
        g_rep = [b_last[c] - b_rep[c] + li_rep[c] for c in range(4)]
        m_new = [jnp.maximum(b_last[c] + m_st[c], jnp.max(g_row[c], axis=-1, keepdims=True)) for c in range(4)]
        w_rep = [jnp.exp(g_rep[c] - m_new[c]) for c in range(4)]
        decay = [jnp.exp(b_last[c] + m_st[c] - m_new[c]) for c in range(4)]
        wv_ext = [jnp.concatenate([w_rep[c] * vc[d], w_rep[c]], axis=1).astype(MXU_DTYPE)
                  for c, (d, hh) in enumerate(chains)]
        upd = [f32dot(kt[d], wv_ext[c]) for c, (d, hh) in enumerate(chains)]
        for c, (d, hh) in enumerate(chains):
            cn_ref[hh * 2 + d] = jnp.concatenate([decay[c], decay[c]], axis=1) * cn[c] + upd[c]
            m_ref[hh * 2 + d] = m_new[c]
        hf_ref[pl.ds(t0s[0], L), :] = jnp.where(head_lanes[0], h_out[0], h_out[1])
        hb_ref[pl.ds(t0s[1], L), :] = jnp.where(head_lanes[0], h_out[2], h_out[3])
        return carry

    lax.fori_loop(0, nc, chunk_step, 0)

    ng = ng_ref[...]
    lane = lax.broadcasted_iota(I32, (L, LANES), 1)
    upper = lane >= hd

    def out_chunk(ci, carry):
        t0 = pl.multiple_of(ci * L, L)
        hs = hf_ref[pl.ds(t0, L), :] + hb_ref[pl.ds(t0, L), :]
        sq = hs * hs
        ms_lo = jnp.sum(jnp.where(upper, 0.0, sq), axis=-1, keepdims=True) / hd
        ms_hi = jnp.sum(jnp.where(upper, sq, 0.0), axis=-1, keepdims=True) / hd
        ms = jnp.where(upper, ms_hi, ms_lo)
        y = hs * lax.rsqrt(ms + EPS) * ng
        y_ref[pl.ds(t0, L), :] = y * jax.nn.sigmoid(o_ref[pl.ds(t0, L), :])
        return carry

    lax.fori_loop(0, nc, out_chunk, 0)


def _mlstm(proj3, gates3, conv_w, gate_bias_row, norm_g_row, n_heads, cols):
    b, seq, _ = proj3.shape
    qb, kb, vb, ob = cols
    n_pairs = n_heads // 2
    width = n_heads * HEAD_DIM
    blk = lambda base: pl.BlockSpec((None, seq, LANES), lambda bi, p, base=base: (bi, 0, base + p))
    k_conv_base = width // LANES
    return pl.pallas_call(
        functools.partial(_mlstm_body, seq=seq, n_heads=n_heads), grid=(b, n_pairs),
        in_specs=[blk(qb), blk(kb), blk(vb), blk(ob),
                  pl.BlockSpec((None, seq, LANES), lambda bi, p: (bi, 0, p)),
                  pl.BlockSpec((MLSTM_CONV, LANES), lambda bi, p: (0, p)),
                  pl.BlockSpec((MLSTM_CONV, LANES), lambda bi, p: (0, k_conv_base + p)),
                  pl.BlockSpec((1, LANES), lambda bi, p: (0, p)),
                  pl.BlockSpec((1, LANES), lambda bi, p: (0, p))],
        out_specs=pl.BlockSpec((None, seq, LANES), lambda bi, p: (bi, 0, p)),
        out_shape=jax.ShapeDtypeStruct((b, seq, width), F32),
        scratch_shapes=[pltpu.VMEM((seq + 2 * CONV_HALO, LANES), F32), pltpu.VMEM((seq + 2 * CONV_HALO, LANES), F32),
                        pltpu.VMEM((seq, LANES), F32), pltpu.VMEM((seq, LANES), F32),
                        pltpu.VMEM((LANES, seq), F32), pltpu.VMEM((LANES, seq), F32),
                        pltpu.VMEM((seq, LANES), F32), pltpu.VMEM((seq, LANES), F32),
                        pltpu.VMEM((4, LANES, 2 * LANES), F32),
                        pltpu.VMEM((4, 1, LANES), F32)],
        compiler_params=_cparams("parallel", "parallel"), name="mlstm",
    )(proj3, proj3, proj3, proj3, gates3, conv_w, conv_w, gate_bias_row, norm_g_row)


def _group_mean_sq(x, gmat):
    hi = (x * x).astype(MXU_DTYPE)
    lo = (x * x - hi.astype(F32)).astype(MXU_DTYPE)
    ssq = jnp.dot(hi, gmat, preferred_element_type=F32) + jnp.dot(lo, gmat, preferred_element_type=F32)
    return ssq * (1.0 / HEAD_DIM)


def _norm_rope(x, g_row, gmat, cos_t, sin_t):
    width = x.shape[1]
    xn = x * lax.rsqrt(_group_mean_sq(x, gmat) + EPS) * g_row
    lane = lax.broadcasted_iota(I32, x.shape, 1)
    partner = jnp.where(lane % 2 == 0, pltpu.roll(xn, width - 1, 1), pltpu.roll(xn, 1, 1))
    return xn * cos_t + partner * sin_t


def _qkprep_body(q_ref, k_ref, v_ref, qg_ref, kg_ref, gq_ref, gk_ref, cos_ref, sin_ref, qo_ref, kt_ref, vo_ref,
                 *, n_q_heads, n_kv_heads):
    hd = HEAD_DIM
    cos2, sin2 = cos_ref[...], sin_ref[...]
    reps = n_q_heads * hd // LANES
    q = _norm_rope(q_ref[...], qg_ref[...], gq_ref[...],
                   jnp.concatenate([cos2] * reps, axis=1), jnp.concatenate([sin2] * reps, axis=1))
    q = q * (hd ** -0.5 * LOG2_E)
    for h in range(n_q_heads):
        qo_ref[h] = q[:, h * hd:(h + 1) * hd].astype(qo_ref.dtype)
    k = _norm_rope(k_ref[...], kg_ref[...], gk_ref[...], cos2, sin2)
    kt = k.T
    v = v_ref[...]
    kv_width = v.shape[1]
    lane = lax.broadcasted_iota(I32, v.shape, 1)
    for h in range(n_kv_heads):
        kt_ref[h] = kt[h * hd:(h + 1) * hd, :].astype(kt_ref.dtype)
        vh = v if h == 0 else pltpu.roll(v, kv_width - h * hd, 1)
        vo_ref[h] = jnp.where(lane < hd, vh, jnp.where(lane == hd, 1.0, 0.0)).astype(vo_ref.dtype)


def _qkprep(proj3, q_gain_row, k_gain_row, gq, gk, cos2, sin2, n_q_heads, n_kv_heads, cols, ts):
    b, seq, _ = proj3.shape
    qcol, kcol, vcol = cols
    qw, kw = n_q_heads * HEAD_DIM, n_kv_heads * HEAD_DIM
    fixed = lambda bi, i: (0, 0)
    return pl.pallas_call(
        functools.partial(_qkprep_body, n_q_heads=n_q_heads, n_kv_heads=n_kv_heads), grid=(b, seq // ts),
        in_specs=[pl.BlockSpec((None, ts, qw), lambda bi, i: (bi, i, qcol)),
                  pl.BlockSpec((None, ts, kw), lambda bi, i: (bi, i, kcol)),
                  pl.BlockSpec((None, ts, kw), lambda bi, i: (bi, i, vcol)),
                  pl.BlockSpec((1, qw), fixed), pl.BlockSpec((1, kw), fixed),
                  pl.BlockSpec((qw, qw), fixed), pl.BlockSpec((kw, kw), fixed),
                  pl.BlockSpec((ts, kw), lambda bi, i: (i, 0)), pl.BlockSpec((ts, kw), lambda bi, i: (i, 0))],
        out_specs=[pl.BlockSpec((None, n_q_heads, ts, HEAD_DIM), lambda bi, i: (bi, 0, i, 0)),
                   pl.BlockSpec((None, n_kv_heads, HEAD_DIM, ts), lambda bi, i: (bi, 0, 0, i)),
                   pl.BlockSpec((None, n_kv_heads, ts, kw), lambda bi, i: (bi, 0, i, 0))],
        out_shape=[jax.ShapeDtypeStruct((b, n_q_heads, seq, HEAD_DIM), MXU_DTYPE),
                   jax.ShapeDtypeStruct((b, n_kv_heads, HEAD_DIM, seq), MXU_DTYPE),
                   jax.ShapeDtypeStruct((b, n_kv_heads, seq, kw), MXU_DTYPE)],
        compiler_params=_cparams("parallel", "parallel"), name="qk_prep",
    )(proj3, proj3, proj3, q_gain_row, k_gain_row, gq, gk, cos2, sin2)


def _flash_body(q_ref, kt_ref, v_ref, o_ref, *, seq, tq, tk, group):
    hd = HEAD_DIM
    rows = group * tq
    q = q_ref[...].reshape(rows, hd)
    m = jnp.full((rows, 1), NEG_INF, F32)
    acc = jnp.zeros((rows, v_ref.shape[1]), F32)
    for kc in range(seq // tk):
        s = jnp.dot(q, kt_ref[:, kc * tk:(kc + 1) * tk], preferred_element_type=F32)
        m_new = jnp.maximum(m, jnp.max(s, axis=-1, keepdims=True))
        p = jnp.exp2(s - m_new)
        acc = jnp.exp2(m - m_new) * acc + jnp.dot(p.astype(v_ref.dtype), v_ref[kc * tk:(kc + 1) * tk, :],
                                                  preferred_element_type=F32)
        m = m_new
    o = acc[:, :hd] / acc[:, hd:hd + 1]
    o_ref[...] = jnp.concatenate([o[g * tq:(g + 1) * tq] for g in range(group)], axis=1)


def _flash_attention(q4, kt4, v4, tq, tk):
    b, n_q_heads, seq, hd = q4.shape
    n_kv_heads = kt4.shape[1]
    group = n_q_heads // n_kv_heads
    return pl.pallas_call(
        functools.partial(_flash_body, seq=seq, tq=tq, tk=tk, group=group), grid=(b, n_kv_heads, seq // tq),
        in_specs=[pl.BlockSpec((None, group, tq, hd), lambda bi, h, i: (bi, h, i, 0)),
                  pl.BlockSpec((None, None, hd, seq), lambda bi, h, i: (bi, h, 0, 0)),
                  pl.BlockSpec((None, None, seq, v4.shape[3]), lambda bi, h, i: (bi, h, 0, 0))],
        out_specs=pl.BlockSpec((None, tq, group * hd), lambda bi, h, i: (bi, i, h)),
        out_shape=jax.ShapeDtypeStruct((b, seq, n_q_heads * hd), F32),
        compiler_params=_cparams("parallel", "parallel", "parallel"), name="flash_gqa",
    )(q4, kt4, v4)


def _kvproj_body(m_ref, w_ref, o_ref):
    o_ref[...] = _mm(m_ref[...], w_ref[...]).astype(o_ref.dtype)


def _kvproj(mem2d, wkv, tm):
    t, d = mem2d.shape
    n = wkv.shape[1]
    return pl.pallas_call(
        _kvproj_body, grid=(t // tm,),
        in_specs=[pl.BlockSpec((tm, d), lambda i: (i, 0)), pl.BlockSpec((d, n), lambda i: (0, 0))],
        out_specs=pl.BlockSpec((tm, n), lambda i: (i, 0)), out_shape=jax.ShapeDtypeStruct((t, n), MXU_DTYPE),
        compiler_params=_cparams("parallel"), name="mem_kv_proj",
    )(mem2d, wkv)


def _crossattn_body(x_ref, yp_ref, ym_ref, ya_ref, wp_ref, wm_ref, wa_ref, kv_ref, gm_ref, wq_ref, wo_ref, gf_ref,
                    rwt_ref, xo_ref, hf_ref, lg_ref, *, ts):
    x = (x_ref[...] + _mm(yp_ref[...], wp_ref[...]) + _mm(ym_ref[...], wm_ref[...])
         + _mm(ya_ref[...], wa_ref[...]))
    q = _mm(_rms(x, gm_ref[...]), wq_ref[...])
    ca_width = q.shape[1]
    dh = ca_width // CA_HEADS
    kv = kv_ref[...]
    heads = []
    for h in range(CA_HEADS):
        s = _mm_nt(q[:, h * dh:(h + 1) * dh], kv[:, h * dh:(h + 1) * dh]) * (dh ** -0.5)
        e = jnp.exp(s - jnp.max(s, axis=-1, keepdims=True))
        p = e / jnp.sum(e, axis=-1, keepdims=True)
        heads.append(_mm(p, kv[:, ca_width + h * dh:ca_width + (h + 1) * dh]))
    x2 = x + _mm(jnp.concatenate(heads, axis=1), wo_ref[...])
    xo_ref[...] = x2
    hf = _rms(x2, gf_ref[...])
    _to_token_tiles(hf_ref, hf, ts)
    lg_ref[...] = _mm_nt(rwt_ref[...], hf)


def _crossattn(x3, yp, ym, ya, w_out, kv3, g_mem, wq, wo, g_ffn, router_wt, ts):
    b, seq, d = x3.shape
    n_mem, kvw = kv3.shape[1], kv3.shape[2]
    ne = router_wt.shape[0]
    fixed = lambda bi, i: (0, 0)
    tile = lambda w: pl.BlockSpec((None, ts, w), lambda bi, i: (bi, i, 0))
    wp, wm, wa = yp.shape[2], ym.shape[2], ya.shape[2]
    return pl.pallas_call(
        functools.partial(_crossattn_body, ts=ts), grid=(b, seq // ts),
        in_specs=[tile(d), tile(wp), tile(wm), tile(wa),
                  pl.BlockSpec((wp, d), fixed), pl.BlockSpec((wm, d), fixed), pl.BlockSpec((wa, d), fixed),
                  pl.BlockSpec((None, n_mem, kvw), lambda bi, i: (bi, 0, 0)),
                  pl.BlockSpec((1, d), fixed), pl.BlockSpec(wq.shape, fixed), pl.BlockSpec(wo.shape, fixed),
                  pl.BlockSpec((1, d), fixed), pl.BlockSpec((ne, d), fixed)],
        out_specs=[pl.BlockSpec((None, ts, d), lambda bi, i: (bi, i, 0)),
                   pl.BlockSpec((None, ts * SUBLANES, LANES), lambda bi, i: (bi, i, 0)),
                   pl.BlockSpec((None, ne, ts), lambda bi, i: (bi, 0, i))],
        out_shape=[jax.ShapeDtypeStruct((b, seq, d), F32),
                   jax.ShapeDtypeStruct((b, seq * SUBLANES, LANES), F32),
                   jax.ShapeDtypeStruct((b, ne, seq), F32)],
        compiler_params=_cparams("parallel", "parallel"), name="cross_attn",
    )(x3, yp, ym, ya, w_out[:wp], w_out[wp:wp + wm], w_out[wp + wm:], kv3, g_mem, wq, wo, g_ffn, router_wt)


def _excl_prefix(flags, upper_excl, ones):
    rows, seq = flags.shape
    off = jnp.zeros((rows, LANES), F32)
    out = []
    for c in range(seq // LANES):
        xc = flags[:, c * LANES:(c + 1) * LANES].astype(MXU_DTYPE)
        out.append(jnp.dot(xc, upper_excl, preferred_element_type=F32) + off)
        off = off + jnp.dot(xc, ones, preferred_element_type=F32)
    return jnp.concatenate(out, axis=1)


def _router_body(lg_ref, idx_ref, gate_ref, *, seq, cap):
    lg = lg_ref[...]
    ne = lg.shape[0]
    ex = jnp.exp(lg - jnp.max(lg, axis=0, keepdims=True))
    aff = ex / jnp.sum(ex, axis=0, keepdims=True)
    bits = pltpu.bitcast(aff, I32)

    thr = jnp.zeros((ne, 1), I32)
    for bit in range(30, -1, -1):
        cand = thr | (1 << bit)
        cnt = jnp.sum((bits >= cand).astype(F32), axis=1, keepdims=True)
        thr = jnp.where(cnt >= cap, cand, thr)

    r_i = lax.broadcasted_iota(I32, (LANES, LANES), 0)
    c_i = lax.broadcasted_iota(I32, (LANES, LANES), 1)
    upper_excl = (r_i < c_i).astype(MXU_DTYPE)
    ones = jnp.ones((LANES, LANES), MXU_DTYPE)

    gt = bits > thr
    eq = bits == thr
    n_gt = jnp.sum(gt.astype(F32), axis=1, keepdims=True)
    tie_rank = _excl_prefix(eq.astype(F32), upper_excl, ones)
    sel = gt | (eq & (tie_rank < cap - n_gt))
    pos = _excl_prefix(sel.astype(F32), upper_excl, ones).astype(I32)

    n_iota = lax.broadcasted_iota(I32, (ne, seq), 1)
    seq_bits = (seq - 1).bit_length()
    flag_bit = 2 * seq_bits
    packed = jnp.where(sel, n_iota | ((n_iota - pos) << seq_bits) | (1 << flag_bit), 0)
    gbits = jnp.where(sel, bits, 0)
    for k in range(seq_bits):
        sh = 1 << k
        moved_p = pltpu.roll(packed, seq - sh, 1)
        moved_g = pltpu.roll(gbits, seq - sh, 1)
        take = ((moved_p >> flag_bit) & 1) * ((moved_p >> (seq_bits + k)) & 1) == 1
        stay = ((packed >> flag_bit) & 1) * (1 - ((packed >> (seq_bits + k)) & 1)) == 1
        packed = jnp.where(take, moved_p, jnp.where(stay, packed, 0))
        gbits = jnp.where(take, moved_g, jnp.where(stay, gbits, 0))
    idx_ref[...] = packed[:, :cap] & (seq - 1)
    gate_ref[...] = pltpu.bitcast(gbits[:, :cap], F32)


def _router(logits_t, cap):
    b, ne, seq = logits_t.shape
    return pl.pallas_call(
        functools.partial(_router_body, seq=seq, cap=cap), grid=(b,),
        in_specs=[pl.BlockSpec((None, ne, seq), lambda bi: (bi, 0, 0))],
        out_specs=[pl.BlockSpec((None, ne, cap), lambda bi: (bi, 0, 0)),
                   pl.BlockSpec((None, ne, cap), lambda bi: (bi, 0, 0))],
        out_shape=[jax.ShapeDtypeStruct((b, ne, cap), I32), jax.ShapeDtypeStruct((b, ne, cap), F32)],
        compiler_params=_cparams("parallel"), name="ec_router",
    )(logits_t)


def _moe_body(*refs, seq, cap, ne, n_fc, stride, b_off, cast, chained):
    idx_hbm, gate_hbm, h_hbm, wg_ref, wu_ref, wd_ref = refs[:6]
    prev_hbm = refs[6] if chained else None
    refs = refs[6 + (1 if chained else 0):]
    out_hbm = refs[0]
    w_outs = refs[1:4] if cast else ()
    idx_s, gate_s, xbuf, xbf, ybuf, yacc, acc, gsem, isem, osem, psem = refs[1 + len(w_outs):]

    def prev_copy():
        return pltpu.make_async_copy(prev_hbm, out_hbm.at[pl.ds(0, b_off)], psem)
    b = pl.program_id(0)
    e = pl.program_id(1)
    c = pl.program_id(2)
    last = pl.num_programs(0) * ne - 1
    step = b * ne + e
    nstep = jnp.minimum(step + 1, last)
    xslot, nxslot = lax.rem(step, 2), lax.rem(step + 1, 2)
    islot, nislot = lax.rem(step, N_IDX_SLOTS), lax.rem(step + 1, N_IDX_SLOTS)
    tile = SUBLANES
    per = cap // n_fc

    def idx_copies(s, sl):
        src = pl.ds(pl.multiple_of((b_off * ne + s) * stride, stride), stride)
        dst = pl.ds(pl.multiple_of(sl * stride, stride), stride)
        return (pltpu.make_async_copy(idx_hbm.at[src], idx_s.at[dst], isem.at[0, sl]),
                pltpu.make_async_copy(gate_hbm.at[src], gate_s.at[dst], isem.at[1, sl]))

    def row_copy(src_row, k, sl):
        return pltpu.make_async_copy(h_hbm.at[pl.ds(src_row, tile), :],
                                     xbuf.at[sl, pl.ds(k * tile, tile), :], gsem.at[sl])

    def start_rows(s, isl, sl, k0, n):
        base = (b_off + s // ne) * seq
        ibase = isl * stride + k0
        for k in range(n):
            row_copy(pl.multiple_of((base + idx_s[ibase + k]) * tile, tile), k0 + k, sl).start()

    def wait_gather(sl):
        pltpu.make_async_copy(h_hbm.at[pl.ds(0, cap * tile), :], xbuf.at[sl], gsem.at[sl]).wait()

    @pl.when(c == 0)
    def _():
        @pl.when(step == 0)
        def _():
            for cp in idx_copies(0, 0):
                cp.start()
            for cp in idx_copies(0, 0):
                cp.wait()
            for cp in idx_copies(jnp.minimum(1, last), 1):
                cp.start()
            if chained:
                prev_copy().start()

            def body(j, carry):
                start_rows(0, 0, 0, j * SUBLANES, SUBLANES)
                return carry
            lax.fori_loop(0, cap // SUBLANES, body, 0)

        for cp in idx_copies(0, nislot):
            cp.wait()
        for cp in idx_copies(jnp.minimum(step + 2, last), lax.rem(step + 2, N_IDX_SLOTS)):
            cp.start()
        wait_gather(xslot)
        xbf[...] = _from_token_tiles(xbuf.at[xslot], cap).astype(xbf.dtype)

        @pl.when(e == 0)
        def _():
            acc[...] = jnp.zeros(acc.shape, F32)

    start_rows(nstep, nislot, nxslot, pl.multiple_of(c * per, per), per)

    x = xbf[...]
    fc = wg_ref.shape[1]
    sub = min(fc, MOE_SUB_COLS)
    y = None
    for j in range(fc // sub):
        wg, wu, wd = wg_ref[:, j * sub:(j + 1) * sub], wu_ref[:, j * sub:(j + 1) * sub], wd_ref[j * sub:(j + 1) * sub, :]
        if cast:
            wg, wu, wd = wg.astype(MXU_DTYPE), wu.astype(MXU_DTYPE), wd.astype(MXU_DTYPE)
            w_outs[0][:, j * sub:(j + 1) * sub] = wg
            w_outs[1][:, j * sub:(j + 1) * sub] = wu
            w_outs[2][j * sub:(j + 1) * sub, :] = wd
        g = jnp.dot(x, wg, preferred_element_type=F32)
        u = jnp.dot(x, wu, preferred_element_type=F32)
        yj = _mm(g * jax.nn.sigmoid(g) * u, wd)
        y = yj if y is None else y + yj

    if n_fc == 1:
        _to_token_tiles(ybuf, y, cap)
    else:
        @pl.when(c == 0)
        def _():
            yacc[...] = y

        @pl.when(c > 0)
        def _():
            yacc[...] += y

    @pl.when(c == n_fc - 1)
    def _():
        if n_fc > 1:
            _to_token_tiles(ybuf, yacc[...], cap)

        ibase = islot * stride
        for k0 in range(0, cap, SCATTER_GROUP):
            rows = [pl.multiple_of(idx_s[ibase + k0 + j] * tile, tile) for j in range(SCATTER_GROUP)]
            new = [acc[pl.ds(rows[j], tile), :] + gate_s[ibase + k0 + j] * ybuf[(k0 + j) * tile:(k0 + j + 1) * tile, :]
                   for j in range(SCATTER_GROUP)]
            for j in range(SCATTER_GROUP):
                acc[pl.ds(rows[j], tile), :] = new[j]

        @pl.when(e == ne - 1)
        def _():
            cp = pltpu.make_async_copy(acc, out_hbm.at[b_off + b], osem)
            cp.start()
            cp.wait()

        @pl.when(step == last)
        def _():
            wait_gather(nxslot)
            for cp in idx_copies(0, lax.rem(step + 2, N_IDX_SLOTS)):
                cp.wait()
            if chained:
                prev_copy().wait()


def _moe(idx, gate, h_tiles, w_gate, w_up, w_down, up_block_off, seq, fc, b_off, n_b, prev_out, cast, layer=None):
    _, ne, cap = idx.shape
    d, f = w_down.shape[-1], w_down.shape[-2]
    n_fc = f // fc
    lead = () if layer is None else (layer,)
    w_spec = lambda shape, imap: pl.BlockSpec((None,) * (len(lead) + 1) + shape,
                                               lambda b, e, c: lead + imap(e, c))
    chained = prev_out is not None
    stride = -(-cap // SMEM_1D_TILE) * SMEM_1D_TILE
    flat = lambda a: jnp.pad(a, ((0, 0), (0, 0), (0, stride - cap))).reshape(-1)
    any_spec = pl.BlockSpec(memory_space=pl.ANY)
    assert chained == (b_off > 0)
    out_shape = [jax.ShapeDtypeStruct((b_off + n_b, seq * SUBLANES, LANES), F32)]
    out_specs = [any_spec]
    if cast:
        out_shape += [jax.ShapeDtypeStruct((ne, d, f), MXU_DTYPE), jax.ShapeDtypeStruct((ne, d, f), MXU_DTYPE),
                      jax.ShapeDtypeStruct((ne, f, d), MXU_DTYPE)]
        out_specs += [pl.BlockSpec((None, d, fc), lambda b, e, c: (e, 0, c)),
                      pl.BlockSpec((None, d, fc), lambda b, e, c: (e, 0, c)),
                      pl.BlockSpec((None, fc, d), lambda b, e, c: (e, c, 0))]
    return pl.pallas_call(
        functools.partial(_moe_body, seq=seq, cap=cap, ne=ne, n_fc=n_fc, stride=stride, b_off=b_off, cast=cast,
                          chained=chained),
        grid=(n_b, ne, n_fc),
        in_specs=[any_spec, any_spec, any_spec,
                  w_spec((d, fc), lambda e, c: (e, 0, c)),
                  w_spec((d, fc), lambda e, c: (e, 0, up_block_off + c)),
                  w_spec((fc, d), lambda e, c: (e, c, 0))] + ([any_spec] if chained else []),
        out_specs=out_specs, out_shape=out_shape,
        scratch_shapes=[pltpu.SMEM((N_IDX_SLOTS * stride,), I32), pltpu.SMEM((N_IDX_SLOTS * stride,), F32),
                        pltpu.VMEM((2, cap * SUBLANES, LANES), F32), pltpu.VMEM((cap, d), MXU_DTYPE),
                        pltpu.VMEM((cap * SUBLANES, LANES), F32), pltpu.VMEM((cap, d), F32),
                        pltpu.VMEM((seq * SUBLANES, LANES), F32),
                        pltpu.SemaphoreType.DMA((2,)), pltpu.SemaphoreType.DMA((2, N_IDX_SLOTS)),
                        pltpu.SemaphoreType.DMA, pltpu.SemaphoreType.DMA],
        compiler_params=_cparams("arbitrary", "arbitrary", "arbitrary"),
        name="ec_moe_cast" if cast else "ec_moe",
    )(flat(idx), flat(gate), h_tiles, w_gate, w_up, w_down, *([prev_out] if chained else []))


def _final_body(x_ref, moe_ref, g_ref, o_ref, *, tm):
    o_ref[...] = _rms(x_ref[...] + _from_token_tiles(moe_ref, tm), g_ref[...])


def _final_norm(x2d, moe_tiles, g, tm):
    t, d = x2d.shape
    row = lambda i: (i, 0)
    return pl.pallas_call(
        functools.partial(_final_body, tm=tm), grid=(t // tm,),
        in_specs=[pl.BlockSpec((tm, d), row), pl.BlockSpec((tm * SUBLANES, LANES), row),
                  pl.BlockSpec((1, d), lambda i: (0, 0))],
        out_specs=pl.BlockSpec((tm, d), row), out_shape=jax.ShapeDtypeStruct((t, d), F32),
        compiler_params=_cparams("parallel"), name="final_norm",
    )(x2d, moe_tiles, g)


def _rope_tables(seq):
    rows = seq // GRID_W
    row_ids = jnp.repeat(jnp.arange(rows), GRID_W).astype(F32)
    col_ids = jnp.tile(jnp.arange(GRID_W), rows).astype(F32)
    n_freq = HEAD_DIM // 4
    inv_freq = ROPE_THETA ** (-jnp.arange(n_freq, dtype=F32) / n_freq)
    ang = jnp.concatenate([row_ids[:, None] * inv_freq, col_ids[:, None] * inv_freq], axis=-1)
    cos, sin = jnp.cos(ang), jnp.sin(ang)
    cos_i = jnp.repeat(cos, 2, axis=1)
    sin_i = jnp.stack([-sin, sin], axis=-1).reshape(seq, HEAD_DIM)
    return jnp.tile(cos_i, (1, LANES // HEAD_DIM)), jnp.tile(sin_i, (1, LANES // HEAD_DIM))


def _group_matrix(width):
    gid = jnp.arange(width) // HEAD_DIM
    return (gid[:, None] == gid[None, :]).astype(MXU_DTYPE)


def _block_diag(w):
    g, c, _ = w.shape
    eye = jnp.eye(g, dtype=w.dtype)
    return (eye[:, None, :, None] * w[:, :, None, :]).reshape(g * c, g * c)


def _pick_tile(n, pref):
    t = min(n, pref)
    while n % t:
        t //= 2
    return t


def kernel(x, mem, norm_mix_g, w_in, pool_w, pool_scale, mlstm_conv_w, mlstm_gate_b, mlstm_norm_g, q_norm_g,
           k_norm_g, w_out, norm_mem_g, ca_wq, ca_wkv, ca_wo, norm_ffn_g, router_w, expert_w_gu, expert_w_down,
           final_norm_g):
    batch, seq, d = x.shape
    depth = w_in.shape[0]
    n_mem = mem.shape[1]
    pool_width = pool_scale.shape[1]
    m_width = mlstm_norm_g.shape[1]
    m_heads = m_width // HEAD_DIM
    n_gate = 2 * m_heads
    kv_width = (w_in.shape[2] - pool_width - 4 * m_width - 2 * n_gate) // 6
    q_width = 4 * kv_width
    n_q_heads, n_kv_heads = q_width // HEAD_DIM, kv_width // HEAD_DIM
    ne = router_w.shape[2]
    cap = EC_CAPACITY_FACTOR * seq // ne
    f_exp = expert_w_down.shape[2]
    t = batch * seq

    o_pool = 0
    o_mq = o_pool + pool_width
    o_mi = o_mq + 4 * m_width
    o_aq = o_mi + 2 * n_gate
    o_ak = o_aq + q_width
    c_pool = q_width // pool_width
    c_mq = (q_width + pool_width) // LANES
    m_cols = tuple(c_mq + j * (m_width // LANES) for j in range(4))
    c_ak = (q_width + pool_width + 4 * m_width) // kv_width
    cos2, sin2 = _rope_tables(seq)
    gq, gk = _group_matrix(q_width), _group_matrix(kv_width)
    n_pairs = m_heads // 2
    gate_cols = [[kind * n_gate + dr * m_heads + 2 * p + hh for kind in range(2) for dr in range(2) for hh in range(2)]
                 for p in range(n_pairs)]

    tm = _pick_tile(t, 512)
    ts = _pick_tile(seq, 512)
    x2d = x.reshape(t, d)
    mem2d = mem.reshape(batch * n_mem, d)
    moe_tiles = None
    for layer in range(depth):
        wl = w_in[layer]
        w_main = jnp.concatenate([wl[:, o_aq:o_ak], wl[:, o_pool:o_mi], wl[:, o_ak:]], axis=1).astype(MXU_DTYPE)
        w_gate = jnp.concatenate(
            [jnp.pad(wl[:, o_mi:o_aq][:, jnp.array(cols)], ((0, 0), (0, LANES - len(cols)))) for cols in gate_cols],
            axis=1).astype(MXU_DTYPE)
        gb_row = jnp.concatenate(
            [jnp.pad(mlstm_gate_b[layer][jnp.array(cols)], (0, LANES - len(cols))) for cols in gate_cols])[None]
        x2d, proj, gates = _inproj(x2d, moe_tiles, norm_mix_g[layer][None], w_main, w_gate, tm)
        proj3 = proj.reshape(batch, seq, proj.shape[1])
        gates3 = gates.reshape(batch, seq, gates.shape[1])

        y_pool = _pool_mixer(proj3, _block_diag(pool_w[layer]).astype(MXU_DTYPE), pool_scale[layer][None], c_pool,
                             _pick_tile(seq, 256))
        y_mlstm = _mlstm(proj3, gates3, mlstm_conv_w[layer], gb_row, mlstm_norm_g[layer][None], m_heads, m_cols)
        q4, kt4, v4 = _qkprep(proj3, jnp.tile(q_norm_g[layer], n_q_heads)[None],
                              jnp.tile(k_norm_g[layer], n_kv_heads)[None], gq, gk, cos2, sin2,
                              n_q_heads, n_kv_heads, (0, c_ak, c_ak + 1), ts)
        y_attn = _flash_attention(q4, kt4, v4, _pick_tile(seq, 256), _pick_tile(seq, 1024))

        kv = _kvproj(mem2d, ca_wkv[layer].astype(MXU_DTYPE), _pick_tile(batch * n_mem, 512))
        x3, h_tiles, logits_t = _crossattn(
            x2d.reshape(batch, seq, d), y_pool, y_mlstm, y_attn, w_out[layer].astype(MXU_DTYPE),
            kv.reshape(batch, n_mem, kv.shape[1]), norm_mem_g[layer][None],
            ca_wq[layer].astype(MXU_DTYPE), ca_wo[layer].astype(MXU_DTYPE), norm_ffn_g[layer][None],
            router_w[layer].T.astype(MXU_DTYPE), ts)
        x2d = x3.reshape(t, d)

        idx, gate = _router(logits_t, cap)
        h_flat = h_tiles.reshape(t * SUBLANES, LANES)
        fc_cast = _pick_tile(f_exp, 512)
        moe_tiles, wg_c, wu_c, wd_c = _moe(idx, gate, h_flat, expert_w_gu, expert_w_gu, expert_w_down,
                                           f_exp // fc_cast, seq, fc_cast, 0, 1, None, True, layer)
        if batch > 1:
            moe_tiles, = _moe(idx, gate, h_flat, wg_c, wu_c, wd_c, 0, seq, _pick_tile(f_exp, 1024), 1, batch - 1,
                              moe_tiles, False)
        moe_tiles = moe_tiles.reshape(t * SUBLANES, LANES)
    out = _final_norm(x2d, moe_tiles, final_norm_g[None], tm)
    return out.reshape(batch, seq, d)
```

```python
import functools
import math

import jax
import jax.numpy as jnp
from jax import lax
from jax.experimental import pallas as pl
from jax.experimental.pallas import tpu as pltpu

F32 = jnp.float32
I32 = jnp.int32
MXU_DTYPE = jnp.bfloat16

LANES = 128
SUBLANES = 8
VMEM_LIMIT_BYTES = 56 * 1024 * 1024

EPS = 1e-6
GRID_W = 64
HEAD_DIM = 64
POOL_WINDOWS = (2, 4, 8, 16)
POOL_HALO = 8
MLSTM_CHUNK = 128
MLSTM_CONV = 5
CONV_HALO = 8
ROPE_THETA = 10000.0
CA_HEADS = 4
N_EXPERTS = 16
EC_CAPACITY_FACTOR = 2
SMEM_1D_TILE = 1024
N_IDX_SLOTS = 3
MOE_SUB_COLS = 512
GATHER_DMA_PRIORITY = 1
SCATTER_GROUP = 16
NEG_INF = float("-inf")
LOG2_E = math.log2(math.e)


def _cparams(*sem):
    return pltpu.CompilerParams(dimension_semantics=sem, vmem_limit_bytes=VMEM_LIMIT_BYTES)


def _mm(a, b):
    return jnp.dot(a.astype(MXU_DTYPE), b.astype(MXU_DTYPE), preferred_element_type=F32)


def _mm_nt(a, b):
    return lax.dot_general(a.astype(MXU_DTYPE), b.astype(MXU_DTYPE), (((1,), (1,)), ((), ())),
                           preferred_element_type=F32)


def _split2(x):
    hi = x.astype(MXU_DTYPE)
    return hi, (x - hi.astype(F32)).astype(MXU_DTYPE)


def _split3(x):
    hi = x.astype(MXU_DTYPE)
    r1 = x - hi.astype(F32)
    mid = r1.astype(MXU_DTYPE)
    lo = (r1 - mid.astype(F32)).astype(MXU_DTYPE)
    return hi, mid, lo


def _rms(x, g):
    return x * lax.rsqrt(jnp.mean(x * x, axis=-1, keepdims=True) + EPS) * g


def _from_token_tiles(ref, rows):
    return jnp.concatenate([ref[pl.ds(s, rows, stride=SUBLANES), :] for s in range(SUBLANES)], axis=1)


def _to_token_tiles(ref, val, rows):
    for s in range(SUBLANES):
        ref[pl.ds(s, rows, stride=SUBLANES), :] = val[:, s * LANES:(s + 1) * LANES]


def _inproj_body(x_ref, g_ref, w_ref, wg_ref, o_ref, og_ref):
    hb = _rms(x_ref[...], g_ref[...]).astype(MXU_DTYPE)
    o_ref[...] = jnp.dot(hb, w_ref[...], preferred_element_type=F32).astype(o_ref.dtype)
    og_ref[...] = jnp.dot(hb, wg_ref[...], preferred_element_type=F32)


def _inproj_merge_body(x_ref, moe_ref, g_ref, w_ref, wg_ref, xo_ref, o_ref, og_ref, *, tm):
    x = x_ref[...] + _from_token_tiles(moe_ref, tm)
    xo_ref[...] = x
    hb = _rms(x, g_ref[...]).astype(MXU_DTYPE)
    o_ref[...] = jnp.dot(hb, w_ref[...], preferred_element_type=F32).astype(o_ref.dtype)
    og_ref[...] = jnp.dot(hb, wg_ref[...], preferred_element_type=F32)


def _inproj(x2d, moe_tiles, g, w_main, w_gate, tm):
    t, d = x2d.shape
    n = w_main.shape[1]
    grid = (t // tm,)
    row = lambda i: (i, 0)
    fixed = lambda i: (0, 0)
    gw = w_gate.shape[1]
    w_specs = [pl.BlockSpec((1, d), fixed), pl.BlockSpec((d, n), fixed), pl.BlockSpec((d, gw), fixed)]
    o_specs = [pl.BlockSpec((tm, n), row), pl.BlockSpec((tm, gw), row)]
    o_shapes = [jax.ShapeDtypeStruct((t, n), MXU_DTYPE), jax.ShapeDtypeStruct((t, gw), F32)]
    if moe_tiles is None:
        proj, gates = pl.pallas_call(
            _inproj_body, grid=grid, in_specs=[pl.BlockSpec((tm, d), row)] + w_specs, out_specs=o_specs,
            out_shape=o_shapes, compiler_params=_cparams("parallel"), name="inproj",
        )(x2d, g, w_main, w_gate)
        return x2d, proj, gates
    xo, proj, gates = pl.pallas_call(
        functools.partial(_inproj_merge_body, tm=tm), grid=grid,
        in_specs=[pl.BlockSpec((tm, d), row), pl.BlockSpec((tm * SUBLANES, LANES), row)] + w_specs,
        out_specs=[pl.BlockSpec((tm, d), row)] + o_specs,
        out_shape=[jax.ShapeDtypeStruct((t, d), F32)] + o_shapes,
        compiler_params=_cparams("parallel"), name="inproj_merge",
    )(x2d, moe_tiles, g, w_main, w_gate)
    return xo, proj, gates


def _pool_body(u_ref, w_ref, sc_ref, o_ref, upad_ref, *, seq, tc):
    i = pl.program_id(1)

    @pl.when(i == 0)
    def _():
        zeros = jnp.zeros((POOL_HALO, LANES), F32)
        for hf in range(2):
            upad_ref[hf, 0:POOL_HALO, :] = zeros
            upad_ref[hf, POOL_HALO + seq:POOL_HALO + seq + POOL_HALO, :] = zeros
            upad_ref[hf, POOL_HALO:POOL_HALO + seq, :] = u_ref[:, hf * LANES:(hf + 1) * LANES].astype(F32)

    t0 = pl.multiple_of(i * tc, tc)
    t = t0 + lax.broadcasted_iota(I32, (tc, LANES), 0)
    lane = lax.broadcasted_iota(I32, (tc, LANES), 1)
    upper = lane >= (LANES // 2)
    halves = []
    for hf in range(2):
        w_lo, w_hi = POOL_WINDOWS[2 * hf], POOL_WINDOWS[2 * hf + 1]
        acc = jnp.zeros((tc, LANES), F32)
        for j in range(-(w_hi // 2), w_hi // 2):
            piece = upad_ref[hf, pl.ds(t0 + (POOL_HALO + j), tc), :]
            if -(w_lo // 2) <= j < w_lo // 2:
                acc = acc + piece
            else:
                acc = acc + jnp.where(upper, piece, 0.0)
        half_w = jnp.where(upper, w_hi // 2, w_lo // 2)
        cnt = jnp.minimum(t + half_w, seq) - jnp.maximum(t - half_w, 0)
        centre = upad_ref[hf, pl.ds(t0 + POOL_HALO, tc), :]
        halves.append(acc / cnt.astype(F32) - centre)
    d = jnp.concatenate(halves, axis=1)
    o_ref[...] = (_mm(d, w_ref[...]) * sc_ref[...]).astype(o_ref.dtype)


def _pool_mixer(proj3, w_bd, scale, col_block, tc):
    b, seq, _ = proj3.shape
    pw = w_bd.shape[0]
    return pl.pallas_call(
        functools.partial(_pool_body, seq=seq, tc=tc), grid=(b, seq // tc),
        in_specs=[pl.BlockSpec((None, seq, pw), lambda bi, i: (bi, 0, col_block)),
                  pl.BlockSpec((pw, pw), lambda bi, i: (0, 0)),
                  pl.BlockSpec((1, pw), lambda bi, i: (0, 0))],
        out_specs=pl.BlockSpec((None, tc, pw), lambda bi, i: (bi, i, 0)),
        out_shape=jax.ShapeDtypeStruct((b, seq, pw), MXU_DTYPE),
        scratch_shapes=[pltpu.VMEM((pw // LANES, seq + 2 * POOL_HALO, LANES), F32)],
        compiler_params=_cparams("parallel", "arbitrary"), name="pool_mixer",
    )(proj3, w_bd, scale)


def _mlstm_body(q_ref, k_ref, v_ref, o_ref, gt_ref, cwq_ref, cwk_ref, gb_ref, ng_ref, y_ref,
                qpad_ref, kpad_ref, qc_ref, kc_ref, kt_ref, gtt_ref, hf_ref, hb_ref, cn_ref, m_ref,
                *, seq, n_heads):
    L = MLSTM_CHUNK
    nc = seq // L
    hd = HEAD_DIM
    gbias = gb_ref[...]

    zeros = jnp.zeros((CONV_HALO, LANES), F32)
    for pad_ref, src_ref in ((qpad_ref, q_ref), (kpad_ref, k_ref)):
        pad_ref[0:CONV_HALO, :] = zeros
        pad_ref[CONV_HALO + seq:CONV_HALO + seq + CONV_HALO, :] = zeros
        pad_ref[CONV_HALO:CONV_HALO + seq, :] = src_ref[...].astype(F32)
    win_rows = L + 2 * CONV_HALO

    def conv_chunk(ci, carry):
        t0 = pl.multiple_of(ci * L, L)
        for pad_ref, cw_ref, dst_ref, post in ((qpad_ref, cwq_ref, qc_ref, 1.0),
                                               (kpad_ref, cwk_ref, kc_ref, 1.0 / math.sqrt(hd))):
            acc = jnp.zeros((L, LANES), F32)
            for j in range(MLSTM_CONV):
                off = j - MLSTM_CONV // 2
                piece = pad_ref[pl.ds(t0 + (CONV_HALO + off), L), :]
                acc = acc + piece * cw_ref[j:j + 1, :]
            act = acc * jax.nn.sigmoid(acc)
            act = act * post if post != 1.0 else act
            dst_ref[pl.ds(t0, L), :] = act
            if dst_ref is kc_ref:
                kt_ref[:, pl.ds(t0, L)] = act.T
        gtt_ref[:, pl.ds(t0, L)] = (gt_ref[pl.ds(t0, L), :] + gbias).T
        return carry

    lax.fori_loop(0, nc, conv_chunk, 0)

    cn_ref[...] = jnp.zeros(cn_ref.shape, F32)
    m_ref[...] = jnp.zeros(m_ref.shape, F32)

    row = lax.broadcasted_iota(I32, (L, L), 0)
    col = lax.broadcasted_iota(I32, (L, L), 1)
    tri = ((col <= row).astype(MXU_DTYPE), (col >= row).astype(MXU_DTYPE))
    tri_t = (tri[1], tri[0])
    keep = (col <= row, col >= row)
    gate_lane = lax.broadcasted_iota(I32, (L, LANES), 1)
    gate_sub = lax.broadcasted_iota(I32, (LANES, L), 0)
    is_i, is_f = gate_lane < 4, (gate_lane >= 4) & (gate_lane < 8)
    is_i_t, is_f_t = gate_sub < 4, (gate_sub >= 4) & (gate_sub < 8)
    head_lanes = (gate_lane < hd, gate_lane >= hd)
    sel_r = lax.broadcasted_iota(I32, (LANES, 2 * LANES), 0)
    sel_c = lax.broadcasted_iota(I32, (LANES, 2 * LANES), 1)
    sel = [[(sel_r == jnp.where(sel_c < LANES, 4 + d * 2 + hh, d * 2 + hh)).astype(MXU_DTYPE) for hh in range(2)]
           for d in range(2)]
    ones_ext = jnp.ones((L, LANES), MXU_DTYPE)

    def cumulative(lf, left, d):
        hi, lo = _split2(lf)
        mat = tri[d] if left else tri_t[d]
        mm = (lambda x: jnp.dot(mat, x, preferred_element_type=F32)) if left else (
            lambda x: jnp.dot(x, mat, preferred_element_type=F32))
        return mm(hi) + mm(lo)

    chains = [(d, hh) for d in range(2) for hh in range(2)]
    f32dot = functools.partial(jnp.dot, preferred_element_type=F32)

    def chunk_step(it, carry):
        t0s = [pl.multiple_of((it if d == 0 else nc - 1 - it) * L, L) for d in range(2)]
        gpre = [gt_ref[pl.ds(t0, L), :] + gbias for t0 in t0s]
        gpre_t = [gtt_ref[:, pl.ds(t0, L)] for t0 in t0s]
        gm = [jnp.where(is_i, gpre[d], cumulative(jnp.where(is_f, jax.nn.log_sigmoid(gpre[d]), 0.0), True, d))
              for d in range(2)]
        gmt = [jnp.where(is_i_t, gpre_t[d], cumulative(jnp.where(is_f_t, jax.nn.log_sigmoid(gpre_t[d]), 0.0), False, d))
               for d in range(2)]
        gm2 = [_split2(g) for g in gm]
        qc = [qc_ref[pl.ds(t0, L), :] for t0 in t0s]
        kc = [kc_ref[pl.ds(t0, L), :].astype(MXU_DTYPE) for t0 in t0s]
        kt = [kt_ref[:, pl.ds(t0, L)].astype(MXU_DTYPE) for t0 in t0s]
        vc = [v_ref[pl.ds(t0, L), :] for t0 in t0s]
        v_ext = [jnp.concatenate([v.astype(MXU_DTYPE), ones_ext], axis=1) for v in vc]

        rep = [f32dot(gm2[d][0], sel[d][hh]) + f32dot(gm2[d][1], sel[d][hh]) for d, hh in chains]
        b_rep = [r[:, :LANES] for r in rep]
        li_rep = [r[:, LANES:] for r in rep]
        li_row = [gmt[d][d * 2 + hh:d * 2 + hh + 1, :] for d, hh in chains]
        b_row = [gmt[d][4 + d * 2 + hh:5 + d * 2 + hh, :] for d, hh in chains]
        m_st = [m_ref[hh * 2 + d] for d, hh in chains]
        cn = [cn_ref[hh * 2 + d] for d, hh in chains]
        qm = [jnp.where(head_lanes[hh], qc[d], 0.0).astype(MXU_DTYPE) for d, hh in chains]
        s_qk = [lax.dot_general(qm[c], kc[d], (((1,), (1,)), ((), ())), preferred_element_type=F32)
                for c, (d, hh) in enumerate(chains)]
        qcn = [f32dot(qm[c], cn[c].astype(MXU_DTYPE)) for c in range(4)]

        dmat = [jnp.where(keep[d], b_rep[c] - b_row[c] + li_row[c], NEG_INF) for c, (d, hh) in enumerate(chains)]
        inter = [b_rep[c] + m_st[c] for c in range(4)]
        m_j = [jnp.maximum(inter[c], jnp.max(dmat[c], axis=-1, keepdims=True)) for c in range(4)]
        a_mat = [jnp.exp(dmat[c] - m_j[c]) * s_qk[c] for c in range(4)]
        a_int = [jnp.exp(inter[c] - m_j[c]) for c in range(4)]
        av = [f32dot(a_mat[c].astype(MXU_DTYPE), v_ext[d]) for c, (d, hh) in enumerate(chains)]
        num = [av[c][:, :LANES] + a_int[c] * qcn[c][:, :LANES] for c in range(4)]
        den = [av[c][:, LANES:] + a_int[c] * qcn[c][:, LANES:] for c in range(4)]
        h_out = [num[c] / jnp.maximum(jnp.abs(den[c]), jnp.exp(-m_j[c])) for c in range(4)]

        b_last = [b_rep[c][(L - 1 if d == 0 else 0):(L if d == 0 else 1), :] for c, (d, hh) in enumerate(chains)]
        g_row = [b_last[c] - b_row[c] + li_row[c] for c in range(4)]
        g_rep = [b_last[c] - b_rep[c] + li_rep[c] for c in range(4)]
        m_new = [jnp.maximum(b_last[c] + m_st[c], jnp.max(g_row[c], axis=-1, keepdims=True)) for c in range(4)]
        w_rep = [jnp.exp(g_rep[c] - m_new[c]) for c in range(4)]
        decay = [jnp.exp(b_last[c] + m_st[c] - m_new[c]) for c in range(4)]
        wv_ext = [jnp.concatenate([w_rep[c] * vc[d].astype(F32), w_rep[c]], axis=1).astype(MXU_DTYPE)
                  for c, (d, hh) in enumerate(chains)]
        upd = [f32dot(kt[d], wv_ext[c]) for c, (d, hh) in enumerate(chains)]
        for c, (d, hh) in enumerate(chains):
            cn_ref[hh * 2 + d] = jnp.concatenate([decay[c], decay[c]], axis=1) * cn[c] + upd[c]
            m_ref[hh * 2 + d] = m_new[c]
        hf_ref[pl.ds(t0s[0], L), :] = jnp.where(head_lanes[0], h_out[0], h_out[1])
        hb_ref[pl.ds(t0s[1], L), :] = jnp.where(head_lanes[0], h_out[2], h_out[3])
        return carry

    lax.fori_loop(0, nc, chunk_step, 0)

    ng = ng_ref[...]
    lane = lax.broadcasted_iota(I32, (L, LANES), 1)
    upper = lane >= hd

    def out_chunk(ci, carry):
        t0 = pl.multiple_of(ci * L, L)
        hs = hf_ref[pl.ds(t0, L), :] + hb_ref[pl.ds(t0, L), :]
        sq = hs * hs
        ms_lo = jnp.sum(jnp.where(upper, 0.0, sq), axis=-1, keepdims=True) / hd
        ms_hi = jnp.sum(jnp.where(upper, sq, 0.0), axis=-1, keepdims=True) / hd
        ms = jnp.where(upper, ms_hi, ms_lo)
        y = hs * lax.rsqrt(ms + EPS) * ng
        y_ref[pl.ds(t0, L), :] = (y * jax.nn.sigmoid(o_ref[pl.ds(t0, L), :].astype(F32))).astype(y_ref.dtype)
        return carry

    lax.fori_loop(0, nc, out_chunk, 0)


def _mlstm(proj3, gates3, conv_w, gate_bias_row, norm_g_row, n_heads, cols):
    b, seq, _ = proj3.shape
    qb, kb, vb, ob = cols
    n_pairs = n_heads // 2
    width = n_heads * HEAD_DIM
    blk = lambda base: pl.BlockSpec((None, seq, LANES), lambda bi, p, base=base: (bi, 0, base + p))
    k_conv_base = width // LANES
    return pl.pallas_call(
        functools.partial(_mlstm_body, seq=seq, n_heads=n_heads), grid=(b, n_pairs),
        in_specs=[blk(qb), blk(kb), blk(vb), blk(ob),
                  pl.BlockSpec((None, seq, LANES), lambda bi, p: (bi, 0, p)),
                  pl.BlockSpec((MLSTM_CONV, LANES), lambda bi, p: (0, p)),
                  pl.BlockSpec((MLSTM_CONV, LANES), lambda bi, p: (0, k_conv_base + p)),
                  pl.BlockSpec((1, LANES), lambda bi, p: (0, p)),
                  pl.BlockSpec((1, LANES), lambda bi, p: (0, p))],
        out_specs=pl.BlockSpec((None, seq, LANES), lambda bi, p: (bi, 0, p)),
        out_shape=jax.ShapeDtypeStruct((b, seq, width), MXU_DTYPE),
        scratch_shapes=[pltpu.VMEM((seq + 2 * CONV_HALO, LANES), F32), pltpu.VMEM((seq + 2 * CONV_HALO, LANES), F32),
                        pltpu.VMEM((seq, LANES), F32), pltpu.VMEM((seq, LANES), F32),
                        pltpu.VMEM((LANES, seq), F32), pltpu.VMEM((LANES, seq), F32),
                        pltpu.VMEM((seq, LANES), F32), pltpu.VMEM((seq, LANES), F32),
                        pltpu.VMEM((4, LANES, 2 * LANES), F32),
                        pltpu.VMEM((4, 1, LANES), F32)],
        compiler_params=_cparams("parallel", "parallel"), name="mlstm",
    )(proj3, proj3, proj3, proj3, gates3, conv_w, conv_w, gate_bias_row, norm_g_row)


def _group_mean_sq(x, gmat):
    hi = (x * x).astype(MXU_DTYPE)
    lo = (x * x - hi.astype(F32)).astype(MXU_DTYPE)
    ssq = jnp.dot(hi, gmat, preferred_element_type=F32) + jnp.dot(lo, gmat, preferred_element_type=F32)
    return ssq * (1.0 / HEAD_DIM)


def _norm_rope(x, g_row, gmat, cos_t, sin_t):
    width = x.shape[1]
    xn = x * lax.rsqrt(_group_mean_sq(x, gmat) + EPS) * g_row
    lane = lax.broadcasted_iota(I32, x.shape, 1)
    partner = jnp.where(lane % 2 == 0, pltpu.roll(xn, width - 1, 1), pltpu.roll(xn, 1, 1))
    return xn * cos_t + partner * sin_t


def _qkprep_body(q_ref, k_ref, v_ref, qg_ref, kg_ref, gq_ref, gk_ref, cos_ref, sin_ref, qo_ref, kt_ref, vo_ref,
                 *, n_q_heads, n_kv_heads):
    hd = HEAD_DIM
    cos2, sin2 = cos_ref[...], sin_ref[...]
    reps = n_q_heads * hd // LANES
    q = _norm_rope(q_ref[...].astype(F32), qg_ref[...], gq_ref[...],
                   jnp.concatenate([cos2] * reps, axis=1), jnp.concatenate([sin2] * reps, axis=1))
    q = q * (hd ** -0.5 * LOG2_E)
    for h in range(n_q_heads):
        qo_ref[h] = q[:, h * hd:(h + 1) * hd].astype(qo_ref.dtype)
    k = _norm_rope(k_ref[...].astype(F32), kg_ref[...], gk_ref[...], cos2, sin2)
    kt = k.T
    v = v_ref[...].astype(F32)
    kv_width = v.shape[1]
    lane = lax.broadcasted_iota(I32, v.shape, 1)
    for h in range(n_kv_heads):
        kt_ref[h] = kt[h * hd:(h + 1) * hd, :].astype(kt_ref.dtype)
        vh = v if h == 0 else pltpu.roll(v, kv_width - h * hd, 1)
        vo_ref[h] = jnp.where(lane < hd, vh, jnp.where(lane == hd, 1.0, 0.0)).astype(vo_ref.dtype)


def _qkprep(proj3, q_gain_row, k_gain_row, gq, gk, cos2, sin2, n_q_heads, n_kv_heads, cols, ts):
    b, seq, _ = proj3.shape
    qcol, kcol, vcol = cols
    qw, kw = n_q_heads * HEAD_DIM, n_kv_heads * HEAD_DIM
    fixed = lambda bi, i: (0, 0)
    return pl.pallas_call(
        functools.partial(_qkprep_body, n_q_heads=n_q_heads, n_kv_heads=n_kv_heads), grid=(b, seq // ts),
        in_specs=[pl.BlockSpec((None, ts, qw), lambda bi, i: (bi, i, qcol)),
                  pl.BlockSpec((None, ts, kw), lambda bi, i: (bi, i, kcol)),
                  pl.BlockSpec((None, ts, kw), lambda bi, i: (bi, i, vcol)),
                  pl.BlockSpec((1, qw), fixed), pl.BlockSpec((1, kw), fixed),
                  pl.BlockSpec((qw, qw), fixed), pl.BlockSpec((kw, kw), fixed),
                  pl.BlockSpec((ts, kw), lambda bi, i: (i, 0)), pl.BlockSpec((ts, kw), lambda bi, i: (i, 0))],
        out_specs=[pl.BlockSpec((None, n_q_heads, ts, HEAD_DIM), lambda bi, i: (bi, 0, i, 0)),
                   pl.BlockSpec((None, n_kv_heads, HEAD_DIM, ts), lambda bi, i: (bi, 0, 0, i)),
                   pl.BlockSpec((None, n_kv_heads, ts, kw), lambda bi, i: (bi, 0, i, 0))],
        out_shape=[jax.ShapeDtypeStruct((b, n_q_heads, seq, HEAD_DIM), MXU_DTYPE),
                   jax.ShapeDtypeStruct((b, n_kv_heads, HEAD_DIM, seq), MXU_DTYPE),
                   jax.ShapeDtypeStruct((b, n_kv_heads, seq, kw), MXU_DTYPE)],
        compiler_params=_cparams("parallel", "parallel"), name="qk_prep",
    )(proj3, proj3, proj3, q_gain_row, k_gain_row, gq, gk, cos2, sin2)


def _flash_body(q_ref, kt_ref, v_ref, o_ref, *, seq, tq, tk, group):
    hd = HEAD_DIM
    rows = group * tq
    q = q_ref[...].reshape(rows, hd)
    m = jnp.full((rows, 1), NEG_INF, F32)
    acc = jnp.zeros((rows, v_ref.shape[1]), F32)
    for kc in range(seq // tk):
        s = jnp.dot(q, kt_ref[:, kc * tk:(kc + 1) * tk], preferred_element_type=F32)
        m_new = jnp.maximum(m, jnp.max(s, axis=-1, keepdims=True))
        p = jnp.exp2(s - m_new)
        acc = jnp.exp2(m - m_new) * acc + jnp.dot(p.astype(v_ref.dtype), v_ref[kc * tk:(kc + 1) * tk, :],
                                                  preferred_element_type=F32)
        m = m_new
    o = acc[:, :hd] / acc[:, hd:hd + 1]
    o_ref[...] = jnp.concatenate([o[g * tq:(g + 1) * tq] for g in range(group)], axis=1).astype(o_ref.dtype)


def _flash_attention(q4, kt4, v4, tq, tk):
    b, n_q_heads, seq, hd = q4.shape
    n_kv_heads = kt4.shape[1]
    group = n_q_heads // n_kv_heads
    return pl.pallas_call(
        functools.partial(_flash_body, seq=seq, tq=tq, tk=tk, group=group), grid=(b, n_kv_heads, seq // tq),
        in_specs=[pl.BlockSpec((None, group, tq, hd), lambda bi, h, i: (bi, h, i, 0)),
                  pl.BlockSpec((None, None, hd, seq), lambda bi, h, i: (bi, h, 0, 0)),
                  pl.BlockSpec((None, None, seq, v4.shape[3]), lambda bi, h, i: (bi, h, 0, 0))],
        out_specs=pl.BlockSpec((None, tq, group * hd), lambda bi, h, i: (bi, i, h)),
        out_shape=jax.ShapeDtypeStruct((b, seq, n_q_heads * hd), MXU_DTYPE),
        compiler_params=_cparams("parallel", "parallel", "parallel"), name="flash_gqa",
    )(q4, kt4, v4)


def _kvproj_body(m_ref, w_ref, o_ref):
    o_ref[...] = _mm(m_ref[...], w_ref[...]).astype(o_ref.dtype)


def _kvproj(mem2d, wkv, tm):
    t, d = mem2d.shape
    n = wkv.shape[1]
    return pl.pallas_call(
        _kvproj_body, grid=(t // tm,),
        in_specs=[pl.BlockSpec((tm, d), lambda i: (i, 0)), pl.BlockSpec((d, n), lambda i: (0, 0))],
        out_specs=pl.BlockSpec((tm, n), lambda i: (i, 0)), out_shape=jax.ShapeDtypeStruct((t, n), MXU_DTYPE),
        compiler_params=_cparams("parallel"), name="mem_kv_proj",
    )(mem2d, wkv)


def _crossattn_body(x_ref, yp_ref, ym_ref, ya_ref, wp_ref, wm_ref, wa_ref, kv_ref, gm_ref, wq_ref, wo_ref, gf_ref,
                    rwt_ref, xo_ref, hf_ref, lg_ref, *, ts):
    x = (x_ref[...] + _mm(yp_ref[...], wp_ref[...]) + _mm(ym_ref[...], wm_ref[...])
         + _mm(ya_ref[...], wa_ref[...]))
    q = _mm(_rms(x, gm_ref[...]), wq_ref[...])
    ca_width = q.shape[1]
    dh = ca_width // CA_HEADS
    kv = kv_ref[...]
    heads = []
    for h in range(CA_HEADS):
        s = _mm_nt(q[:, h * dh:(h + 1) * dh], kv[:, h * dh:(h + 1) * dh]) * (dh ** -0.5)
        e = jnp.exp(s - jnp.max(s, axis=-1, keepdims=True))
        p = e / jnp.sum(e, axis=-1, keepdims=True)
        heads.append(_mm(p, kv[:, ca_width + h * dh:ca_width + (h + 1) * dh]))
    x2 = x + _mm(jnp.concatenate(heads, axis=1), wo_ref[...])
    xo_ref[...] = x2
    hf = _rms(x2, gf_ref[...])
    _to_token_tiles(hf_ref, hf, ts)
    lg_ref[...] = _mm_nt(rwt_ref[...], hf)


def _crossattn(x3, yp, ym, ya, w_out, kv3, g_mem, wq, wo, g_ffn, router_wt, ts):
    b, seq, d = x3.shape
    n_mem, kvw = kv3.shape[1], kv3.shape[2]
    ne = router_wt.shape[0]
    fixed = lambda bi, i: (0, 0)
    tile = lambda w: pl.BlockSpec((None, ts, w), lambda bi, i: (bi, i, 0))
    wp, wm, wa = yp.shape[2], ym.shape[2], ya.shape[2]
    return pl.pallas_call(
        functools.partial(_crossattn_body, ts=ts), grid=(b, seq // ts),
        in_specs=[tile(d), tile(wp), tile(wm), tile(wa),
                  pl.BlockSpec((wp, d), fixed), pl.BlockSpec((wm, d), fixed), pl.BlockSpec((wa, d), fixed),
                  pl.BlockSpec((None, n_mem, kvw), lambda bi, i: (bi, 0, 0)),
                  pl.BlockSpec((1, d), fixed), pl.BlockSpec(wq.shape, fixed), pl.BlockSpec(wo.shape, fixed),
                  pl.BlockSpec((1, d), fixed), pl.BlockSpec((ne, d), fixed)],
        out_specs=[pl.BlockSpec((None, ts, d), lambda bi, i: (bi, i, 0)),
                   pl.BlockSpec((None, ts * SUBLANES, LANES), lambda bi, i: (bi, i, 0)),
                   pl.BlockSpec((None, ne, ts), lambda bi, i: (bi, 0, i))],
        out_shape=[jax.ShapeDtypeStruct((b, seq, d), F32),
                   jax.ShapeDtypeStruct((b, seq * SUBLANES, LANES), F32),
                   jax.ShapeDtypeStruct((b, ne, seq), F32)],
        compiler_params=_cparams("parallel", "parallel"), name="cross_attn",
    )(x3, yp, ym, ya, w_out[:wp], w_out[wp:wp + wm], w_out[wp + wm:], kv3, g_mem, wq, wo, g_ffn, router_wt)


def _excl_prefix(flags, upper_excl, ones):
    rows, seq = flags.shape
    off = jnp.zeros((rows, LANES), F32)
    out = []
    for c in range(seq // LANES):
        xc = flags[:, c * LANES:(c + 1) * LANES].astype(MXU_DTYPE)
        out.append(jnp.dot(xc, upper_excl, preferred_element_type=F32) + off)
        off = off + jnp.dot(xc, ones, preferred_element_type=F32)
    return jnp.concatenate(out, axis=1)


def _router_body(lg_ref, idx_ref, gate_ref, *, seq, cap):
    lg = lg_ref[...]
    ne = lg.shape[0]
    ex = jnp.exp(lg - jnp.max(lg, axis=0, keepdims=True))
    aff = ex / jnp.sum(ex, axis=0, keepdims=True)
    bits = pltpu.bitcast(aff, I32)

    thr = jnp.zeros((ne, 1), I32)
    for bit in range(30, -1, -1):
        cand = thr | (1 << bit)
        cnt = jnp.sum((bits >= cand).astype(F32), axis=1, keepdims=True)
        thr = jnp.where(cnt >= cap, cand, thr)

    r_i = lax.broadcasted_iota(I32, (LANES, LANES), 0)
    c_i = lax.broadcasted_iota(I32, (LANES, LANES), 1)
    upper_excl = (r_i < c_i).astype(MXU_DTYPE)
    ones = jnp.ones((LANES, LANES), MXU_DTYPE)

    gt = bits > thr
    eq = bits == thr
    n_gt = jnp.sum(gt.astype(F32), axis=1, keepdims=True)
    tie_rank = _excl_prefix(eq.astype(F32), upper_excl, ones)
    sel = gt | (eq & (tie_rank < cap - n_gt))
    pos = _excl_prefix(sel.astype(F32), upper_excl, ones).astype(I32)

    n_iota = lax.broadcasted_iota(I32, (ne, seq), 1)
    seq_bits = (seq - 1).bit_length()
    flag_bit = 2 * seq_bits
    packed = jnp.where(sel, n_iota | ((n_iota - pos) << seq_bits) | (1 << flag_bit), 0)
    gbits = jnp.where(sel, bits, 0)
    for k in range(seq_bits):
        sh = 1 << k
        moved_p = pltpu.roll(packed, seq - sh, 1)
        moved_g = pltpu.roll(gbits, seq - sh, 1)
        take = ((moved_p >> flag_bit) & 1) * ((moved_p >> (seq_bits + k)) & 1) == 1
        stay = ((packed >> flag_bit) & 1) * (1 - ((packed >> (seq_bits + k)) & 1)) == 1
        packed = jnp.where(take, moved_p, jnp.where(stay, packed, 0))
        gbits = jnp.where(take, moved_g, jnp.where(stay, gbits, 0))
    idx_ref[...] = packed[:, :cap] & (seq - 1)
    gate_ref[...] = pltpu.bitcast(gbits[:, :cap], F32)


def _router(logits_t, cap):
    b, ne, seq = logits_t.shape
    return pl.pallas_call(
        functools.partial(_router_body, seq=seq, cap=cap), grid=(b,),
        in_specs=[pl.BlockSpec((None, ne, seq), lambda bi: (bi, 0, 0))],
        out_specs=[pl.BlockSpec((None, ne, cap), lambda bi: (bi, 0, 0)),
                   pl.BlockSpec((None, ne, cap), lambda bi: (bi, 0, 0))],
        out_shape=[jax.ShapeDtypeStruct((b, ne, cap), I32), jax.ShapeDtypeStruct((b, ne, cap), F32)],
        compiler_params=_cparams("parallel"), name="ec_router",
    )(logits_t)


def _moe_body(*refs, seq, cap, ne, n_fc, stride, b_off, cast, chained):
    idx_hbm, gate_hbm, h_hbm, wg_ref, wu_ref, wd_ref = refs[:6]
    prev_hbm = refs[6] if chained else None
    refs = refs[6 + (1 if chained else 0):]
    out_hbm = refs[0]
    w_outs = refs[1:4] if cast else ()
    idx_s, gate_s, xbuf, xbf, ybuf, yacc, acc, gsem, isem, osem, psem = refs[1 + len(w_outs):]

    def prev_copy():
        return pltpu.make_async_copy(prev_hbm, out_hbm.at[pl.ds(0, b_off)], psem)
    b = pl.program_id(0)
    e = pl.program_id(1)
    c = pl.program_id(2)
    last = pl.num_programs(0) * ne - 1
    step = b * ne + e
    nstep = jnp.minimum(step + 1, last)
    xslot, nxslot = lax.rem(step, 2), lax.rem(step + 1, 2)
    islot, nislot = lax.rem(step, N_IDX_SLOTS), lax.rem(step + 1, N_IDX_SLOTS)
    tile = SUBLANES
    per = cap // n_fc

    def idx_copies(s, sl):
        src = pl.ds(pl.multiple_of((b_off * ne + s) * stride, stride), stride)
        dst = pl.ds(pl.multiple_of(sl * stride, stride), stride)
        return (pltpu.make_async_copy(idx_hbm.at[src], idx_s.at[dst], isem.at[0, sl]),
                pltpu.make_async_copy(gate_hbm.at[src], gate_s.at[dst], isem.at[1, sl]))

    def row_copy(src_row, k, sl):
        return pltpu.make_async_copy(h_hbm.at[pl.ds(src_row, tile), :],
                                     xbuf.at[sl, pl.ds(k * tile, tile), :], gsem.at[sl])

    def start_rows(s, isl, sl, k0, n):
        base = (b_off + s // ne) * seq
        ibase = isl * stride + k0
        for k in range(n):
            row_copy(pl.multiple_of((base + idx_s[ibase + k]) * tile, tile), k0 + k, sl).start(priority=GATHER_DMA_PRIORITY)

    def wait_gather(sl):
        pltpu.make_async_copy(h_hbm.at[pl.ds(0, cap * tile), :], xbuf.at[sl], gsem.at[sl]).wait()

    @pl.when(c == 0)
    def _():
        @pl.when(step == 0)
        def _():
            for cp in idx_copies(0, 0):
                cp.start()
            for cp in idx_copies(0, 0):
                cp.wait()
            for cp in idx_copies(jnp.minimum(1, last), 1):
                cp.start()
            if chained:
                prev_copy().start()

            def body(j, carry):
                start_rows(0, 0, 0, j * SUBLANES, SUBLANES)
                return carry
            lax.fori_loop(0, cap // SUBLANES, body, 0)

        for cp in idx_copies(0, nislot):
            cp.wait()
        for cp in idx_copies(jnp.minimum(step + 2, last), lax.rem(step + 2, N_IDX_SLOTS)):
            cp.start()
        wait_gather(xslot)
        xbf[...] = _from_token_tiles(xbuf.at[xslot], cap).astype(xbf.dtype)

        @pl.when(e == 0)
        def _():
            acc[...] = jnp.zeros(acc.shape, F32)

    start_rows(nstep, nislot, nxslot, pl.multiple_of(c * per, per), per)

    x = xbf[...]
    fc = wg_ref.shape[1]
    sub = min(fc, MOE_SUB_COLS)
    y = None
    for j in range(fc // sub):
        wg, wu, wd = wg_ref[:, j * sub:(j + 1) * sub], wu_ref[:, j * sub:(j + 1) * sub], wd_ref[j * sub:(j + 1) * sub, :]
        if cast:
            wg, wu, wd = wg.astype(MXU_DTYPE), wu.astype(MXU_DTYPE), wd.astype(MXU_DTYPE)
            w_outs[0][:, j * sub:(j + 1) * sub] = wg
            w_outs[1][:, j * sub:(j + 1) * sub] = wu
            w_outs[2][j * sub:(j + 1) * sub, :] = wd
        g = jnp.dot(x, wg, preferred_element_type=F32)
        u = jnp.dot(x, wu, preferred_element_type=F32)
        yj = _mm(g * jax.nn.sigmoid(g) * u, wd)
        y = yj if y is None else y + yj

    if n_fc == 1:
        _to_token_tiles(ybuf, y, cap)
    else:
        @pl.when(c == 0)
        def _():
            yacc[...] = y

        @pl.when(c > 0)
        def _():
            yacc[...] += y

    @pl.when(c == n_fc - 1)
    def _():
        if n_fc > 1:
            _to_token_tiles(ybuf, yacc[...], cap)

        ibase = islot * stride
        for k0 in range(0, cap, SCATTER_GROUP):
            rows = [pl.multiple_of(idx_s[ibase + k0 + j] * tile, tile) for j in range(SCATTER_GROUP)]
            new = [acc[pl.ds(rows[j], tile), :] + gate_s[ibase + k0 + j] * ybuf[(k0 + j) * tile:(k0 + j + 1) * tile, :]
                   for j in range(SCATTER_GROUP)]
            for j in range(SCATTER_GROUP):
                acc[pl.ds(rows[j], tile), :] = new[j]

        @pl.when(e == ne - 1)
        def _():
            cp = pltpu.make_async_copy(acc, out_hbm.at[b_off + b], osem)
            cp.start()
            cp.wait()

        @pl.when(step == last)
        def _():
            wait_gather(nxslot)
            for cp in idx_copies(0, lax.rem(step + 2, N_IDX_SLOTS)):
                cp.wait()
            if chained:
                prev_copy().wait()


def _moe(idx, gate, h_tiles, w_gate, w_up, w_down, up_block_off, seq, fc, b_off, n_b, prev_out, cast, layer=None):
    _, ne, cap = idx.shape
    d, f = w_down.shape[-1], w_down.shape[-2]
    n_fc = f // fc
    lead = () if layer is None else (layer,)
    w_spec = lambda shape, imap: pl.BlockSpec((None,) * (len(lead) + 1) + shape,
                                               lambda b, e, c: lead + imap(e, c))
    chained = prev_out is not None
    stride = -(-cap // SMEM_1D_TILE) * SMEM_1D_TILE
    flat = lambda a: jnp.pad(a, ((0, 0), (0, 0), (0, stride - cap))).reshape(-1)
    any_spec = pl.BlockSpec(memory_space=pl.ANY)
    assert chained == (b_off > 0)
    out_shape = [jax.ShapeDtypeStruct((b_off + n_b, seq * SUBLANES, LANES), F32)]
    out_specs = [any_spec]
    if cast:
        out_shape += [jax.ShapeDtypeStruct((ne, d, f), MXU_DTYPE), jax.ShapeDtypeStruct((ne, d, f), MXU_DTYPE),
                      jax.ShapeDtypeStruct((ne, f, d), MXU_DTYPE)]
        out_specs += [pl.BlockSpec((None, d, fc), lambda b, e, c: (e, 0, c)),
                      pl.BlockSpec((None, d, fc), lambda b, e, c: (e, 0, c)),
                      pl.BlockSpec((None, fc, d), lambda b, e, c: (e, c, 0))]
    return pl.pallas_call(
        functools.partial(_moe_body, seq=seq, cap=cap, ne=ne, n_fc=n_fc, stride=stride, b_off=b_off, cast=cast,
                          chained=chained),
        grid=(n_b, ne, n_fc),
        in_specs=[any_spec, any_spec, any_spec,
                  w_spec((d, fc), lambda e, c: (e, 0, c)),
                  w_spec((d, fc), lambda e, c: (e, 0, up_block_off + c)),
                  w_spec((fc, d), lambda e, c: (e, c, 0))] + ([any_spec] if chained else []),
        out_specs=out_specs, out_shape=out_shape,
        scratch_shapes=[pltpu.SMEM((N_IDX_SLOTS * stride,), I32), pltpu.SMEM((N_IDX_SLOTS * stride,), F32),
                        pltpu.VMEM((2, cap * SUBLANES, LANES), F32), pltpu.VMEM((cap, d), MXU_DTYPE),
                        pltpu.VMEM((cap * SUBLANES, LANES), F32), pltpu.VMEM((cap, d), F32),
                        pltpu.VMEM((seq * SUBLANES, LANES), F32),
                        pltpu.SemaphoreType.DMA((2,)), pltpu.SemaphoreType.DMA((2, N_IDX_SLOTS)),
                        pltpu.SemaphoreType.DMA, pltpu.SemaphoreType.DMA],
        compiler_params=_cparams("arbitrary", "arbitrary", "arbitrary"),
        name="ec_moe_cast" if cast else "ec_moe",
    )(flat(idx), flat(gate), h_tiles, w_gate, w_up, w_down, *([prev_out] if chained else []))


def _final_body(x_ref, moe_ref, g_ref, o_ref, *, tm):
    o_ref[...] = _rms(x_ref[...] + _from_token_tiles(moe_ref, tm), g_ref[...])


def _final_norm(x2d, moe_tiles, g, tm):
    t, d = x2d.shape
    row = lambda i: (i, 0)
    return pl.pallas_call(
        functools.partial(_final_body, tm=tm), grid=(t // tm,),
        in_specs=[pl.BlockSpec((tm, d), row), pl.BlockSpec((tm * SUBLANES, LANES), row),
                  pl.BlockSpec((1, d), lambda i: (0, 0))],
        out_specs=pl.BlockSpec((tm, d), row), out_shape=jax.ShapeDtypeStruct((t, d), F32),
        compiler_params=_cparams("parallel"), name="final_norm",
    )(x2d, moe_tiles, g)


def _rope_tables(seq):
    rows = seq // GRID_W
    row_ids = jnp.repeat(jnp.arange(rows), GRID_W).astype(F32)
    col_ids = jnp.tile(jnp.arange(GRID_W), rows).astype(F32)
    n_freq = HEAD_DIM // 4
    inv_freq = ROPE_THETA ** (-jnp.arange(n_freq, dtype=F32) / n_freq)
    ang = jnp.concatenate([row_ids[:, None] * inv_freq, col_ids[:, None] * inv_freq], axis=-1)
    cos, sin = jnp.cos(ang), jnp.sin(ang)
    cos_i = jnp.repeat(cos, 2, axis=1)
    sin_i = jnp.stack([-sin, sin], axis=-1).reshape(seq, HEAD_DIM)
    return jnp.tile(cos_i, (1, LANES // HEAD_DIM)), jnp.tile(sin_i, (1, LANES // HEAD_DIM))


def _group_matrix(width):
    gid = jnp.arange(width) // HEAD_DIM
    return (gid[:, None] == gid[None, :]).astype(MXU_DTYPE)


def _block_diag(w):
    g, c, _ = w.shape
    eye = jnp.eye(g, dtype=w.dtype)
    return (eye[:, None, :, None] * w[:, :, None, :]).reshape(g * c, g * c)


def _pick_tile(n, pref):
    t = min(n, pref)
    while n % t:
        t //= 2
    return t


def kernel(x, mem, norm_mix_g, w_in, pool_w, pool_scale, mlstm_conv_w, mlstm_gate_b, mlstm_norm_g, q_norm_g,
           k_norm_g, w_out, norm_mem_g, ca_wq, ca_wkv, ca_wo, norm_ffn_g, router_w, expert_w_gu, expert_w_down,
           final_norm_g):
    batch, seq, d = x.shape
    depth = w_in.shape[0]
    n_mem = mem.shape[1]
    pool_width = pool_scale.shape[1]
    m_width = mlstm_norm_g.shape[1]
    m_heads = m_width // HEAD_DIM
    n_gate = 2 * m_heads
    kv_width = (w_in.shape[2] - pool_width - 4 * m_width - 2 * n_gate) // 6
    q_width = 4 * kv_width
    n_q_heads, n_kv_heads = q_width // HEAD_DIM, kv_width // HEAD_DIM
    ne = router_w.shape[2]
    cap = EC_CAPACITY_FACTOR * seq // ne
    f_exp = expert_w_down.shape[2]
    t = batch * seq

    o_pool = 0
    o_mq = o_pool + pool_width
    o_mi = o_mq + 4 * m_width
    o_aq = o_mi + 2 * n_gate
    o_ak = o_aq + q_width
    c_pool = q_width // pool_width
    c_mq = (q_width + pool_width) // LANES
    m_cols = tuple(c_mq + j * (m_width // LANES) for j in range(4))
    c_ak = (q_width + pool_width + 4 * m_width) // kv_width
    cos2, sin2 = _rope_tables(seq)
    gq, gk = _group_matrix(q_width), _group_matrix(kv_width)
    n_pairs = m_heads // 2
    gate_cols = [[kind * n_gate + dr * m_heads + 2 * p + hh for kind in range(2) for dr in range(2) for hh in range(2)]
                 for p in range(n_pairs)]

    tm = _pick_tile(t, 512)
    ts = _pick_tile(seq, 512)
    x2d = x.reshape(t, d)
    mem2d = mem.reshape(batch * n_mem, d)
    moe_tiles = None
    for layer in range(depth):
        wl = w_in[layer]
        w_main = jnp.concatenate([wl[:, o_aq:o_ak], wl[:, o_pool:o_mi], wl[:, o_ak:]], axis=1).astype(MXU_DTYPE)
        w_gate = jnp.concatenate(
            [jnp.pad(wl[:, o_mi:o_aq][:, jnp.array(cols)], ((0, 0), (0, LANES - len(cols)))) for cols in gate_cols],
            axis=1).astype(MXU_DTYPE)
        gb_row = jnp.concatenate(
            [jnp.pad(mlstm_gate_b[layer][jnp.array(cols)], (0, LANES - len(cols))) for cols in gate_cols])[None]
        x2d, proj, gates = _inproj(x2d, moe_tiles, norm_mix_g[layer][None], w_main, w_gate, tm)
        proj3 = proj.reshape(batch, seq, proj.shape[1])
        gates3 = gates.reshape(batch, seq, gates.shape[1])

        y_pool = _pool_mixer(proj3, _block_diag(pool_w[layer]).astype(MXU_DTYPE), pool_scale[layer][None], c_pool,
                             _pick_tile(seq, 256))
        y_mlstm = _mlstm(proj3, gates3, mlstm_conv_w[layer], gb_row, mlstm_norm_g[layer][None], m_heads, m_cols)
        q4, kt4, v4 = _qkprep(proj3, jnp.tile(q_norm_g[layer], n_q_heads)[None],
                              jnp.tile(k_norm_g[layer], n_kv_heads)[None], gq, gk, cos2, sin2,
                              n_q_heads, n_kv_heads, (0, c_ak, c_ak + 1), ts)
        y_attn = _flash_attention(q4, kt4, v4, _pick_tile(seq, 256), _pick_tile(seq, 1024))

        kv = _kvproj(mem2d, ca_wkv[layer].astype(MXU_DTYPE), _pick_tile(batch * n_mem, 512))
        x3, h_tiles, logits_t = _crossattn(
            x2d.reshape(batch, seq, d), y_pool, y_mlstm, y_attn, w_out[layer].astype(MXU_DTYPE),
            kv.reshape(batch, n_mem, kv.shape[1]), norm_mem_g[layer][None],
            ca_wq[layer].astype(MXU_DTYPE), ca_wo[layer].astype(MXU_DTYPE), norm_ffn_g[layer][None],
            router_w[layer].T.astype(MXU_DTYPE), ts)
        x2d = x3.reshape(t, d)

        idx, gate = _router(logits_t, cap)
        h_flat = h_tiles.reshape(t * SUBLANES, LANES)
        fc_cast = _pick_tile(f_exp, 512)
        moe_tiles, wg_c, wu_c, wd_c = _moe(idx, gate, h_flat, expert_w_gu, expert_w_gu, expert_w_down,
                                           f_exp // fc_cast, seq, fc_cast, 0, 1, None, True, layer)
        if batch > 1:
            moe_tiles, = _moe(idx, gate, h_flat, wg_c, wu_c, wd_c, 0, seq, _pick_tile(f_exp, 1024), 1, batch - 1,
                              moe_tiles, False)
        moe_tiles = moe_tiles.reshape(t * SUBLANES, LANES)
    out = _final_norm(x2d, moe_tiles, final_norm_g[None], tm)
    return out.reshape(batch, seq, d)
```

```python
import functools
import math

import jax
import jax.numpy as jnp
from jax import lax
from jax.experimental import pallas as pl
from jax.experimental.pallas import tpu as pltpu

F32 = jnp.float32
I32 = jnp.int32
MXU_DTYPE = jnp.bfloat16

LANES = 128
SUBLANES = 8
VMEM_LIMIT_BYTES = 56 * 1024 * 1024

EPS = 1e-6
GRID_W = 64
HEAD_DIM = 64
POOL_WINDOWS = (2, 4, 8, 16)
POOL_HALO = 8
MLSTM_CHUNK = 128
MLSTM_CONV = 5
CONV_HALO = 8
ROPE_THETA = 10000.0
CA_HEADS = 4
N_EXPERTS = 16
EC_CAPACITY_FACTOR = 2
SMEM_1D_TILE = 1024
N_IDX_SLOTS = 3
MOE_SUB_COLS = 512
GATHER_DMA_PRIORITY = 1
SCATTER_GROUP = 16
NEG_INF = float("-inf")
LOG2_E = math.log2(math.e)


def _cparams(*sem):
    return pltpu.CompilerParams(dimension_semantics=sem, vmem_limit_bytes=VMEM_LIMIT_BYTES)


def _mm(a, b):
    return jnp.dot(a.astype(MXU_DTYPE), b.astype(MXU_DTYPE), preferred_element_type=F32)


def _mm_nt(a, b):
    return lax.dot_general(a.astype(MXU_DTYPE), b.astype(MXU_DTYPE), (((1,), (1,)), ((), ())),
                           preferred_element_type=F32)


def _split2(x):
    hi = x.astype(MXU_DTYPE)
    return hi, (x - hi.astype(F32)).astype(MXU_DTYPE)


def _split3(x):
    hi = x.astype(MXU_DTYPE)
    r1 = x - hi.astype(F32)
    mid = r1.astype(MXU_DTYPE)
    lo = (r1 - mid.astype(F32)).astype(MXU_DTYPE)
    return hi, mid, lo


def _rms(x, g):
    return x * lax.rsqrt(jnp.mean(x * x, axis=-1, keepdims=True) + EPS) * g


def _from_token_tiles(ref, rows):
    return jnp.concatenate([ref[pl.ds(s, rows, stride=SUBLANES), :] for s in range(SUBLANES)], axis=1)


def _to_token_tiles(ref, val, rows):
    for s in range(SUBLANES):
        ref[pl.ds(s, rows, stride=SUBLANES), :] = val[:, s * LANES:(s + 1) * LANES]


def _inproj_body(x_ref, g_ref, w_ref, wg_ref, o_ref, og_ref):
    hb = _rms(x_ref[...], g_ref[...]).astype(MXU_DTYPE)
    o_ref[...] = jnp.dot(hb, w_ref[...], preferred_element_type=F32).astype(o_ref.dtype)
    og_ref[...] = jnp.dot(hb, wg_ref[...], preferred_element_type=F32)


def _inproj_merge_body(x_ref, moe_ref, g_ref, w_ref, wg_ref, xo_ref, o_ref, og_ref, *, tm):
    x = x_ref[...] + _from_token_tiles(moe_ref, tm)
    xo_ref[...] = x
    hb = _rms(x, g_ref[...]).astype(MXU_DTYPE)
    o_ref[...] = jnp.dot(hb, w_ref[...], preferred_element_type=F32).astype(o_ref.dtype)
    og_ref[...] = jnp.dot(hb, wg_ref[...], preferred_element_type=F32)


def _inproj(x2d, moe_tiles, g, w_main, w_gate, tm):
    t, d = x2d.shape
    n = w_main.shape[1]
    grid = (t // tm,)
    row = lambda i: (i, 0)
    fixed = lambda i: (0, 0)
    gw = w_gate.shape[1]
    w_specs = [pl.BlockSpec((1, d), fixed), pl.BlockSpec((d, n), fixed), pl.BlockSpec((d, gw), fixed)]
    o_specs = [pl.BlockSpec((tm, n), row), pl.BlockSpec((tm, gw), row)]
    o_shapes = [jax.ShapeDtypeStruct((t, n), MXU_DTYPE), jax.ShapeDtypeStruct((t, gw), F32)]
    if moe_tiles is None:
        proj, gates = pl.pallas_call(
            _inproj_body, grid=grid, in_specs=[pl.BlockSpec((tm, d), row)] + w_specs, out_specs=o_specs,
            out_shape=o_shapes, compiler_params=_cparams("parallel"), name="inproj",
        )(x2d, g, w_main, w_gate)
        return x2d, proj, gates
    xo, proj, gates = pl.pallas_call(
        functools.partial(_inproj_merge_body, tm=tm), grid=grid,
        in_specs=[pl.BlockSpec((tm, d), row), pl.BlockSpec((tm * SUBLANES, LANES), row)] + w_specs,
        out_specs=[pl.BlockSpec((tm, d), row)] + o_specs,
        out_shape=[jax.ShapeDtypeStruct((t, d), F32)] + o_shapes,
        compiler_params=_cparams("parallel"), name="inproj_merge",
    )(x2d, moe_tiles, g, w_main, w_gate)
    return xo, proj, gates


def _pool_body(u_ref, w_ref, sc_ref, o_ref, upad_ref, *, seq, tc):
    i = pl.program_id(1)

    @pl.when(i == 0)
    def _():
        zeros = jnp.zeros((POOL_HALO, LANES), F32)
        for hf in range(2):
            upad_ref[hf, 0:POOL_HALO, :] = zeros
            upad_ref[hf, POOL_HALO + seq:POOL_HALO + seq + POOL_HALO, :] = zeros
            upad_ref[hf, POOL_HALO:POOL_HALO + seq, :] = u_ref[:, hf * LANES:(hf + 1) * LANES].astype(F32)

    t0 = pl.multiple_of(i * tc, tc)
    t = t0 + lax.broadcasted_iota(I32, (tc, LANES), 0)
    lane = lax.broadcasted_iota(I32, (tc, LANES), 1)
    upper = lane >= (LANES // 2)
    halves = []
    for hf in range(2):
        w_lo, w_hi = POOL_WINDOWS[2 * hf], POOL_WINDOWS[2 * hf + 1]
        acc = jnp.zeros((tc, LANES), F32)
        for j in range(-(w_hi // 2), w_hi // 2):
            piece = upad_ref[hf, pl.ds(t0 + (POOL_HALO + j), tc), :]
            if -(w_lo // 2) <= j < w_lo // 2:
                acc = acc + piece
            else:
                acc = acc + jnp.where(upper, piece, 0.0)
        half_w = jnp.where(upper, w_hi // 2, w_lo // 2)
        cnt = jnp.minimum(t + half_w, seq) - jnp.maximum(t - half_w, 0)
        centre = upad_ref[hf, pl.ds(t0 + POOL_HALO, tc), :]
        halves.append(acc / cnt.astype(F32) - centre)
    d = jnp.concatenate(halves, axis=1)
    o_ref[...] = (_mm(d, w_ref[...]) * sc_ref[...]).astype(o_ref.dtype)


def _pool_mixer(proj3, w_bd, scale, col_block, tc):
    b, seq, _ = proj3.shape
    pw = w_bd.shape[0]
    return pl.pallas_call(
        functools.partial(_pool_body, seq=seq, tc=tc), grid=(b, seq // tc),
        in_specs=[pl.BlockSpec((None, seq, pw), lambda bi, i: (bi, 0, col_block)),
                  pl.BlockSpec((pw, pw), lambda bi, i: (0, 0)),
                  pl.BlockSpec((1, pw), lambda bi, i: (0, 0))],
        out_specs=pl.BlockSpec((None, tc, pw), lambda bi, i: (bi, i, 0)),
        out_shape=jax.ShapeDtypeStruct((b, seq, pw), MXU_DTYPE),
        scratch_shapes=[pltpu.VMEM((pw // LANES, seq + 2 * POOL_HALO, LANES), F32)],
        compiler_params=_cparams("parallel", "arbitrary"), name="pool_mixer",
    )(proj3, w_bd, scale)


def _mlstm_body(q_ref, k_ref, v_ref, o_ref, gt_ref, cwq_ref, cwk_ref, gb_ref, ng_ref, y_ref,
                qpad_ref, kpad_ref, qc_ref, kc_ref, kt_ref, gtt_ref, hf_ref, hb_ref, cn_ref, m_ref,
                *, seq, n_heads):
    L = MLSTM_CHUNK
    nc = seq // L
    hd = HEAD_DIM
    gbias = gb_ref[...]

    zeros = jnp.zeros((CONV_HALO, LANES), F32)
    for pad_ref, src_ref in ((qpad_ref, q_ref), (kpad_ref, k_ref)):
        pad_ref[0:CONV_HALO, :] = zeros
        pad_ref[CONV_HALO + seq:CONV_HALO + seq + CONV_HALO, :] = zeros
        pad_ref[CONV_HALO:CONV_HALO + seq, :] = src_ref[...].astype(F32)
    win_rows = L + 2 * CONV_HALO

    def conv_chunk(ci, carry):
        t0 = pl.multiple_of(ci * L, L)
        for pad_ref, cw_ref, dst_ref, post in ((qpad_ref, cwq_ref, qc_ref, 1.0),
                                               (kpad_ref, cwk_ref, kc_ref, 1.0 / math.sqrt(hd))):
            acc = jnp.zeros((L, LANES), F32)
            for j in range(MLSTM_CONV):
                off = j - MLSTM_CONV // 2
                piece = pad_ref[pl.ds(t0 + (CONV_HALO + off), L), :]
                acc = acc + piece * cw_ref[j:j + 1, :]
            act = acc * jax.nn.sigmoid(acc)
            act = act * post if post != 1.0 else act
            dst_ref[pl.ds(t0, L), :] = act
            if dst_ref is kc_ref:
                kt_ref[:, pl.ds(t0, L)] = act.T
        gtt_ref[:, pl.ds(t0, L)] = (gt_ref[pl.ds(t0, L), :] + gbias).T
        return carry

    lax.fori_loop(0, nc, conv_chunk, 0)

    cn_ref[...] = jnp.zeros(cn_ref.shape, F32)
    m_ref[...] = jnp.zeros(m_ref.shape, F32)

    row = lax.broadcasted_iota(I32, (L, L), 0)
    col = lax.broadcasted_iota(I32, (L, L), 1)
    tri = ((col <= row).astype(MXU_DTYPE), (col >= row).astype(MXU_DTYPE))
    tri_t = (tri[1], tri[0])
    keep = (col <= row, col >= row)
    gate_lane = lax.broadcasted_iota(I32, (L, LANES), 1)
    gate_sub = lax.broadcasted_iota(I32, (LANES, L), 0)
    is_i, is_f = gate_lane < 4, (gate_lane >= 4) & (gate_lane < 8)
    is_i_t, is_f_t = gate_sub < 4, (gate_sub >= 4) & (gate_sub < 8)
    head_lanes = (gate_lane < hd, gate_lane >= hd)
    sel_r = lax.broadcasted_iota(I32, (LANES, 2 * LANES), 0)
    sel_c = lax.broadcasted_iota(I32, (LANES, 2 * LANES), 1)
    sel = [[(sel_r == jnp.where(sel_c < LANES, 4 + d * 2 + hh, d * 2 + hh)).astype(MXU_DTYPE) for hh in range(2)]
           for d in range(2)]
    ones_ext = jnp.ones((L, LANES), MXU_DTYPE)

    def cumulative(lf, left, d):
        hi, lo = _split2(lf)
        mat = tri[d] if left else tri_t[d]
        mm = (lambda x: jnp.dot(mat, x, preferred_element_type=F32)) if left else (
            lambda x: jnp.dot(x, mat, preferred_element_type=F32))
        return mm(hi) + mm(lo)

    chains = [(d, hh) for d in range(2) for hh in range(2)]
    f32dot = functools.partial(jnp.dot, preferred_element_type=F32)

    def chunk_step(it, carry):
        t0s = [pl.multiple_of((it if d == 0 else nc - 1 - it) * L, L) for d in range(2)]
        gpre = [gt_ref[pl.ds(t0, L), :] + gbias for t0 in t0s]
        gpre_t = [gtt_ref[:, pl.ds(t0, L)] for t0 in t0s]
        gm = [jnp.where(is_i, gpre[d], cumulative(jnp.where(is_f, jax.nn.log_sigmoid(gpre[d]), 0.0), True, d))
              for d in range(2)]
        gmt = [jnp.where(is_i_t, gpre_t[d], cumulative(jnp.where(is_f_t, jax.nn.log_sigmoid(gpre_t[d]), 0.0), False, d))
               for d in range(2)]
        gm2 = [_split2(g) for g in gm]
        qc = [qc_ref[pl.ds(t0, L), :] for t0 in t0s]
        kc = [kc_ref[pl.ds(t0, L), :].astype(MXU_DTYPE) for t0 in t0s]
        kt = [kt_ref[:, pl.ds(t0, L)].astype(MXU_DTYPE) for t0 in t0s]
        vc = [v_ref[pl.ds(t0, L), :] for t0 in t0s]
        v_ext = [jnp.concatenate([v.astype(MXU_DTYPE), ones_ext], axis=1) for v in vc]

        rep = [f32dot(gm2[d][0], sel[d][hh]) + f32dot(gm2[d][1], sel[d][hh]) for d, hh in chains]
        b_rep = [r[:, :LANES] for r in rep]
        li_rep = [r[:, LANES:] for r in rep]
        li_row = [gmt[d][d * 2 + hh:d * 2 + hh + 1, :] for d, hh in chains]
        b_row = [gmt[d][4 + d * 2 + hh:5 + d * 2 + hh, :] for d, hh in chains]
        m_st = [m_ref[hh * 2 + d] for d, hh in chains]
        cn = [cn_ref[hh * 2 + d] for d, hh in chains]
        qm = [jnp.where(head_lanes[hh], qc[d], 0.0).astype(MXU_DTYPE) for d, hh in chains]
        s_qk = [lax.dot_general(qm[c], kc[d], (((1,), (1,)), ((), ())), preferred_element_type=F32)
                for c, (d, hh) in enumerate(chains)]
        qcn = [f32dot(qm[c], cn[c].astype(MXU_DTYPE)) for c in range(4)]

        dmat = [jnp.where(keep[d], b_rep[c] - b_row[c] + li_row[c], NEG_INF) for c, (d, hh) in enumerate(chains)]
        inter = [b_rep[c] + m_st[c] for c in range(4)]
        m_j = [jnp.maximum(inter[c], jnp.max(dmat[c], axis=-1, keepdims=True)) for c in range(4)]
        a_mat = [jnp.exp(dmat[c] - m_j[c]) * s_qk[c] for c in range(4)]
        a_int = [jnp.exp(inter[c] - m_j[c]) for c in range(4)]
        av = [f32dot(a_mat[c].astype(MXU_DTYPE), v_ext[d]) for c, (d, hh) in enumerate(chains)]
        num = [av[c][:, :LANES] + a_int[c] * qcn[c][:, :LANES] for c in range(4)]
        den = [av[c][:, LANES:] + a_int[c] * qcn[c][:, LANES:] for c in range(4)]
        h_out = [num[c] / jnp.maximum(jnp.abs(den[c]), jnp.exp(-m_j[c])) for c in range(4)]

        b_last = [b_rep[c][(L - 1 if d == 0 else 0):(L if d == 0 else 1), :] for c, (d, hh) in enumerate(chains)]
        g_row = [b_last[c] - b_row[c] + li_row[c] for c in range(4)]
        g_rep = [b_last[c] - b_rep[c] + li_rep[c] for c in range(4)]
        m_new = [jnp.maximum(b_last[c] + m_st[c], jnp.max(g_row[c], axis=-1, keepdims=True)) for c in range(4)]
        w_rep = [jnp.exp(g_rep[c] - m_new[c]) for c in range(4)]
        decay = [jnp.exp(b_last[c] + m_st[c] - m_new[c]) for c in range(4)]
        wv_ext = [jnp.concatenate([w_rep[c] * vc[d].astype(F32), w_rep[c]], axis=1).astype(MXU_DTYPE)
                  for c, (d, hh) in enumerate(chains)]
        upd = [f32dot(kt[d], wv_ext[c]) for c, (d, hh) in enumerate(chains)]
        for c, (d, hh) in enumerate(chains):
            cn_ref[hh * 2 + d] = jnp.concatenate([decay[c], decay[c]], axis=1) * cn[c] + upd[c]
            m_ref[hh * 2 + d] = m_new[c]
        hf_ref[pl.ds(t0s[0], L), :] = jnp.where(head_lanes[0], h_out[0], h_out[1])
        hb_ref[pl.ds(t0s[1], L), :] = jnp.where(head_lanes[0], h_out[2], h_out[3])
        return carry

    lax.fori_loop(0, nc, chunk_step, 0)

    ng = ng_ref[...]
    lane = lax.broadcasted_iota(I32, (L, LANES), 1)
    upper = lane >= hd

    def out_chunk(ci, carry):
        t0 = pl.multiple_of(ci * L, L)
        hs = hf_ref[pl.ds(t0, L), :] + hb_ref[pl.ds(t0, L), :]
        sq = hs * hs
        ms_lo = jnp.sum(jnp.where(upper, 0.0, sq), axis=-1, keepdims=True) / hd
        ms_hi = jnp.sum(jnp.where(upper, sq, 0.0), axis=-1, keepdims=True) / hd
        ms = jnp.where(upper, ms_hi, ms_lo)
        y = hs * lax.rsqrt(ms + EPS) * ng
        y_ref[pl.ds(t0, L), :] = (y * jax.nn.sigmoid(o_ref[pl.ds(t0, L), :].astype(F32))).astype(y_ref.dtype)
        return carry

    lax.fori_loop(0, nc, out_chunk, 0)


def _mlstm(proj3, gates3, conv_w, gate_bias_row, norm_g_row, n_heads, cols):
    b, seq, _ = proj3.shape
    qb, kb, vb, ob = cols
    n_pairs = n_heads // 2
    width = n_heads * HEAD_DIM
    blk = lambda base: pl.BlockSpec((None, seq, LANES), lambda bi, p, base=base: (bi, 0, base + p))
    k_conv_base = width // LANES
    return pl.pallas_call(
        functools.partial(_mlstm_body, seq=seq, n_heads=n_heads), grid=(b, n_pairs),
        in_specs=[blk(qb), blk(kb), blk(vb), blk(ob),
                  pl.BlockSpec((None, seq, LANES), lambda bi, p: (bi, 0, p)),
                  pl.BlockSpec((MLSTM_CONV, LANES), lambda bi, p: (0, p)),
                  pl.BlockSpec((MLSTM_CONV, LANES), lambda bi, p: (0, k_conv_base + p)),
                  pl.BlockSpec((1, LANES), lambda bi, p: (0, p)),
                  pl.BlockSpec((1, LANES), lambda bi, p: (0, p))],
        out_specs=pl.BlockSpec((None, seq, LANES), lambda bi, p: (bi, 0, p)),
        out_shape=jax.ShapeDtypeStruct((b, seq, width), MXU_DTYPE),
        scratch_shapes=[pltpu.VMEM((seq + 2 * CONV_HALO, LANES), F32), pltpu.VMEM((seq + 2 * CONV_HALO, LANES), F32),
                        pltpu.VMEM((seq, LANES), F32), pltpu.VMEM((seq, LANES), F32),
                        pltpu.VMEM((LANES, seq), F32), pltpu.VMEM((LANES, seq), F32),
                        pltpu.VMEM((seq, LANES), F32), pltpu.VMEM((seq, LANES), F32),
                        pltpu.VMEM((4, LANES, 2 * LANES), F32),
                        pltpu.VMEM((4, 1, LANES), F32)],
        compiler_params=_cparams("parallel", "parallel"), name="mlstm",
    )(proj3, proj3, proj3, proj3, gates3, conv_w, conv_w, gate_bias_row, norm_g_row)


def _group_mean_sq(x, gmat):
    hi = (x * x).astype(MXU_DTYPE)
    lo = (x * x - hi.astype(F32)).astype(MXU_DTYPE)
    ssq = jnp.dot(hi, gmat, preferred_element_type=F32) + jnp.dot(lo, gmat, preferred_element_type=F32)
    return ssq * (1.0 / HEAD_DIM)


def _norm_rope(x, g_row, gmat, cos_t, sin_t):
    width = x.shape[1]
    xn = x * lax.rsqrt(_group_mean_sq(x, gmat) + EPS) * g_row
    lane = lax.broadcasted_iota(I32, x.shape, 1)
    partner = jnp.where(lane % 2 == 0, pltpu.roll(xn, width - 1, 1), pltpu.roll(xn, 1, 1))
    return xn * cos_t + partner * sin_t


def _qkprep_body(q_ref, k_ref, v_ref, qg_ref, kg_ref, gq_ref, gk_ref, cos_ref, sin_ref, qo_ref, kt_ref, vo_ref,
                 *, n_q_heads, n_kv_heads):
    hd = HEAD_DIM
    cos2, sin2 = cos_ref[...], sin_ref[...]
    reps = n_q_heads * hd // LANES
    q = _norm_rope(q_ref[...].astype(F32), qg_ref[...], gq_ref[...],
                   jnp.concatenate([cos2] * reps, axis=1), jnp.concatenate([sin2] * reps, axis=1))
    q = q * (hd ** -0.5 * LOG2_E)
    for h in range(n_q_heads):
        qo_ref[h] = q[:, h * hd:(h + 1) * hd].astype(qo_ref.dtype)
    k = _norm_rope(k_ref[...].astype(F32), kg_ref[...], gk_ref[...], cos2, sin2)
    kt = k.T
    v = v_ref[...].astype(F32)
    kv_width = v.shape[1]
    lane = lax.broadcasted_iota(I32, v.shape, 1)
    for h in range(n_kv_heads):
        kt_ref[h] = kt[h * hd:(h + 1) * hd, :].astype(kt_ref.dtype)
        vh = v if h == 0 else pltpu.roll(v, kv_width - h * hd, 1)
        vo_ref[h] = jnp.where(lane < hd, vh, jnp.where(lane == hd, 1.0, 0.0)).astype(vo_ref.dtype)


def _qkprep(proj3, q_gain_row, k_gain_row, gq, gk, cos2, sin2, n_q_heads, n_kv_heads, cols, ts):
    b, seq, _ = proj3.shape
    qcol, kcol, vcol = cols
    qw, kw = n_q_heads * HEAD_DIM, n_kv_heads * HEAD_DIM
    fixed = lambda bi, i: (0, 0)
    return pl.pallas_call(
        functools.partial(_qkprep_body, n_q_heads=n_q_heads, n_kv_heads=n_kv_heads), grid=(b, seq // ts),
        in_specs=[pl.BlockSpec((None, ts, qw), lambda bi, i: (bi, i, qcol)),
                  pl.BlockSpec((None, ts, kw), lambda bi, i: (bi, i, kcol)),
                  pl.BlockSpec((None, ts, kw), lambda bi, i: (bi, i, vcol)),
                  pl.BlockSpec((1, qw), fixed), pl.BlockSpec((1, kw), fixed),
                  pl.BlockSpec((qw, qw), fixed), pl.BlockSpec((kw, kw), fixed),
                  pl.BlockSpec((ts, kw), lambda bi, i: (i, 0)), pl.BlockSpec((ts, kw), lambda bi, i: (i, 0))],
        out_specs=[pl.BlockSpec((None, n_q_heads, ts, HEAD_DIM), lambda bi, i: (bi, 0, i, 0)),
                   pl.BlockSpec((None, n_kv_heads, HEAD_DIM, ts), lambda bi, i: (bi, 0, 0, i)),
                   pl.BlockSpec((None, n_kv_heads, ts, kw), lambda bi, i: (bi, 0, i, 0))],
        out_shape=[jax.ShapeDtypeStruct((b, n_q_heads, seq, HEAD_DIM), MXU_DTYPE),
                   jax.ShapeDtypeStruct((b, n_kv_heads, HEAD_DIM, seq), MXU_DTYPE),
                   jax.ShapeDtypeStruct((b, n_kv_heads, seq, kw), MXU_DTYPE)],
        compiler_params=_cparams("parallel", "parallel"), name="qk_prep",
    )(proj3, proj3, proj3, q_gain_row, k_gain_row, gq, gk, cos2, sin2)


def _flash_body(q_ref, kt_ref, v_ref, o_ref, *, seq, tq, tk, group):
    hd = HEAD_DIM
    rows = group * tq
    q = q_ref[...].reshape(rows, hd)
    m = jnp.full((rows, 1), NEG_INF, F32)
    acc = jnp.zeros((rows, v_ref.shape[1]), F32)
    for kc in range(seq // tk):
        s = jnp.dot(q, kt_ref[:, kc * tk:(kc + 1) * tk], preferred_element_type=F32)
        m_new = jnp.maximum(m, jnp.max(s, axis=-1, keepdims=True))
        p = jnp.exp2(s - m_new)
        acc = jnp.exp2(m - m_new) * acc + jnp.dot(p.astype(v_ref.dtype), v_ref[kc * tk:(kc + 1) * tk, :],
                                                  preferred_element_type=F32)
        m = m_new
    o = acc[:, :hd] / acc[:, hd:hd + 1]
    o_ref[...] = jnp.concatenate([o[g * tq:(g + 1) * tq] for g in range(group)], axis=1).astype(o_ref.dtype)


def _flash_attention(q4, kt4, v4, tq, tk):
    b, n_q_heads, seq, hd = q4.shape
    n_kv_heads = kt4.shape[1]
    group = n_q_heads // n_kv_heads
    return pl.pallas_call(
        functools.partial(_flash_body, seq=seq, tq=tq, tk=tk, group=group), grid=(b, n_kv_heads, seq // tq),
        in_specs=[pl.BlockSpec((None, group, tq, hd), lambda bi, h, i: (bi, h, i, 0)),
                  pl.BlockSpec((None, None, hd, seq), lambda bi, h, i: (bi, h, 0, 0)),
                  pl.BlockSpec((None, None, seq, v4.shape[3]), lambda bi, h, i: (bi, h, 0, 0))],
        out_specs=pl.BlockSpec((None, tq, group * hd), lambda bi, h, i: (bi, i, h)),
        out_shape=jax.ShapeDtypeStruct((b, seq, n_q_heads * hd), MXU_DTYPE),
        compiler_params=_cparams("parallel", "parallel", "parallel"), name="flash_gqa",
    )(q4, kt4, v4)


def _kvproj_body(m_ref, w_ref, o_ref):
    o_ref[...] = _mm(m_ref[...], w_ref[...]).astype(o_ref.dtype)


def _kvproj(mem2d, wkv, tm):
    t, d = mem2d.shape
    n = wkv.shape[1]
    return pl.pallas_call(
        _kvproj_body, grid=(t // tm,),
        in_specs=[pl.BlockSpec((tm, d), lambda i: (i, 0)), pl.BlockSpec((d, n), lambda i: (0, 0))],
        out_specs=pl.BlockSpec((tm, n), lambda i: (i, 0)), out_shape=jax.ShapeDtypeStruct((t, n), MXU_DTYPE),
        compiler_params=_cparams("parallel"), name="mem_kv_proj",
    )(mem2d, wkv)


def _crossattn_body(x_ref, yp_ref, ym_ref, ya_ref, wp_ref, wm_ref, wa_ref, kv_ref, gm_ref, wq_ref, wo_ref, gf_ref,
                    rwt_ref, xo_ref, hf_ref, lg_ref, *, ts):
    x = (x_ref[...] + _mm(yp_ref[...], wp_ref[...]) + _mm(ym_ref[...], wm_ref[...])
         + _mm(ya_ref[...], wa_ref[...]))
    q = _mm(_rms(x, gm_ref[...]), wq_ref[...])
    ca_width = q.shape[1]
    dh = ca_width // CA_HEADS
    kv = kv_ref[...]
    hs = range(CA_HEADS)
    s = [_mm_nt(q[:, h * dh:(h + 1) * dh], kv[:, h * dh:(h + 1) * dh]) * (dh ** -0.5) for h in hs]
    e = [jnp.exp(s[h] - jnp.max(s[h], axis=-1, keepdims=True)) for h in hs]
    p = [e[h] / jnp.sum(e[h], axis=-1, keepdims=True) for h in hs]
    heads = [_mm(p[h], kv[:, ca_width + h * dh:ca_width + (h + 1) * dh]) for h in hs]
    x2 = x + _mm(jnp.concatenate(heads, axis=1), wo_ref[...])
    xo_ref[...] = x2
    hf = _rms(x2, gf_ref[...])
    _to_token_tiles(hf_ref, hf, ts)
    lg_ref[...] = _mm_nt(rwt_ref[...], hf)


def _crossattn(x3, yp, ym, ya, w_out, kv3, g_mem, wq, wo, g_ffn, router_wt, ts):
    b, seq, d = x3.shape
    n_mem, kvw = kv3.shape[1], kv3.shape[2]
    ne = router_wt.shape[0]
    fixed = lambda bi, i: (0, 0)
    tile = lambda w: pl.BlockSpec((None, ts, w), lambda bi, i: (bi, i, 0))
    wp, wm, wa = yp.shape[2], ym.shape[2], ya.shape[2]
    return pl.pallas_call(
        functools.partial(_crossattn_body, ts=ts), grid=(b, seq // ts),
        in_specs=[tile(d), tile(wp), tile(wm), tile(wa),
                  pl.BlockSpec((wp, d), fixed), pl.BlockSpec((wm, d), fixed), pl.BlockSpec((wa, d), fixed),
                  pl.BlockSpec((None, n_mem, kvw), lambda bi, i: (bi, 0, 0)),
                  pl.BlockSpec((1, d), fixed), pl.BlockSpec(wq.shape, fixed), pl.BlockSpec(wo.shape, fixed),
                  pl.BlockSpec((1, d), fixed), pl.BlockSpec((ne, d), fixed)],
        out_specs=[pl.BlockSpec((None, ts, d), lambda bi, i: (bi, i, 0)),
                   pl.BlockSpec((None, ts * SUBLANES, LANES), lambda bi, i: (bi, i, 0)),
                   pl.BlockSpec((None, ne, ts), lambda bi, i: (bi, 0, i))],
        out_shape=[jax.ShapeDtypeStruct((b, seq, d), F32),
                   jax.ShapeDtypeStruct((b, seq * SUBLANES, LANES), F32),
                   jax.ShapeDtypeStruct((b, ne, seq), F32)],
        compiler_params=_cparams("parallel", "parallel"), name="cross_attn",
    )(x3, yp, ym, ya, w_out[:wp], w_out[wp:wp + wm], w_out[wp + wm:], kv3, g_mem, wq, wo, g_ffn, router_wt)


def _excl_prefix(flags, upper_excl, ones):
    rows, seq = flags.shape
    off = jnp.zeros((rows, LANES), F32)
    out = []
    for c in range(seq // LANES):
        xc = flags[:, c * LANES:(c + 1) * LANES].astype(MXU_DTYPE)
        out.append(jnp.dot(xc, upper_excl, preferred_element_type=F32) + off)
        off = off + jnp.dot(xc, ones, preferred_element_type=F32)
    return jnp.concatenate(out, axis=1)


def _router_body(lg_ref, idx_ref, gate_ref, *, seq, cap):
    lg = lg_ref[...]
    ne = lg.shape[0]
    ex = jnp.exp(lg - jnp.max(lg, axis=0, keepdims=True))
    aff = ex / jnp.sum(ex, axis=0, keepdims=True)
    bits = pltpu.bitcast(aff, I32)

    thr = jnp.zeros((ne, 1), I32)
    for bit in range(30, -1, -1):
        cand = thr | (1 << bit)
        cnt = jnp.sum((bits >= cand).astype(F32), axis=1, keepdims=True)
        thr = jnp.where(cnt >= cap, cand, thr)

    r_i = lax.broadcasted_iota(I32, (LANES, LANES), 0)
    c_i = lax.broadcasted_iota(I32, (LANES, LANES), 1)
    upper_excl = (r_i < c_i).astype(MXU_DTYPE)
    ones = jnp.ones((LANES, LANES), MXU_DTYPE)

    gt = bits > thr
    eq = bits == thr
    n_gt = jnp.sum(gt.astype(F32), axis=1, keepdims=True)
    tie_rank = _excl_prefix(eq.astype(F32), upper_excl, ones)
    sel = gt | (eq & (tie_rank < cap - n_gt))
    pos = _excl_prefix(sel.astype(F32), upper_excl, ones).astype(I32)

    n_iota = lax.broadcasted_iota(I32, (ne, seq), 1)
    seq_bits = (seq - 1).bit_length()
    flag_bit = 2 * seq_bits
    packed = jnp.where(sel, n_iota | ((n_iota - pos) << seq_bits) | (1 << flag_bit), 0)
    gbits = jnp.where(sel, bits, 0)
    for k in range(seq_bits):
        sh = 1 << k
        moved_p = pltpu.roll(packed, seq - sh, 1)
        moved_g = pltpu.roll(gbits, seq - sh, 1)
        take = ((moved_p >> flag_bit) & 1) * ((moved_p >> (seq_bits + k)) & 1) == 1
        stay = ((packed >> flag_bit) & 1) * (1 - ((packed >> (seq_bits + k)) & 1)) == 1
        packed = jnp.where(take, moved_p, jnp.where(stay, packed, 0))
        gbits = jnp.where(take, moved_g, jnp.where(stay, gbits, 0))
    idx_ref[...] = packed[:, :cap] & (seq - 1)
    gate_ref[...] = pltpu.bitcast(gbits[:, :cap], F32)


def _router(logits_t, cap):
    b, ne, seq = logits_t.shape
    return pl.pallas_call(
        functools.partial(_router_body, seq=seq, cap=cap), grid=(b,),
        in_specs=[pl.BlockSpec((None, ne, seq), lambda bi: (bi, 0, 0))],
        out_specs=[pl.BlockSpec((None, ne, cap), lambda bi: (bi, 0, 0)),
                   pl.BlockSpec((None, ne, cap), lambda bi: (bi, 0, 0))],
        out_shape=[jax.ShapeDtypeStruct((b, ne, cap), I32), jax.ShapeDtypeStruct((b, ne, cap), F32)],
        compiler_params=_cparams("parallel"), name="ec_router",
    )(logits_t)


def _moe_body(*refs, seq, cap, ne, n_fc, stride, b_off, cast, chained):
    idx_hbm, gate_hbm, h_hbm, wg_ref, wu_ref, wd_ref = refs[:6]
    prev_hbm = refs[6] if chained else None
    refs = refs[6 + (1 if chained else 0):]
    out_hbm = refs[0]
    w_outs = refs[1:4] if cast else ()
    idx_s, gate_s, xbuf, xbf, ybuf, yacc, acc, gsem, isem, osem, psem = refs[1 + len(w_outs):]

    def prev_copy():
        return pltpu.make_async_copy(prev_hbm, out_hbm.at[pl.ds(0, b_off)], psem)
    b = pl.program_id(0)
    e = pl.program_id(1)
    c = pl.program_id(2)
    last = pl.num_programs(0) * ne - 1
    step = b * ne + e
    nstep = jnp.minimum(step + 1, last)
    xslot, nxslot = lax.rem(step, 2), lax.rem(step + 1, 2)
    islot, nislot = lax.rem(step, N_IDX_SLOTS), lax.rem(step + 1, N_IDX_SLOTS)
    tile = SUBLANES
    per = cap // n_fc

    def idx_copies(s, sl):
        src = pl.ds(pl.multiple_of((b_off * ne + s) * stride, stride), stride)
        dst = pl.ds(pl.multiple_of(sl * stride, stride), stride)
        return (pltpu.make_async_copy(idx_hbm.at[src], idx_s.at[dst], isem.at[0, sl]),
                pltpu.make_async_copy(gate_hbm.at[src], gate_s.at[dst], isem.at[1, sl]))

    def row_copy(src_row, k, sl):
        return pltpu.make_async_copy(h_hbm.at[pl.ds(src_row, tile), :],
                                     xbuf.at[sl, pl.ds(k * tile, tile), :], gsem.at[sl])

    def start_rows(s, isl, sl, k0, n):
        base = (b_off + s // ne) * seq
        ibase = isl * stride + k0
        for k in range(n):
            row_copy(pl.multiple_of((base + idx_s[ibase + k]) * tile, tile), k0 + k, sl).start(priority=GATHER_DMA_PRIORITY)

    def wait_gather(sl):
        pltpu.make_async_copy(h_hbm.at[pl.ds(0, cap * tile), :], xbuf.at[sl], gsem.at[sl]).wait()

    @pl.when(c == 0)
    def _():
        @pl.when(step == 0)
        def _():
            for cp in idx_copies(0, 0):
                cp.start()
            for cp in idx_copies(0, 0):
                cp.wait()
            for cp in idx_copies(jnp.minimum(1, last), 1):
                cp.start()
            if chained:
                prev_copy().start()

            def body(j, carry):
                start_rows(0, 0, 0, j * SUBLANES, SUBLANES)
                return carry
            lax.fori_loop(0, cap // SUBLANES, body, 0)

        for cp in idx_copies(0, nislot):
            cp.wait()
        for cp in idx_copies(jnp.minimum(step + 2, last), lax.rem(step + 2, N_IDX_SLOTS)):
            cp.start()
        wait_gather(xslot)
        xbf[...] = _from_token_tiles(xbuf.at[xslot], cap).astype(xbf.dtype)

        @pl.when(e == 0)
        def _():
            acc[...] = jnp.zeros(acc.shape, F32)

    start_rows(nstep, nislot, nxslot, pl.multiple_of(c * per, per), per)

    x = xbf[...]
    fc = wg_ref.shape[1]
    sub = min(fc, MOE_SUB_COLS)
    y = None
    for j in range(fc // sub):
        wg, wu, wd = wg_ref[:, j * sub:(j + 1) * sub], wu_ref[:, j * sub:(j + 1) * sub], wd_ref[j * sub:(j + 1) * sub, :]
        if cast:
            wg, wu, wd = wg.astype(MXU_DTYPE), wu.astype(MXU_DTYPE), wd.astype(MXU_DTYPE)
            w_outs[0][:, j * sub:(j + 1) * sub] = wg
            w_outs[1][:, j * sub:(j + 1) * sub] = wu
            w_outs[2][j * sub:(j + 1) * sub, :] = wd
        g = jnp.dot(x, wg, preferred_element_type=F32)
        u = jnp.dot(x, wu, preferred_element_type=F32)
        yj = _mm(g * jax.nn.sigmoid(g) * u, wd)
        y = yj if y is None else y + yj

    if n_fc == 1:
        _to_token_tiles(ybuf, y, cap)
    else:
        @pl.when(c == 0)
        def _():
            yacc[...] = y

        @pl.when(c > 0)
        def _():
            yacc[...] += y

    @pl.when(c == n_fc - 1)
    def _():
        if n_fc > 1:
            _to_token_tiles(ybuf, yacc[...], cap)

        ibase = islot * stride
        for k0 in range(0, cap, SCATTER_GROUP):
            rows = [pl.multiple_of(idx_s[ibase + k0 + j] * tile, tile) for j in range(SCATTER_GROUP)]
            new = [acc[pl.ds(rows[j], tile), :] + gate_s[ibase + k0 + j] * ybuf[(k0 + j) * tile:(k0 + j + 1) * tile, :]
                   for j in range(SCATTER_GROUP)]
            for j in range(SCATTER_GROUP):
                acc[pl.ds(rows[j], tile), :] = new[j]

        @pl.when(e == ne - 1)
        def _():
            cp = pltpu.make_async_copy(acc, out_hbm.at[b_off + b], osem)
            cp.start()
            cp.wait()

        @pl.when(step == last)
        def _():
            wait_gather(nxslot)
            for cp in idx_copies(0, lax.rem(step + 2, N_IDX_SLOTS)):
                cp.wait()
            if chained:
                prev_copy().wait()


def _moe(idx, gate, h_tiles, w_gate, w_up, w_down, up_block_off, seq, fc, b_off, n_b, prev_out, cast, layer=None):
    _, ne, cap = idx.shape
    d, f = w_down.shape[-1], w_down.shape[-2]
    n_fc = f // fc
    lead = () if layer is None else (layer,)
    w_spec = lambda shape, imap: pl.BlockSpec((None,) * (len(lead) + 1) + shape,
                                               lambda b, e, c: lead + imap(e, c))
    chained = prev_out is not None
    stride = -(-cap // SMEM_1D_TILE) * SMEM_1D_TILE
    flat = lambda a: jnp.pad(a, ((0, 0), (0, 0), (0, stride - cap))).reshape(-1)
    any_spec = pl.BlockSpec(memory_space=pl.ANY)
    assert chained == (b_off > 0)
    out_shape = [jax.ShapeDtypeStruct((b_off + n_b, seq * SUBLANES, LANES), F32)]
    out_specs = [any_spec]
    if cast:
        out_shape += [jax.ShapeDtypeStruct((ne, d, f), MXU_DTYPE), jax.ShapeDtypeStruct((ne, d, f), MXU_DTYPE),
                      jax.ShapeDtypeStruct((ne, f, d), MXU_DTYPE)]
        out_specs += [pl.BlockSpec((None, d, fc), lambda b, e, c: (e, 0, c)),
                      pl.BlockSpec((None, d, fc), lambda b, e, c: (e, 0, c)),
                      pl.BlockSpec((None, fc, d), lambda b, e, c: (e, c, 0))]
    return pl.pallas_call(
        functools.partial(_moe_body, seq=seq, cap=cap, ne=ne, n_fc=n_fc, stride=stride, b_off=b_off, cast=cast,
                          chained=chained),
        grid=(n_b, ne, n_fc),
        in_specs=[any_spec, any_spec, any_spec,
                  w_spec((d, fc), lambda e, c: (e, 0, c)),
                  w_spec((d, fc), lambda e, c: (e, 0, up_block_off + c)),
                  w_spec((fc, d), lambda e, c: (e, c, 0))] + ([any_spec] if chained else []),
        out_specs=out_specs, out_shape=out_shape,
        scratch_shapes=[pltpu.SMEM((N_IDX_SLOTS * stride,), I32), pltpu.SMEM((N_IDX_SLOTS * stride,), F32),
                        pltpu.VMEM((2, cap * SUBLANES, LANES), F32), pltpu.VMEM((cap, d), MXU_DTYPE),
                        pltpu.VMEM((cap * SUBLANES, LANES), F32), pltpu.VMEM((cap, d), F32),
                        pltpu.VMEM((seq * SUBLANES, LANES), F32),
                        pltpu.SemaphoreType.DMA((2,)), pltpu.SemaphoreType.DMA((2, N_IDX_SLOTS)),
                        pltpu.SemaphoreType.DMA, pltpu.SemaphoreType.DMA],
        compiler_params=_cparams("arbitrary", "arbitrary", "arbitrary"),
        name="ec_moe_cast" if cast else "ec_moe",
    )(flat(idx), flat(gate), h_tiles, w_gate, w_up, w_down, *([prev_out] if chained else []))


def _final_body(x_ref, moe_ref, g_ref, o_ref, *, tm):
    o_ref[...] = _rms(x_ref[...] + _from_token_tiles(moe_ref, tm), g_ref[...])


def _final_norm(x2d, moe_tiles, g, tm):
    t, d = x2d.shape
    row = lambda i: (i, 0)
    return pl.pallas_call(
        functools.partial(_final_body, tm=tm), grid=(t // tm,),
        in_specs=[pl.BlockSpec((tm, d), row), pl.BlockSpec((tm * SUBLANES, LANES), row),
                  pl.BlockSpec((1, d), lambda i: (0, 0))],
        out_specs=pl.BlockSpec((tm, d), row), out_shape=jax.ShapeDtypeStruct((t, d), F32),
        compiler_params=_cparams("parallel"), name="final_norm",
    )(x2d, moe_tiles, g)


def _rope_tables(seq):
    rows = seq // GRID_W
    row_ids = jnp.repeat(jnp.arange(rows), GRID_W).astype(F32)
    col_ids = jnp.tile(jnp.arange(GRID_W), rows).astype(F32)
    n_freq = HEAD_DIM // 4
    inv_freq = ROPE_THETA ** (-jnp.arange(n_freq, dtype=F32) / n_freq)
    ang = jnp.concatenate([row_ids[:, None] * inv_freq, col_ids[:, None] * inv_freq], axis=-1)
    cos, sin = jnp.cos(ang), jnp.sin(ang)
    cos_i = jnp.repeat(cos, 2, axis=1)
    sin_i = jnp.stack([-sin, sin], axis=-1).reshape(seq, HEAD_DIM)
    return jnp.tile(cos_i, (1, LANES // HEAD_DIM)), jnp.tile(sin_i, (1, LANES // HEAD_DIM))


def _group_matrix(width):
    gid = jnp.arange(width) // HEAD_DIM
    return (gid[:, None] == gid[None, :]).astype(MXU_DTYPE)


def _block_diag(w):
    g, c, _ = w.shape
    eye = jnp.eye(g, dtype=w.dtype)
    return (eye[:, None, :, None] * w[:, :, None, :]).reshape(g * c, g * c)


def _pick_tile(n, pref):
    t = min(n, pref)
    while n % t:
        t //= 2
    return t


def kernel(x, mem, norm_mix_g, w_in, pool_w, pool_scale, mlstm_conv_w, mlstm_gate_b, mlstm_norm_g, q_norm_g,
           k_norm_g, w_out, norm_mem_g, ca_wq, ca_wkv, ca_wo, norm_ffn_g, router_w, expert_w_gu, expert_w_down,
           final_norm_g):
    batch, seq, d = x.shape
    depth = w_in.shape[0]
    n_mem = mem.shape[1]
    pool_width = pool_scale.shape[1]
    m_width = mlstm_norm_g.shape[1]
    m_heads = m_width // HEAD_DIM
    n_gate = 2 * m_heads
    kv_width = (w_in.shape[2] - pool_width - 4 * m_width - 2 * n_gate) // 6
    q_width = 4 * kv_width
    n_q_heads, n_kv_heads = q_width // HEAD_DIM, kv_width // HEAD_DIM
    ne = router_w.shape[2]
    cap = EC_CAPACITY_FACTOR * seq // ne
    f_exp = expert_w_down.shape[2]
    t = batch * seq

    o_pool = 0
    o_mq = o_pool + pool_width
    o_mi = o_mq + 4 * m_width
    o_aq = o_mi + 2 * n_gate
    o_ak = o_aq + q_width
    c_pool = q_width // pool_width
    c_mq = (q_width + pool_width) // LANES
    m_cols = tuple(c_mq + j * (m_width // LANES) for j in range(4))
    c_ak = (q_width + pool_width + 4 * m_width) // kv_width
    cos2, sin2 = _rope_tables(seq)
    gq, gk = _group_matrix(q_width), _group_matrix(kv_width)
    n_pairs = m_heads // 2
    gate_cols = [[kind * n_gate + dr * m_heads + 2 * p + hh for kind in range(2) for dr in range(2) for hh in range(2)]
                 for p in range(n_pairs)]

    tm = _pick_tile(t, 512)
    ts = _pick_tile(seq, 512)
    x2d = x.reshape(t, d)
    mem2d = mem.reshape(batch * n_mem, d)
    moe_tiles = None
    for layer in range(depth):
        wl = w_in[layer]
        w_main = jnp.concatenate([wl[:, o_aq:o_ak], wl[:, o_pool:o_mi], wl[:, o_ak:]], axis=1).astype(MXU_DTYPE)
        w_gate = jnp.concatenate(
            [jnp.pad(wl[:, o_mi:o_aq][:, jnp.array(cols)], ((0, 0), (0, LANES - len(cols)))) for cols in gate_cols],
            axis=1).astype(MXU_DTYPE)
        gb_row = jnp.concatenate(
            [jnp.pad(mlstm_gate_b[layer][jnp.array(cols)], (0, LANES - len(cols))) for cols in gate_cols])[None]
        x2d, proj, gates = _inproj(x2d, moe_tiles, norm_mix_g[layer][None], w_main, w_gate, tm)
        proj3 = proj.reshape(batch, seq, proj.shape[1])
        gates3 = gates.reshape(batch, seq, gates.shape[1])

        y_pool = _pool_mixer(proj3, _block_diag(pool_w[layer]).astype(MXU_DTYPE), pool_scale[layer][None], c_pool,
                             _pick_tile(seq, 256))
        y_mlstm = _mlstm(proj3, gates3, mlstm_conv_w[layer], gb_row, mlstm_norm_g[layer][None], m_heads, m_cols)
        q4, kt4, v4 = _qkprep(proj3, jnp.tile(q_norm_g[layer], n_q_heads)[None],
                              jnp.tile(k_norm_g[layer], n_kv_heads)[None], gq, gk, cos2, sin2,
                              n_q_heads, n_kv_heads, (0, c_ak, c_ak + 1), ts)
        y_attn = _flash_attention(q4, kt4, v4, _pick_tile(seq, 512), _pick_tile(seq, 512))

        kv = _kvproj(mem2d, ca_wkv[layer].astype(MXU_DTYPE), _pick_tile(batch * n_mem, 512))
        x3, h_tiles, logits_t = _crossattn(
            x2d.reshape(batch, seq, d), y_pool, y_mlstm, y_attn, w_out[layer].astype(MXU_DTYPE),
            kv.reshape(batch, n_mem, kv.shape[1]), norm_mem_g[layer][None],
            ca_wq[layer].astype(MXU_DTYPE), ca_wo[layer].astype(MXU_DTYPE), norm_ffn_g[layer][None],
            router_w[layer].T.astype(MXU_DTYPE), ts)
        x2d = x3.reshape(t, d)

        idx, gate = _router(logits_t, cap)
        h_flat = h_tiles.reshape(t * SUBLANES, LANES)
        fc_cast = _pick_tile(f_exp, 512)
        moe_tiles, wg_c, wu_c, wd_c = _moe(idx, gate, h_flat, expert_w_gu, expert_w_gu, expert_w_down,
                                           f_exp // fc_cast, seq, fc_cast, 0, 1, None, True, layer)
        if batch > 1:
            moe_tiles, = _moe(idx, gate, h_flat, wg_c, wu_c, wd_c, 0, seq, _pick_tile(f_exp, 1024), 1, batch - 1,
                              moe_tiles, False)
        moe_tiles = moe_tiles.reshape(t * SUBLANES, LANES)
    out = _final_norm(x2d, moe_tiles, final_norm_g[None], tm)
    return out.reshape(batch, seq, d)
```

```python
import functools
import math

import jax
import jax.numpy as jnp
from jax import lax
from jax.experimental import pallas as pl
from jax.experimental.pallas import tpu as pltpu

F32 = jnp.float32
I32 = jnp.int32
MXU_DTYPE = jnp.bfloat16

LANES = 128
SUBLANES = 8
VMEM_LIMIT_BYTES = 56 * 1024 * 1024

TOKEN_TILE = 1024
POOL_TILE = 256
FLASH_Q_TILE = 512
FLASH_KV_TILE = 512
MOE_F_CHUNK = 1024
MOE_F_CHUNK_CAST = 512

EPS = 1e-6
GRID_W = 64
HEAD_DIM = 64
POOL_WINDOWS = (2, 4, 8, 16)
POOL_HALO = 8
MLSTM_CHUNK = 128
MLSTM_CONV = 5
CONV_HALO = 8
ROPE_THETA = 10000.0
CA_HEADS = 4
N_EXPERTS = 16
EC_CAPACITY_FACTOR = 2
SMEM_1D_TILE = 1024
N_IDX_SLOTS = 3
MOE_SUB_COLS = 512
GATHER_DMA_PRIORITY = 1
SCATTER_GROUP = 16
NEG_INF = float("-inf")
LOG2_E = math.log2(math.e)


def _cparams(*sem):
    return pltpu.CompilerParams(dimension_semantics=sem, vmem_limit_bytes=VMEM_LIMIT_BYTES)


def _mm(a, b):
    return jnp.dot(a.astype(MXU_DTYPE), b.astype(MXU_DTYPE), preferred_element_type=F32)


def _mm_nt(a, b):
    return lax.dot_general(a.astype(MXU_DTYPE), b.astype(MXU_DTYPE), (((1,), (1,)), ((), ())),
                           preferred_element_type=F32)


def _split2(x):
    hi = x.astype(MXU_DTYPE)
    return hi, (x - hi.astype(F32)).astype(MXU_DTYPE)


def _rms(x, g):
    return x * lax.rsqrt(jnp.mean(x * x, axis=-1, keepdims=True) + EPS) * g


def _from_token_tiles(ref, rows):
    return jnp.concatenate([ref[pl.ds(s, rows, stride=SUBLANES), :] for s in range(SUBLANES)], axis=1)


def _to_token_tiles(ref, val, rows):
    for s in range(SUBLANES):
        ref[pl.ds(s, rows, stride=SUBLANES), :] = val[:, s * LANES:(s + 1) * LANES]


def _inproj_body(x_ref, g_ref, w_ref, wg_ref, o_ref, og_ref):
    hb = _rms(x_ref[...], g_ref[...]).astype(MXU_DTYPE)
    o_ref[...] = jnp.dot(hb, w_ref[...], preferred_element_type=F32).astype(o_ref.dtype)
    og_ref[...] = jnp.dot(hb, wg_ref[...], preferred_element_type=F32)


def _inproj_merge_body(x_ref, moe_ref, g_ref, w_ref, wg_ref, xo_ref, o_ref, og_ref, *, tm):
    x = x_ref[...] + _from_token_tiles(moe_ref, tm)
    xo_ref[...] = x
    hb = _rms(x, g_ref[...]).astype(MXU_DTYPE)
    o_ref[...] = jnp.dot(hb, w_ref[...], preferred_element_type=F32).astype(o_ref.dtype)
    og_ref[...] = jnp.dot(hb, wg_ref[...], preferred_element_type=F32)


def _inproj(x2d, moe_tiles, g, w_main, w_gate, tm):
    t, d = x2d.shape
    n = w_main.shape[1]
    grid = (t // tm,)
    row = lambda i: (i, 0)
    fixed = lambda i: (0, 0)
    gw = w_gate.shape[1]
    w_specs = [pl.BlockSpec((1, d), fixed), pl.BlockSpec((d, n), fixed), pl.BlockSpec((d, gw), fixed)]
    o_specs = [pl.BlockSpec((tm, n), row), pl.BlockSpec((tm, gw), row)]
    o_shapes = [jax.ShapeDtypeStruct((t, n), MXU_DTYPE), jax.ShapeDtypeStruct((t, gw), F32)]
    if moe_tiles is None:
        proj, gates = pl.pallas_call(
            _inproj_body, grid=grid, in_specs=[pl.BlockSpec((tm, d), row)] + w_specs, out_specs=o_specs,
            out_shape=o_shapes, compiler_params=_cparams("parallel"), name="inproj",
        )(x2d, g, w_main, w_gate)
        return x2d, proj, gates
    xo, proj, gates = pl.pallas_call(
        functools.partial(_inproj_merge_body, tm=tm), grid=grid,
        in_specs=[pl.BlockSpec((tm, d), row), pl.BlockSpec((tm * SUBLANES, LANES), row)] + w_specs,
        out_specs=[pl.BlockSpec((tm, d), row)] + o_specs,
        out_shape=[jax.ShapeDtypeStruct((t, d), F32)] + o_shapes,
        compiler_params=_cparams("parallel"), name="inproj_merge",
    )(x2d, moe_tiles, g, w_main, w_gate)
    return xo, proj, gates


def _pool_body(u_ref, w_ref, sc_ref, o_ref, upad_ref, *, seq, tc):
    i = pl.program_id(1)

    @pl.when(i == 0)
    def _():
        zeros = jnp.zeros((POOL_HALO, LANES), F32)
        for hf in range(2):
            upad_ref[hf, 0:POOL_HALO, :] = zeros
            upad_ref[hf, POOL_HALO + seq:POOL_HALO + seq + POOL_HALO, :] = zeros
            upad_ref[hf, POOL_HALO:POOL_HALO + seq, :] = u_ref[:, hf * LANES:(hf + 1) * LANES].astype(F32)

    t0 = pl.multiple_of(i * tc, tc)
    t = t0 + lax.broadcasted_iota(I32, (tc, LANES), 0)
    lane = lax.broadcasted_iota(I32, (tc, LANES), 1)
    upper = lane >= (LANES // 2)
    halves = []
    for hf in range(2):
        w_lo, w_hi = POOL_WINDOWS[2 * hf], POOL_WINDOWS[2 * hf + 1]
        acc = jnp.zeros((tc, LANES), F32)
        for j in range(-(w_hi // 2), w_hi // 2):
            piece = upad_ref[hf, pl.ds(t0 + (POOL_HALO + j), tc), :]
            if -(w_lo // 2) <= j < w_lo // 2:
                acc = acc + piece
            else:
                acc = acc + jnp.where(upper, piece, 0.0)
        half_w = jnp.where(upper, w_hi // 2, w_lo // 2)
        cnt = jnp.minimum(t + half_w, seq) - jnp.maximum(t - half_w, 0)
        centre = upad_ref[hf, pl.ds(t0 + POOL_HALO, tc), :]
        halves.append(acc / cnt.astype(F32) - centre)
    d = jnp.concatenate(halves, axis=1)
    o_ref[...] = (_mm(d, w_ref[...]) * sc_ref[...]).astype(o_ref.dtype)


def _pool_mixer(proj3, w_bd, scale, col_block, tc):
    b, seq, _ = proj3.shape
    pw = w_bd.shape[0]
    return pl.pallas_call(
        functools.partial(_pool_body, seq=seq, tc=tc), grid=(b, seq // tc),
        in_specs=[pl.BlockSpec((None, seq, pw), lambda bi, i: (bi, 0, col_block)),
                  pl.BlockSpec((pw, pw), lambda bi, i: (0, 0)),
                  pl.BlockSpec((1, pw), lambda bi, i: (0, 0))],
        out_specs=pl.BlockSpec((None, tc, pw), lambda bi, i: (bi, i, 0)),
        out_shape=jax.ShapeDtypeStruct((b, seq, pw), MXU_DTYPE),
        scratch_shapes=[pltpu.VMEM((pw // LANES, seq + 2 * POOL_HALO, LANES), F32)],
        compiler_params=_cparams("parallel", "arbitrary"), name="pool_mixer",
    )(proj3, w_bd, scale)


def _mlstm_body(q_ref, k_ref, v_ref, o_ref, gt_ref, cwq_ref, cwk_ref, gb_ref, ng_ref, y_ref,
                qpad_ref, kpad_ref, qc_ref, kc_ref, kt_ref, gtt_ref, hf_ref, hb_ref, cn_ref, m_ref,
                *, seq, n_heads):
    L = MLSTM_CHUNK
    nc = seq // L
    hd = HEAD_DIM
    gbias = gb_ref[...]

    zeros = jnp.zeros((CONV_HALO, LANES), F32)
    for pad_ref, src_ref in ((qpad_ref, q_ref), (kpad_ref, k_ref)):
        pad_ref[0:CONV_HALO, :] = zeros
        pad_ref[CONV_HALO + seq:CONV_HALO + seq + CONV_HALO, :] = zeros
        pad_ref[CONV_HALO:CONV_HALO + seq, :] = src_ref[...].astype(F32)

    def conv_chunk(ci, carry):
        t0 = pl.multiple_of(ci * L, L)
        for pad_ref, cw_ref, dst_ref, post in ((qpad_ref, cwq_ref, qc_ref, 1.0),
                                               (kpad_ref, cwk_ref, kc_ref, 1.0 / math.sqrt(hd))):
            acc = jnp.zeros((L, LANES), F32)
            for j in range(MLSTM_CONV):
                off = j - MLSTM_CONV // 2
                piece = pad_ref[pl.ds(t0 + (CONV_HALO + off), L), :]
                acc = acc + piece * cw_ref[j:j + 1, :]
            act = acc * jax.nn.sigmoid(acc)
            act = act * post if post != 1.0 else act
            dst_ref[pl.ds(t0, L), :] = act
            if dst_ref is kc_ref:
                kt_ref[:, pl.ds(t0, L)] = act.T
        gtt_ref[:, pl.ds(t0, L)] = (gt_ref[pl.ds(t0, L), :] + gbias).T
        return carry

    lax.fori_loop(0, nc, conv_chunk, 0)

    cn_ref[...] = jnp.zeros(cn_ref.shape, F32)
    m_ref[...] = jnp.zeros(m_ref.shape, F32)

    row = lax.broadcasted_iota(I32, (L, L), 0)
    col = lax.broadcasted_iota(I32, (L, L), 1)
    tri = ((col <= row).astype(MXU_DTYPE), (col >= row).astype(MXU_DTYPE))
    tri_t = (tri[1], tri[0])
    keep = (col <= row, col >= row)
    gate_lane = lax.broadcasted_iota(I32, (L, LANES), 1)
    gate_sub = lax.broadcasted_iota(I32, (LANES, L), 0)
    is_i, is_f = gate_lane < 4, (gate_lane >= 4) & (gate_lane < 8)
    is_i_t, is_f_t = gate_sub < 4, (gate_sub >= 4) & (gate_sub < 8)
    head_lanes = (gate_lane < hd, gate_lane >= hd)
    sel_r = lax.broadcasted_iota(I32, (LANES, 2 * LANES), 0)
    sel_c = lax.broadcasted_iota(I32, (LANES, 2 * LANES), 1)
    sel = [[(sel_r == jnp.where(sel_c < LANES, 4 + d * 2 + hh, d * 2 + hh)).astype(MXU_DTYPE) for hh in range(2)]
           for d in range(2)]
    ones_ext = jnp.ones((L, LANES), MXU_DTYPE)

    def cumulative(lf, left, d):
        hi, lo = _split2(lf)
        mat = tri[d] if left else tri_t[d]
        mm = (lambda x: jnp.dot(mat, x, preferred_element_type=F32)) if left else (
            lambda x: jnp.dot(x, mat, preferred_element_type=F32))
        return mm(hi) + mm(lo)

    chains = [(d, hh) for d in range(2) for hh in range(2)]
    f32dot = functools.partial(jnp.dot, preferred_element_type=F32)

    def chunk_step(it, carry):
        t0s = [pl.multiple_of((it if d == 0 else nc - 1 - it) * L, L) for d in range(2)]
        gpre = [gt_ref[pl.ds(t0, L), :] + gbias for t0 in t0s]
        gpre_t = [gtt_ref[:, pl.ds(t0, L)] for t0 in t0s]
        gm = [jnp.where(is_i, gpre[d], cumulative(jnp.where(is_f, jax.nn.log_sigmoid(gpre[d]), 0.0), True, d))
              for d in range(2)]
        gmt = [jnp.where(is_i_t, gpre_t[d], cumulative(jnp.where(is_f_t, jax.nn.log_sigmoid(gpre_t[d]), 0.0), False, d))
               for d in range(2)]
        gm2 = [_split2(g) for g in gm]
        qc = [qc_ref[pl.ds(t0, L), :] for t0 in t0s]
        kc = [kc_ref[pl.ds(t0, L), :].astype(MXU_DTYPE) for t0 in t0s]
        kt = [kt_ref[:, pl.ds(t0, L)].astype(MXU_DTYPE) for t0 in t0s]
        vc = [v_ref[pl.ds(t0, L), :] for t0 in t0s]
        v_ext = [jnp.concatenate([v.astype(MXU_DTYPE), ones_ext], axis=1) for v in vc]

        rep = [f32dot(gm2[d][0], sel[d][hh]) + f32dot(gm2[d][1], sel[d][hh]) for d, hh in chains]
        b_rep = [r[:, :LANES] for r in rep]
        li_rep = [r[:, LANES:] for r in rep]
        li_row = [gmt[d][d * 2 + hh:d * 2 + hh + 1, :] for d, hh in chains]
        b_row = [gmt[d][4 + d * 2 + hh:5 + d * 2 + hh, :] for d, hh in chains]
        m_st = [m_ref[hh * 2 + d] for d, hh in chains]
        cn = [cn_ref[hh * 2 + d] for d, hh in chains]
        qm = [jnp.where(head_lanes[hh], qc[d], 0.0).astype(MXU_DTYPE) for d, hh in chains]
        s_qk = [lax.dot_general(qm[c], kc[d], (((1,), (1,)), ((), ())), preferred_element_type=F32)
                for c, (d, hh) in enumerate(chains)]
        qcn = [f32dot(qm[c], cn[c].astype(MXU_DTYPE)) for c in range(4)]

        dmat = [jnp.where(keep[d], b_rep[c] - b_row[c] + li_row[c], NEG_INF) for c, (d, hh) in enumerate(chains)]
        inter = [b_rep[c] + m_st[c] for c in range(4)]
        m_j = [jnp.maximum(inter[c], jnp.max(dmat[c], axis=-1, keepdims=True)) for c in range(4)]
        a_mat = [jnp.exp(dmat[c] - m_j[c]) * s_qk[c] for c in range(4)]
        a_int = [jnp.exp(inter[c] - m_j[c]) for c in range(4)]
        av = [f32dot(a_mat[c].astype(MXU_DTYPE), v_ext[d]) for c, (d, hh) in enumerate(chains)]
        num = [av[c][:, :LANES] + a_int[c] * qcn[c][:, :LANES] for c in range(4)]
        den = [av[c][:, LANES:] + a_int[c] * qcn[c][:, LANES:] for c in range(4)]
        h_out = [num[c] / jnp.maximum(jnp.abs(den[c]), jnp.exp(-m_j[c])) for c in range(4)]

        b_last = [b_rep[c][(L - 1 if d == 0 else 0):(L if d == 0 else 1), :] for c, (d, hh) in enumerate(chains)]
        g_row = [b_last[c] - b_row[c] + li_row[c] for c in range(4)]
        g_rep = [b_last[c] - b_rep[c] + li_rep[c] for c in range(4)]
        m_new = [jnp.maximum(b_last[c] + m_st[c], jnp.max(g_row[c], axis=-1, keepdims=True)) for c in range(4)]
        w_rep = [jnp.exp(g_rep[c] - m_new[c]) for c in range(4)]
        decay = [jnp.exp(b_last[c] + m_st[c] - m_new[c]) for c in range(4)]
        wv_ext = [jnp.concatenate([w_rep[c] * vc[d].astype(F32), w_rep[c]], axis=1).astype(MXU_DTYPE)
                  for c, (d, hh) in enumerate(chains)]
        upd = [f32dot(kt[d], wv_ext[c]) for c, (d, hh) in enumerate(chains)]
        for c, (d, hh) in enumerate(chains):
            cn_ref[hh * 2 + d] = jnp.concatenate([decay[c], decay[c]], axis=1) * cn[c] + upd[c]
            m_ref[hh * 2 + d] = m_new[c]
        hf_ref[pl.ds(t0s[0], L), :] = jnp.where(head_lanes[0], h_out[0], h_out[1])
        hb_ref[pl.ds(t0s[1], L), :] = jnp.where(head_lanes[0], h_out[2], h_out[3])
        return carry

    lax.fori_loop(0, nc, chunk_step, 0)

    ng = ng_ref[...]
    lane = lax.broadcasted_iota(I32, (L, LANES), 1)
    upper = lane >= hd

    def out_chunk(ci, carry):
        t0 = pl.multiple_of(ci * L, L)
        hs = hf_ref[pl.ds(t0, L), :] + hb_ref[pl.ds(t0, L), :]
        sq = hs * hs
        ms_lo = jnp.sum(jnp.where(upper, 0.0, sq), axis=-1, keepdims=True) / hd
        ms_hi = jnp.sum(jnp.where(upper, sq, 0.0), axis=-1, keepdims=True) / hd
        ms = jnp.where(upper, ms_hi, ms_lo)
        y = hs * lax.rsqrt(ms + EPS) * ng
        y_ref[pl.ds(t0, L), :] = (y * jax.nn.sigmoid(o_ref[pl.ds(t0, L), :].astype(F32))).astype(y_ref.dtype)
        return carry

    lax.fori_loop(0, nc, out_chunk, 0)


def _mlstm(proj3, gates3, conv_w, gate_bias_row, norm_g_row, n_heads, cols):
    b, seq, _ = proj3.shape
    qb, kb, vb, ob = cols
    n_pairs = n_heads // 2
    width = n_heads * HEAD_DIM
    blk = lambda base: pl.BlockSpec((None, seq, LANES), lambda bi, p, base=base: (bi, 0, base + p))
    k_conv_base = width // LANES
    return pl.pallas_call(
        functools.partial(_mlstm_body, seq=seq, n_heads=n_heads), grid=(b, n_pairs),
        in_specs=[blk(qb), blk(kb), blk(vb), blk(ob),
                  pl.BlockSpec((None, seq, LANES), lambda bi, p: (bi, 0, p)),
                  pl.BlockSpec((MLSTM_CONV, LANES), lambda bi, p: (0, p)),
                  pl.BlockSpec((MLSTM_CONV, LANES), lambda bi, p: (0, k_conv_base + p)),
                  pl.BlockSpec((1, LANES), lambda bi, p: (0, p)),
                  pl.BlockSpec((1, LANES), lambda bi, p: (0, p))],
        out_specs=pl.BlockSpec((None, seq, LANES), lambda bi, p: (bi, 0, p)),
        out_shape=jax.ShapeDtypeStruct((b, seq, width), MXU_DTYPE),
        scratch_shapes=[pltpu.VMEM((seq + 2 * CONV_HALO, LANES), F32), pltpu.VMEM((seq + 2 * CONV_HALO, LANES), F32),
                        pltpu.VMEM((seq, LANES), F32), pltpu.VMEM((seq, LANES), F32),
                        pltpu.VMEM((LANES, seq), F32), pltpu.VMEM((LANES, seq), F32),
                        pltpu.VMEM((seq, LANES), F32), pltpu.VMEM((seq, LANES), F32),
                        pltpu.VMEM((4, LANES, 2 * LANES), F32),
                        pltpu.VMEM((4, 1, LANES), F32)],
        compiler_params=_cparams("parallel", "parallel"), name="mlstm",
    )(proj3, proj3, proj3, proj3, gates3, conv_w, conv_w, gate_bias_row, norm_g_row)


def _group_mean_sq(x, gmat):
    hi, lo = _split2(x * x)
    ssq = jnp.dot(hi, gmat, preferred_element_type=F32) + jnp.dot(lo, gmat, preferred_element_type=F32)
    return ssq * (1.0 / HEAD_DIM)


def _norm_rope(x, g_row, gmat, cos_t, sin_t):
    width = x.shape[1]
    xn = x * lax.rsqrt(_group_mean_sq(x, gmat) + EPS) * g_row
    lane = lax.broadcasted_iota(I32, x.shape, 1)
    partner = jnp.where(lane % 2 == 0, pltpu.roll(xn, width - 1, 1), pltpu.roll(xn, 1, 1))
    return xn * cos_t + partner * sin_t


def _qkprep_body(q_ref, k_ref, v_ref, qg_ref, kg_ref, gq_ref, gk_ref, cos_ref, sin_ref, qo_ref, kt_ref, vo_ref,
                 *, n_q_heads, n_kv_heads):
    hd = HEAD_DIM
    cos2, sin2 = cos_ref[...], sin_ref[...]
    reps = n_q_heads * hd // LANES
    q = _norm_rope(q_ref[...].astype(F32), qg_ref[...], gq_ref[...],
                   jnp.concatenate([cos2] * reps, axis=1), jnp.concatenate([sin2] * reps, axis=1))
    q = q * (hd ** -0.5 * LOG2_E)
    for h in range(n_q_heads):
        qo_ref[h] = q[:, h * hd:(h + 1) * hd].astype(qo_ref.dtype)
    k = _norm_rope(k_ref[...].astype(F32), kg_ref[...], gk_ref[...], cos2, sin2)
    kt = k.T
    v = v_ref[...].astype(F32)
    kv_width = v.shape[1]
    lane = lax.broadcasted_iota(I32, v.shape, 1)
    for h in range(n_kv_heads):
        kt_ref[h] = kt[h * hd:(h + 1) * hd, :].astype(kt_ref.dtype)
        vh = v if h == 0 else pltpu.roll(v, kv_width - h * hd, 1)
        vo_ref[h] = jnp.where(lane < hd, vh, jnp.where(lane == hd, 1.0, 0.0)).astype(vo_ref.dtype)


def _qkprep(proj3, q_gain_row, k_gain_row, gq, gk, cos2, sin2, n_q_heads, n_kv_heads, cols, ts):
    b, seq, _ = proj3.shape
    qcol, kcol, vcol = cols
    qw, kw = n_q_heads * HEAD_DIM, n_kv_heads * HEAD_DIM
    fixed = lambda bi, i: (0, 0)
    return pl.pallas_call(
        functools.partial(_qkprep_body, n_q_heads=n_q_heads, n_kv_heads=n_kv_heads), grid=(b, seq // ts),
        in_specs=[pl.BlockSpec((None, ts, qw), lambda bi, i: (bi, i, qcol)),
                  pl.BlockSpec((None, ts, kw), lambda bi, i: (bi, i, kcol)),
                  pl.BlockSpec((None, ts, kw), lambda bi, i: (bi, i, vcol)),
                  pl.BlockSpec((1, qw), fixed), pl.BlockSpec((1, kw), fixed),
                  pl.BlockSpec((qw, qw), fixed), pl.BlockSpec((kw, kw), fixed),
                  pl.BlockSpec((ts, kw), lambda bi, i: (i, 0)), pl.BlockSpec((ts, kw), lambda bi, i: (i, 0))],
        out_specs=[pl.BlockSpec((None, n_q_heads, ts, HEAD_DIM), lambda bi, i: (bi, 0, i, 0)),
                   pl.BlockSpec((None, n_kv_heads, HEAD_DIM, ts), lambda bi, i: (bi, 0, 0, i)),
                   pl.BlockSpec((None, n_kv_heads, ts, kw), lambda bi, i: (bi, 0, i, 0))],
        out_shape=[jax.ShapeDtypeStruct((b, n_q_heads, seq, HEAD_DIM), MXU_DTYPE),
                   jax.ShapeDtypeStruct((b, n_kv_heads, HEAD_DIM, seq), MXU_DTYPE),
                   jax.ShapeDtypeStruct((b, n_kv_heads, seq, kw), MXU_DTYPE)],
        compiler_params=_cparams("parallel", "parallel"), name="qk_prep",
    )(proj3, proj3, proj3, q_gain_row, k_gain_row, gq, gk, cos2, sin2)


def _flash_body(q_ref, kt_ref, v_ref, o_ref, *, seq, tq, tk, group):
    hd = HEAD_DIM
    rows = group * tq
    q = q_ref[...].reshape(rows, hd)
    m = jnp.full((rows, 1), NEG_INF, F32)
    acc = jnp.zeros((rows, v_ref.shape[1]), F32)
    for kc in range(seq // tk):
        s = jnp.dot(q, kt_ref[:, kc * tk:(kc + 1) * tk], preferred_element_type=F32)
        m_new = jnp.maximum(m, jnp.max(s, axis=-1, keepdims=True))
        p = jnp.exp2(s - m_new)
        acc = jnp.exp2(m - m_new) * acc + jnp.dot(p.astype(v_ref.dtype), v_ref[kc * tk:(kc + 1) * tk, :],
                                                  preferred_element_type=F32)
        m = m_new
    o = acc[:, :hd] / acc[:, hd:hd + 1]
    o_ref[...] = jnp.concatenate([o[g * tq:(g + 1) * tq] for g in range(group)], axis=1).astype(o_ref.dtype)


def _flash_attention(q4, kt4, v4, tq, tk):
    b, n_q_heads, seq, hd = q4.shape
    n_kv_heads = kt4.shape[1]
    group = n_q_heads // n_kv_heads
    return pl.pallas_call(
        functools.partial(_flash_body, seq=seq, tq=tq, tk=tk, group=group), grid=(b, n_kv_heads, seq // tq),
        in_specs=[pl.BlockSpec((None, group, tq, hd), lambda bi, h, i: (bi, h, i, 0)),
                  pl.BlockSpec((None, None, hd, seq), lambda bi, h, i: (bi, h, 0, 0)),
                  pl.BlockSpec((None, None, seq, v4.shape[3]), lambda bi, h, i: (bi, h, 0, 0))],
        out_specs=pl.BlockSpec((None, tq, group * hd), lambda bi, h, i: (bi, i, h)),
        out_shape=jax.ShapeDtypeStruct((b, seq, n_q_heads * hd), MXU_DTYPE),
        compiler_params=_cparams("parallel", "parallel", "parallel"), name="flash_gqa",
    )(q4, kt4, v4)


def _kvproj_body(m_ref, w_ref, o_ref):
    o_ref[...] = _mm(m_ref[...], w_ref[...]).astype(o_ref.dtype)


def _kvproj(mem2d, wkv, tm):
    t, d = mem2d.shape
    n = wkv.shape[1]
    return pl.pallas_call(
        _kvproj_body, grid=(t // tm,),
        in_specs=[pl.BlockSpec((tm, d), lambda i: (i, 0)), pl.BlockSpec((d, n), lambda i: (0, 0))],
        out_specs=pl.BlockSpec((tm, n), lambda i: (i, 0)), out_shape=jax.ShapeDtypeStruct((t, n), MXU_DTYPE),
        compiler_params=_cparams("parallel"), name="mem_kv_proj",
    )(mem2d, wkv)


def _crossattn_body(x_ref, yp_ref, ym_ref, ya_ref, wp_ref, wm_ref, wa_ref, kv_ref, gm_ref, wq_ref, wo_ref, gf_ref,
                    rwt_ref, xo_ref, hf_ref, lg_ref, *, ts):
    x = (x_ref[...] + _mm(yp_ref[...], wp_ref[...]) + _mm(ym_ref[...], wm_ref[...])
         + _mm(ya_ref[...], wa_ref[...]))
    q = _mm(_rms(x, gm_ref[...]), wq_ref[...])
    ca_width = q.shape[1]
    dh = ca_width // CA_HEADS
    kv = kv_ref[...]
    hs = range(CA_HEADS)
    s = [_mm_nt(q[:, h * dh:(h + 1) * dh], kv[:, h * dh:(h + 1) * dh]) * (dh ** -0.5) for h in hs]
    e = [jnp.exp(s[h] - jnp.max(s[h], axis=-1, keepdims=True)) for h in hs]
    p = [e[h] / jnp.sum(e[h], axis=-1, keepdims=True) for h in hs]
    heads = [_mm(p[h], kv[:, ca_width + h * dh:ca_width + (h + 1) * dh]) for h in hs]
    x2 = x + _mm(jnp.concatenate(heads, axis=1), wo_ref[...])
    xo_ref[...] = x2
    hf = _rms(x2, gf_ref[...])
    _to_token_tiles(hf_ref, hf, ts)
    lg_ref[...] = _mm_nt(rwt_ref[...], hf)


def _crossattn(x3, yp, ym, ya, w_out, kv3, g_mem, wq, wo, g_ffn, router_wt, ts):
    b, seq, d = x3.shape
    n_mem, kvw = kv3.shape[1], kv3.shape[2]
    ne = router_wt.shape[0]
    fixed = lambda bi, i: (0, 0)
    tile = lambda w: pl.BlockSpec((None, ts, w), lambda bi, i: (bi, i, 0))
    wp, wm, wa = yp.shape[2], ym.shape[2], ya.shape[2]
    return pl.pallas_call(
        functools.partial(_crossattn_body, ts=ts), grid=(b, seq // ts),
        in_specs=[tile(d), tile(wp), tile(wm), tile(wa),
                  pl.BlockSpec((wp, d), fixed), pl.BlockSpec((wm, d), fixed), pl.BlockSpec((wa, d), fixed),
                  pl.BlockSpec((None, n_mem, kvw), lambda bi, i: (bi, 0, 0)),
                  pl.BlockSpec((1, d), fixed), pl.BlockSpec(wq.shape, fixed), pl.BlockSpec(wo.shape, fixed),
                  pl.BlockSpec((1, d), fixed), pl.BlockSpec((ne, d), fixed)],
        out_specs=[pl.BlockSpec((None, ts, d), lambda bi, i: (bi, i, 0)),
                   pl.BlockSpec((None, ts * SUBLANES, LANES), lambda bi, i: (bi, i, 0)),
                   pl.BlockSpec((None, ne, ts), lambda bi, i: (bi, 0, i))],
        out_shape=[jax.ShapeDtypeStruct((b, seq, d), F32),
                   jax.ShapeDtypeStruct((b, seq * SUBLANES, LANES), F32),
                   jax.ShapeDtypeStruct((b, ne, seq), F32)],
        compiler_params=_cparams("parallel", "parallel"), name="cross_attn",
    )(x3, yp, ym, ya, w_out[:wp], w_out[wp:wp + wm], w_out[wp + wm:], kv3, g_mem, wq, wo, g_ffn, router_wt)


def _excl_prefix(flags, upper_excl, ones):
    rows, seq = flags.shape
    off = jnp.zeros((rows, LANES), F32)
    out = []
    for c in range(seq // LANES):
        xc = flags[:, c * LANES:(c + 1) * LANES].astype(MXU_DTYPE)
        out.append(jnp.dot(xc, upper_excl, preferred_element_type=F32) + off)
        off = off + jnp.dot(xc, ones, preferred_element_type=F32)
    return jnp.concatenate(out, axis=1)


def _router_body(lg_ref, idx_ref, gate_ref, *, seq, cap):
    lg = lg_ref[...]
    ne = lg.shape[0]
    ex = jnp.exp(lg - jnp.max(lg, axis=0, keepdims=True))
    aff = ex / jnp.sum(ex, axis=0, keepdims=True)
    bits = pltpu.bitcast(aff, I32)

    thr = jnp.zeros((ne, 1), I32)
    for bit in range(30, -1, -1):
        cand = thr | (1 << bit)
        cnt = jnp.sum((bits >= cand).astype(F32), axis=1, keepdims=True)
        thr = jnp.where(cnt >= cap, cand, thr)

    r_i = lax.broadcasted_iota(I32, (LANES, LANES), 0)
    c_i = lax.broadcasted_iota(I32, (LANES, LANES), 1)
    upper_excl = (r_i < c_i).astype(MXU_DTYPE)
    ones = jnp.ones((LANES, LANES), MXU_DTYPE)

    gt = bits > thr
    eq = bits == thr
    n_gt = jnp.sum(gt.astype(F32), axis=1, keepdims=True)
    tie_rank = _excl_prefix(eq.astype(F32), upper_excl, ones)
    sel = gt | (eq & (tie_rank < cap - n_gt))
    pos = _excl_prefix(sel.astype(F32), upper_excl, ones).astype(I32)

    n_iota = lax.broadcasted_iota(I32, (ne, seq), 1)
    seq_bits = (seq - 1).bit_length()
    flag_bit = 2 * seq_bits
    packed = jnp.where(sel, n_iota | ((n_iota - pos) << seq_bits) | (1 << flag_bit), 0)
    gbits = jnp.where(sel, bits, 0)
    for k in range(seq_bits):
        sh = 1 << k
        moved_p = pltpu.roll(packed, seq - sh, 1)
        moved_g = pltpu.roll(gbits, seq - sh, 1)
        take = ((moved_p >> flag_bit) & 1) * ((moved_p >> (seq_bits + k)) & 1) == 1
        stay = ((packed >> flag_bit) & 1) * (1 - ((packed >> (seq_bits + k)) & 1)) == 1
        packed = jnp.where(take, moved_p, jnp.where(stay, packed, 0))
        gbits = jnp.where(take, moved_g, jnp.where(stay, gbits, 0))
    idx_ref[...] = packed[:, :cap] & (seq - 1)
    gate_ref[...] = pltpu.bitcast(gbits[:, :cap], F32)


def _router(logits_t, cap):
    b, ne, seq = logits_t.shape
    return pl.pallas_call(
        functools.partial(_router_body, seq=seq, cap=cap), grid=(b,),
        in_specs=[pl.BlockSpec((None, ne, seq), lambda bi: (bi, 0, 0))],
        out_specs=[pl.BlockSpec((None, ne, cap), lambda bi: (bi, 0, 0)),
                   pl.BlockSpec((None, ne, cap), lambda bi: (bi, 0, 0))],
        out_shape=[jax.ShapeDtypeStruct((b, ne, cap), I32), jax.ShapeDtypeStruct((b, ne, cap), F32)],
        compiler_params=_cparams("parallel"), name="ec_router",
    )(logits_t)


def _moe_body(*refs, seq, cap, ne, n_fc, stride, b_off, cast, chained):
    idx_hbm, gate_hbm, h_hbm, wg_ref, wu_ref, wd_ref = refs[:6]
    prev_hbm = refs[6] if chained else None
    refs = refs[6 + (1 if chained else 0):]
    out_hbm = refs[0]
    w_outs = refs[1:4] if cast else ()
    idx_s, gate_s, xbuf, xbf, ybuf, yacc, acc, gsem, isem, osem, psem = refs[1 + len(w_outs):]

    def prev_copy():
        return pltpu.make_async_copy(prev_hbm, out_hbm.at[pl.ds(0, b_off)], psem)
    b = pl.program_id(0)
    e = pl.program_id(1)
    c = pl.program_id(2)
    last = pl.num_programs(0) * ne - 1
    step = b * ne + e
    nstep = jnp.minimum(step + 1, last)
    xslot, nxslot = lax.rem(step, 2), lax.rem(step + 1, 2)
    islot, nislot = lax.rem(step, N_IDX_SLOTS), lax.rem(step + 1, N_IDX_SLOTS)
    tile = SUBLANES
    per = cap // n_fc

    def idx_copies(s, sl):
        src = pl.ds(pl.multiple_of((b_off * ne + s) * stride, stride), stride)
        dst = pl.ds(pl.multiple_of(sl * stride, stride), stride)
        return (pltpu.make_async_copy(idx_hbm.at[src], idx_s.at[dst], isem.at[0, sl]),
                pltpu.make_async_copy(gate_hbm.at[src], gate_s.at[dst], isem.at[1, sl]))

    def row_copy(src_row, k, sl):
        return pltpu.make_async_copy(h_hbm.at[pl.ds(src_row, tile), :],
                                     xbuf.at[sl, pl.ds(k * tile, tile), :], gsem.at[sl])

    def start_rows(s, isl, sl, k0, n):
        base = (b_off + s // ne) * seq
        ibase = isl * stride + k0
        for k in range(n):
            src_row = pl.multiple_of((base + idx_s[ibase + k]) * tile, tile)
            row_copy(src_row, k0 + k, sl).start(priority=GATHER_DMA_PRIORITY)

    def wait_gather(sl):
        pltpu.make_async_copy(h_hbm.at[pl.ds(0, cap * tile), :], xbuf.at[sl], gsem.at[sl]).wait()

    @pl.when(c == 0)
    def _():
        @pl.when(step == 0)
        def _():
            for cp in idx_copies(0, 0):
                cp.start()
            for cp in idx_copies(0, 0):
                cp.wait()
            for cp in idx_copies(jnp.minimum(1, last), 1):
                cp.start()
            if chained:
                prev_copy().start()

            def body(j, carry):
                start_rows(0, 0, 0, j * SUBLANES, SUBLANES)
                return carry
            lax.fori_loop(0, cap // SUBLANES, body, 0)

        for cp in idx_copies(0, nislot):
            cp.wait()
        for cp in idx_copies(jnp.minimum(step + 2, last), lax.rem(step + 2, N_IDX_SLOTS)):
            cp.start()
        wait_gather(xslot)
        xbf[...] = _from_token_tiles(xbuf.at[xslot], cap).astype(xbf.dtype)

        @pl.when(e == 0)
        def _():
            acc[...] = jnp.zeros(acc.shape, F32)

    start_rows(nstep, nislot, nxslot, pl.multiple_of(c * per, per), per)

    x = xbf[...]
    fc = wg_ref.shape[1]
    sub = min(fc, MOE_SUB_COLS)
    y = None
    for j in range(fc // sub):
        cols = slice(j * sub, (j + 1) * sub)
        wg, wu, wd = wg_ref[:, cols], wu_ref[:, cols], wd_ref[cols, :]
        if cast:
            wg, wu, wd = wg.astype(MXU_DTYPE), wu.astype(MXU_DTYPE), wd.astype(MXU_DTYPE)
            w_outs[0][:, cols] = wg
            w_outs[1][:, cols] = wu
            w_outs[2][cols, :] = wd
        g = jnp.dot(x, wg, preferred_element_type=F32)
        u = jnp.dot(x, wu, preferred_element_type=F32)
        yj = _mm(g * jax.nn.sigmoid(g) * u, wd)
        y = yj if y is None else y + yj

    if n_fc == 1:
        _to_token_tiles(ybuf, y, cap)
    else:
        @pl.when(c == 0)
        def _():
            yacc[...] = y

        @pl.when(c > 0)
        def _():
            yacc[...] += y

    @pl.when(c == n_fc - 1)
    def _():
        if n_fc > 1:
            _to_token_tiles(ybuf, yacc[...], cap)

        ibase = islot * stride
        for k0 in range(0, cap, SCATTER_GROUP):
            rows = [pl.multiple_of(idx_s[ibase + k0 + j] * tile, tile) for j in range(SCATTER_GROUP)]
            new = [acc[pl.ds(rows[j], tile), :] + gate_s[ibase + k0 + j] * ybuf[(k0 + j) * tile:(k0 + j + 1) * tile, :]
                   for j in range(SCATTER_GROUP)]
            for j in range(SCATTER_GROUP):
                acc[pl.ds(rows[j], tile), :] = new[j]

        @pl.when(e == ne - 1)
        def _():
            cp = pltpu.make_async_copy(acc, out_hbm.at[b_off + b], osem)
            cp.start()
            cp.wait()

        @pl.when(step == last)
        def _():
            wait_gather(nxslot)
            for cp in idx_copies(0, lax.rem(step + 2, N_IDX_SLOTS)):
                cp.wait()
            if chained:
                prev_copy().wait()


def _moe(idx, gate, h_tiles, w_gate, w_up, w_down, up_block_off, seq, fc, b_off, n_b, prev_out, cast, layer=None):
    _, ne, cap = idx.shape
    d, f = w_down.shape[-1], w_down.shape[-2]
    n_fc = f // fc
    lead = () if layer is None else (layer,)
    w_spec = lambda shape, imap: pl.BlockSpec((None,) * (len(lead) + 1) + shape,
                                               lambda b, e, c: lead + imap(e, c))
    chained = prev_out is not None
    stride = -(-cap // SMEM_1D_TILE) * SMEM_1D_TILE
    flat = lambda a: jnp.pad(a, ((0, 0), (0, 0), (0, stride - cap))).reshape(-1)
    any_spec = pl.BlockSpec(memory_space=pl.ANY)
    assert chained == (b_off > 0)
    out_shape = [jax.ShapeDtypeStruct((b_off + n_b, seq * SUBLANES, LANES), F32)]
    out_specs = [any_spec]
    if cast:
        out_shape += [jax.ShapeDtypeStruct((ne, d, f), MXU_DTYPE), jax.ShapeDtypeStruct((ne, d, f), MXU_DTYPE),
                      jax.ShapeDtypeStruct((ne, f, d), MXU_DTYPE)]
        out_specs += [pl.BlockSpec((None, d, fc), lambda b, e, c: (e, 0, c)),
                      pl.BlockSpec((None, d, fc), lambda b, e, c: (e, 0, c)),
                      pl.BlockSpec((None, fc, d), lambda b, e, c: (e, c, 0))]
    return pl.pallas_call(
        functools.partial(_moe_body, seq=seq, cap=cap, ne=ne, n_fc=n_fc, stride=stride, b_off=b_off, cast=cast,
                          chained=chained),
        grid=(n_b, ne, n_fc),
        in_specs=[any_spec, any_spec, any_spec,
                  w_spec((d, fc), lambda e, c: (e, 0, c)),
                  w_spec((d, fc), lambda e, c: (e, 0, up_block_off + c)),
                  w_spec((fc, d), lambda e, c: (e, c, 0))] + ([any_spec] if chained else []),
        out_specs=out_specs, out_shape=out_shape,
        scratch_shapes=[pltpu.SMEM((N_IDX_SLOTS * stride,), I32), pltpu.SMEM((N_IDX_SLOTS * stride,), F32),
                        pltpu.VMEM((2, cap * SUBLANES, LANES), F32), pltpu.VMEM((cap, d), MXU_DTYPE),
                        pltpu.VMEM((cap * SUBLANES, LANES), F32), pltpu.VMEM((cap, d), F32),
                        pltpu.VMEM((seq * SUBLANES, LANES), F32),
                        pltpu.SemaphoreType.DMA((2,)), pltpu.SemaphoreType.DMA((2, N_IDX_SLOTS)),
                        pltpu.SemaphoreType.DMA, pltpu.SemaphoreType.DMA],
        compiler_params=_cparams("arbitrary", "arbitrary", "arbitrary"),
        name="ec_moe_cast" if cast else "ec_moe",
    )(flat(idx), flat(gate), h_tiles, w_gate, w_up, w_down, *([prev_out] if chained else []))


def _final_body(x_ref, moe_ref, g_ref, o_ref, *, tm):
    o_ref[...] = _rms(x_ref[...] + _from_token_tiles(moe_ref, tm), g_ref[...])


def _final_norm(x2d, moe_tiles, g, tm):
    t, d = x2d.shape
    row = lambda i: (i, 0)
    return pl.pallas_call(
        functools.partial(_final_body, tm=tm), grid=(t // tm,),
        in_specs=[pl.BlockSpec((tm, d), row), pl.BlockSpec((tm * SUBLANES, LANES), row),
                  pl.BlockSpec((1, d), lambda i: (0, 0))],
        out_specs=pl.BlockSpec((tm, d), row), out_shape=jax.ShapeDtypeStruct((t, d), F32),
        compiler_params=_cparams("parallel"), name="final_norm",
    )(x2d, moe_tiles, g)


def _rope_tables(seq):
    rows = seq // GRID_W
    row_ids = jnp.repeat(jnp.arange(rows), GRID_W).astype(F32)
    col_ids = jnp.tile(jnp.arange(GRID_W), rows).astype(F32)
    n_freq = HEAD_DIM // 4
    inv_freq = ROPE_THETA ** (-jnp.arange(n_freq, dtype=F32) / n_freq)
    ang = jnp.concatenate([row_ids[:, None] * inv_freq, col_ids[:, None] * inv_freq], axis=-1)
    cos, sin = jnp.cos(ang), jnp.sin(ang)
    cos_i = jnp.repeat(cos, 2, axis=1)
    sin_i = jnp.stack([-sin, sin], axis=-1).reshape(seq, HEAD_DIM)
    return jnp.tile(cos_i, (1, LANES // HEAD_DIM)), jnp.tile(sin_i, (1, LANES // HEAD_DIM))


def _group_matrix(width):
    gid = jnp.arange(width) // HEAD_DIM
    return (gid[:, None] == gid[None, :]).astype(MXU_DTYPE)


def _block_diag(w):
    g, c, _ = w.shape
    eye = jnp.eye(g, dtype=w.dtype)
    return (eye[:, None, :, None] * w[:, :, None, :]).reshape(g * c, g * c)


def _pick_tile(n, pref):
    t = min(n, pref)
    while n % t:
        t //= 2
    return t


def kernel(x, mem, norm_mix_g, w_in, pool_w, pool_scale, mlstm_conv_w, mlstm_gate_b, mlstm_norm_g, q_norm_g,
           k_norm_g, w_out, norm_mem_g, ca_wq, ca_wkv, ca_wo, norm_ffn_g, router_w, expert_w_gu, expert_w_down,
           final_norm_g):
    batch, seq, d = x.shape
    depth = w_in.shape[0]
    n_mem = mem.shape[1]
    pool_width = pool_scale.shape[1]
    m_width = mlstm_norm_g.shape[1]
    m_heads = m_width // HEAD_DIM
    n_gate = 2 * m_heads
    kv_width = (w_in.shape[2] - pool_width - 4 * m_width - 2 * n_gate) // 6
    q_width = 4 * kv_width
    n_q_heads, n_kv_heads = q_width // HEAD_DIM, kv_width // HEAD_DIM
    ne = router_w.shape[2]
    cap = EC_CAPACITY_FACTOR * seq // ne
    f_exp = expert_w_down.shape[2]
    t = batch * seq

    o_pool = 0
    o_mq = o_pool + pool_width
    o_mi = o_mq + 4 * m_width
    o_aq = o_mi + 2 * n_gate
    o_ak = o_aq + q_width
    c_pool = q_width // pool_width
    c_mq = (q_width + pool_width) // LANES
    m_cols = tuple(c_mq + j * (m_width // LANES) for j in range(4))
    c_ak = (q_width + pool_width + 4 * m_width) // kv_width
    cos2, sin2 = _rope_tables(seq)
    gq, gk = _group_matrix(q_width), _group_matrix(kv_width)
    n_pairs = m_heads // 2
    gate_cols = [[kind * n_gate + dr * m_heads + 2 * p + hh for kind in range(2) for dr in range(2) for hh in range(2)]
                 for p in range(n_pairs)]

    assert kv_width == LANES and MLSTM_CHUNK == LANES and m_heads % 2 == 0
    tm = _pick_tile(t, TOKEN_TILE)
    ts = _pick_tile(seq, TOKEN_TILE)
    x2d = x.reshape(t, d)
    mem2d = mem.reshape(batch * n_mem, d)
    moe_tiles = None
    for layer in range(depth):
        wl = w_in[layer]
        w_main = jnp.concatenate([wl[:, o_aq:o_ak], wl[:, o_pool:o_mi], wl[:, o_ak:]], axis=1).astype(MXU_DTYPE)
        w_gate = jnp.concatenate(
            [jnp.pad(wl[:, o_mi:o_aq][:, jnp.array(cols)], ((0, 0), (0, LANES - len(cols)))) for cols in gate_cols],
            axis=1).astype(MXU_DTYPE)
        gb_row = jnp.concatenate(
            [jnp.pad(mlstm_gate_b[layer][jnp.array(cols)], (0, LANES - len(cols))) for cols in gate_cols])[None]
        x2d, proj, gates = _inproj(x2d, moe_tiles, norm_mix_g[layer][None], w_main, w_gate, tm)
        proj3 = proj.reshape(batch, seq, proj.shape[1])
        gates3 = gates.reshape(batch, seq, gates.shape[1])

        y_pool = _pool_mixer(proj3, _block_diag(pool_w[layer]).astype(MXU_DTYPE), pool_scale[layer][None], c_pool,
                             _pick_tile(seq, POOL_TILE))
        y_mlstm = _mlstm(proj3, gates3, mlstm_conv_w[layer], gb_row, mlstm_norm_g[layer][None], m_heads, m_cols)
        q4, kt4, v4 = _qkprep(proj3, jnp.tile(q_norm_g[layer], n_q_heads)[None],
                              jnp.tile(k_norm_g[layer], n_kv_heads)[None], gq, gk, cos2, sin2,
                              n_q_heads, n_kv_heads, (0, c_ak, c_ak + 1), ts)
        y_attn = _flash_attention(q4, kt4, v4, _pick_tile(seq, FLASH_Q_TILE), _pick_tile(seq, FLASH_KV_TILE))

        kv = _kvproj(mem2d, ca_wkv[layer].astype(MXU_DTYPE), _pick_tile(batch * n_mem, TOKEN_TILE))
        x3, h_tiles, logits_t = _crossattn(
            x2d.reshape(batch, seq, d), y_pool, y_mlstm, y_attn, w_out[layer].astype(MXU_DTYPE),
            kv.reshape(batch, n_mem, kv.shape[1]), norm_mem_g[layer][None],
            ca_wq[layer].astype(MXU_DTYPE), ca_wo[layer].astype(MXU_DTYPE), norm_ffn_g[layer][None],
            router_w[layer].T.astype(MXU_DTYPE), ts)
        x2d = x3.reshape(t, d)

        idx, gate = _router(logits_t, cap)
        h_flat = h_tiles.reshape(t * SUBLANES, LANES)
        fc_cast = _pick_tile(f_exp, MOE_F_CHUNK_CAST)
        moe_tiles, wg_c, wu_c, wd_c = _moe(idx, gate, h_flat, expert_w_gu, expert_w_gu, expert_w_down,
                                           f_exp // fc_cast, seq, fc_cast, 0, 1, None, True, layer)
        if batch > 1:
            moe_tiles, = _moe(idx, gate, h_flat, wg_c, wu_c, wd_c, 0, seq, _pick_tile(f_exp, MOE_F_CHUNK), 1,
                              batch - 1, moe_tiles, False)
        moe_tiles = moe_tiles.reshape(t * SUBLANES, LANES)
    out = _final_norm(x2d, moe_tiles, final_norm_g[None], tm)
    return out.reshape(batch, seq, d)
```

```python
import functools
import math

import jax
import jax.numpy as jnp
from jax import lax
from jax.experimental import pallas as pl
from jax.experimental.pallas import tpu as pltpu

F32 = jnp.float32
I32 = jnp.int32
MXU_DTYPE = jnp.bfloat16

LANES = 128
SUBLANES = 8
VMEM_LIMIT_BYTES = 56 * 1024 * 1024

TOKEN_TILE = 1024
POOL_TILE = 256
FLASH_Q_TILE = 512
FLASH_KV_TILE = 512
MOE_F_CHUNK = 1024
MOE_F_CHUNK_CAST = 512

EPS = 1e-6
GRID_W = 64
HEAD_DIM = 64
POOL_WINDOWS = (2, 4, 8, 16)
POOL_HALO = 8
MLSTM_CHUNK = 128
MLSTM_CONV = 5
CONV_HALO = 8
ROPE_THETA = 10000.0
CA_HEADS = 4
N_EXPERTS = 16
EC_CAPACITY_FACTOR = 2
SMEM_1D_TILE = 1024
N_IDX_SLOTS = 3
MOE_SUB_COLS = 512
GATHER_DMA_PRIORITY = 1
SCATTER_GROUP = 16
NEG_INF = float("-inf")
LOG2_E = math.log2(math.e)


def _cparams(*sem):
    return pltpu.CompilerParams(dimension_semantics=sem, vmem_limit_bytes=VMEM_LIMIT_BYTES)


def _mm(a, b):
    return jnp.dot(a.astype(MXU_DTYPE), b.astype(MXU_DTYPE), preferred_element_type=F32)


def _mm_nt(a, b):
    return lax.dot_general(a.astype(MXU_DTYPE), b.astype(MXU_DTYPE), (((1,), (1,)), ((), ())),
                           preferred_element_type=F32)


def _split2(x):
    hi = x.astype(MXU_DTYPE)
    return hi, (x - hi.astype(F32)).astype(MXU_DTYPE)


def _rms(x, g):
    return x * lax.rsqrt(jnp.mean(x * x, axis=-1, keepdims=True) + EPS) * g


def _from_token_tiles(ref, rows):
    return jnp.concatenate([ref[pl.ds(s, rows, stride=SUBLANES), :] for s in range(SUBLANES)], axis=1)


def _to_token_tiles(ref, val, rows):
    for s in range(SUBLANES):
        ref[pl.ds(s, rows, stride=SUBLANES), :] = val[:, s * LANES:(s + 1) * LANES]


def _inproj_body(x_ref, g_ref, w_ref, wg_ref, o_ref, og_ref):
    hb = _rms(x_ref[...], g_ref[...]).astype(MXU_DTYPE)
    o_ref[...] = jnp.dot(hb, w_ref[...], preferred_element_type=F32).astype(o_ref.dtype)
    og_ref[...] = jnp.dot(hb, wg_ref[...], preferred_element_type=F32)


def _inproj_merge_body(x_ref, moe_ref, g_ref, w_ref, wg_ref, xo_ref, o_ref, og_ref, *, tm):
    x = x_ref[...] + _from_token_tiles(moe_ref, tm)
    xo_ref[...] = x
    hb = _rms(x, g_ref[...]).astype(MXU_DTYPE)
    o_ref[...] = jnp.dot(hb, w_ref[...], preferred_element_type=F32).astype(o_ref.dtype)
    og_ref[...] = jnp.dot(hb, wg_ref[...], preferred_element_type=F32)


def _inproj(x2d, moe_tiles, g, w_main, w_gate, tm):
    t, d = x2d.shape
    n = w_main.shape[1]
    grid = (t // tm,)
    row = lambda i: (i, 0)
    fixed = lambda i: (0, 0)
    gw = w_gate.shape[1]
    w_specs = [pl.BlockSpec((1, d), fixed), pl.BlockSpec((d, n), fixed), pl.BlockSpec((d, gw), fixed)]
    o_specs = [pl.BlockSpec((tm, n), row), pl.BlockSpec((tm, gw), row)]
    o_shapes = [jax.ShapeDtypeStruct((t, n), MXU_DTYPE), jax.ShapeDtypeStruct((t, gw), F32)]
    if moe_tiles is None:
        proj, gates = pl.pallas_call(
            _inproj_body, grid=grid, in_specs=[pl.BlockSpec((tm, d), row)] + w_specs, out_specs=o_specs,
            out_shape=o_shapes, compiler_params=_cparams("parallel"), name="inproj",
        )(x2d, g, w_main, w_gate)
        return x2d, proj, gates
    xo, proj, gates = pl.pallas_call(
        functools.partial(_inproj_merge_body, tm=tm), grid=grid,
        in_specs=[pl.BlockSpec((tm, d), row), pl.BlockSpec((tm * SUBLANES, LANES), row)] + w_specs,
        out_specs=[pl.BlockSpec((tm, d), row)] + o_specs,
        out_shape=[jax.ShapeDtypeStruct((t, d), F32)] + o_shapes,
        compiler_params=_cparams("parallel"), name="inproj_merge",
    )(x2d, moe_tiles, g, w_main, w_gate)
    return xo, proj, gates


def _pool_body(u_ref, w_ref, sc_ref, o_ref, upad_ref, *, seq, tc):
    i = pl.program_id(1)

    @pl.when(i == 0)
    def _():
        zeros = jnp.zeros((POOL_HALO, LANES), F32)
        for hf in range(2):
            upad_ref[hf, 0:POOL_HALO, :] = zeros
            upad_ref[hf, POOL_HALO + seq:POOL_HALO + seq + POOL_HALO, :] = zeros
            upad_ref[hf, POOL_HALO:POOL_HALO + seq, :] = u_ref[:, hf * LANES:(hf + 1) * LANES].astype(F32)

    t0 = pl.multiple_of(i * tc, tc)
    t = t0 + lax.broadcasted_iota(I32, (tc, LANES), 0)
    lane = lax.broadcasted_iota(I32, (tc, LANES), 1)
    upper = lane >= (LANES // 2)
    halves = []
    for hf in range(2):
        w_lo, w_hi = POOL_WINDOWS[2 * hf], POOL_WINDOWS[2 * hf + 1]
        acc = jnp.zeros((tc, LANES), F32)
        for j in range(-(w_hi // 2), w_hi // 2):
            piece = upad_ref[hf, pl.ds(t0 + (POOL_HALO + j), tc), :]
            if -(w_lo // 2) <= j < w_lo // 2:
                acc = acc + piece
            else:
                acc = acc + jnp.where(upper, piece, 0.0)
        half_w = jnp.where(upper, w_hi // 2, w_lo // 2)
        cnt = jnp.minimum(t + half_w, seq) - jnp.maximum(t - half_w, 0)
        centre = upad_ref[hf, pl.ds(t0 + POOL_HALO, tc), :]
        halves.append(acc / cnt.astype(F32) - centre)
    d = jnp.concatenate(halves, axis=1)
    o_ref[...] = (_mm(d, w_ref[...]) * sc_ref[...]).astype(o_ref.dtype)


def _pool_mixer(proj3, w_bd, scale, col_block, tc):
    b, seq, _ = proj3.shape
    pw = w_bd.shape[0]
    return pl.pallas_call(
        functools.partial(_pool_body, seq=seq, tc=tc), grid=(b, seq // tc),
        in_specs=[pl.BlockSpec((None, seq, pw), lambda bi, i: (bi, 0, col_block)),
                  pl.BlockSpec((pw, pw), lambda bi, i: (0, 0)),
                  pl.BlockSpec((1, pw), lambda bi, i: (0, 0))],
        out_specs=pl.BlockSpec((None, tc, pw), lambda bi, i: (bi, i, 0)),
        out_shape=jax.ShapeDtypeStruct((b, seq, pw), MXU_DTYPE),
        scratch_shapes=[pltpu.VMEM((pw // LANES, seq + 2 * POOL_HALO, LANES), F32)],
        compiler_params=_cparams("parallel", "arbitrary"), name="pool_mixer",
    )(proj3, w_bd, scale)


def _mlstm_body(q_ref, k_ref, v_ref, o_ref, gt_ref, cwq_ref, cwk_ref, gb_ref, ng_ref, y_ref,
                qpad_ref, kpad_ref, qc_ref, kc_ref, kt_ref, gtt_ref, hf_ref, hb_ref, cn_ref, m_ref,
                *, seq, n_heads):
    L = MLSTM_CHUNK
    nc = seq // L
    hd = HEAD_DIM
    gbias = gb_ref[...]

    zeros = jnp.zeros((CONV_HALO, LANES), F32)
    for pad_ref, src_ref in ((qpad_ref, q_ref), (kpad_ref, k_ref)):
        pad_ref[0:CONV_HALO, :] = zeros
        pad_ref[CONV_HALO + seq:CONV_HALO + seq + CONV_HALO, :] = zeros
        pad_ref[CONV_HALO:CONV_HALO + seq, :] = src_ref[...].astype(F32)

    conv_unroll = 2 if nc % 2 == 0 else 1

    def conv_chunk(ci, carry):
        t0s = [pl.multiple_of((ci * conv_unroll + r) * L, L) for r in range(conv_unroll)]
        jobs = [(t0, pad_ref, cw_ref) for t0 in t0s for pad_ref, cw_ref in ((qpad_ref, cwq_ref), (kpad_ref, cwk_ref))]
        accs = [jnp.zeros((L, LANES), F32) for _ in jobs]
        for j in range(MLSTM_CONV):
            off = j - MLSTM_CONV // 2
            pieces = [pad_ref[pl.ds(t0 + (CONV_HALO + off), L), :] for t0, pad_ref, _ in jobs]
            accs = [acc + piece * cw_ref[j:j + 1, :] for acc, piece, (_, _, cw_ref) in zip(accs, pieces, jobs)]
        acts = [acc * jax.nn.sigmoid(acc) for acc in accs]
        gts = [(gt_ref[pl.ds(t0, L), :] + gbias).T for t0 in t0s]
        for r, t0 in enumerate(t0s):
            k_act = acts[2 * r + 1] * (1.0 / math.sqrt(hd))
            qc_ref[pl.ds(t0, L), :] = acts[2 * r]
            kc_ref[pl.ds(t0, L), :] = k_act
            kt_ref[:, pl.ds(t0, L)] = k_act.T
            gtt_ref[:, pl.ds(t0, L)] = gts[r]
        return carry

    lax.fori_loop(0, nc // conv_unroll, conv_chunk, 0)

    cn_ref[...] = jnp.zeros(cn_ref.shape, F32)
    m_ref[...] = jnp.zeros(m_ref.shape, F32)

    row = lax.broadcasted_iota(I32, (L, L), 0)
    col = lax.broadcasted_iota(I32, (L, L), 1)
    tri = ((col <= row).astype(MXU_DTYPE), (col >= row).astype(MXU_DTYPE))
    tri_t = (tri[1], tri[0])
    keep = (col <= row, col >= row)
    gate_lane = lax.broadcasted_iota(I32, (L, LANES), 1)
    gate_sub = lax.broadcasted_iota(I32, (LANES, L), 0)
    is_i, is_f = gate_lane < 4, (gate_lane >= 4) & (gate_lane < 8)
    is_i_t, is_f_t = gate_sub < 4, (gate_sub >= 4) & (gate_sub < 8)
    head_lanes = (gate_lane < hd, gate_lane >= hd)
    sel_r = lax.broadcasted_iota(I32, (LANES, 2 * LANES), 0)
    sel_c = lax.broadcasted_iota(I32, (LANES, 2 * LANES), 1)
    sel = [[(sel_r == jnp.where(sel_c < LANES, 4 + d * 2 + hh, d * 2 + hh)).astype(MXU_DTYPE) for hh in range(2)]
           for d in range(2)]
    ones_ext = jnp.ones((L, LANES), MXU_DTYPE)

    def cumulative(lf, left, d):
        hi, lo = _split2(lf)
        mat = tri[d] if left else tri_t[d]
        mm = (lambda x: jnp.dot(mat, x, preferred_element_type=F32)) if left else (
            lambda x: jnp.dot(x, mat, preferred_element_type=F32))
        return mm(hi) + mm(lo)

    chains = [(d, hh) for d in range(2) for hh in range(2)]
    f32dot = functools.partial(jnp.dot, preferred_element_type=F32)

    def chunk_step(it, carry):
        t0s = [pl.multiple_of((it if d == 0 else nc - 1 - it) * L, L) for d in range(2)]
        gpre = [gt_ref[pl.ds(t0, L), :] + gbias for t0 in t0s]
        gpre_t = [gtt_ref[:, pl.ds(t0, L)] for t0 in t0s]
        gm = [jnp.where(is_i, gpre[d], cumulative(jnp.where(is_f, jax.nn.log_sigmoid(gpre[d]), 0.0), True, d))
              for d in range(2)]
        gmt = [jnp.where(is_i_t, gpre_t[d], cumulative(jnp.where(is_f_t, jax.nn.log_sigmoid(gpre_t[d]), 0.0), False, d))
               for d in range(2)]
        gm2 = [_split2(g) for g in gm]
        qc = [qc_ref[pl.ds(t0, L), :] for t0 in t0s]
        kc = [kc_ref[pl.ds(t0, L), :].astype(MXU_DTYPE) for t0 in t0s]
        kt = [kt_ref[:, pl.ds(t0, L)].astype(MXU_DTYPE) for t0 in t0s]
        vc = [v_ref[pl.ds(t0, L), :] for t0 in t0s]
        v_ext = [jnp.concatenate([v.astype(MXU_DTYPE), ones_ext], axis=1) for v in vc]

        rep = [f32dot(gm2[d][0], sel[d][hh]) + f32dot(gm2[d][1], sel[d][hh]) for d, hh in chains]
        b_rep = [r[:, :LANES] for r in rep]
        li_rep = [r[:, LANES:] for r in rep]
        li_row = [gmt[d][d * 2 + hh:d * 2 + hh + 1, :] for d, hh in chains]
        b_row = [gmt[d][4 + d * 2 + hh:5 + d * 2 + hh, :] for d, hh in chains]
        m_st = [m_ref[hh * 2 + d] for d, hh in chains]
        cn = [cn_ref[hh * 2 + d] for d, hh in chains]
        qm = [jnp.where(head_lanes[hh], qc[d], 0.0).astype(MXU_DTYPE) for d, hh in chains]
        s_qk = [lax.dot_general(qm[c], kc[d], (((1,), (1,)), ((), ())), preferred_element_type=F32)
                for c, (d, hh) in enumerate(chains)]
        qcn = [f32dot(qm[c], cn[c].astype(MXU_DTYPE)) for c in range(4)]

        dmat = [jnp.where(keep[d], b_rep[c] - b_row[c] + li_row[c], NEG_INF) for c, (d, hh) in enumerate(chains)]
        inter = [b_rep[c] + m_st[c] for c in range(4)]
        m_j = [jnp.maximum(inter[c], jnp.max(dmat[c], axis=-1, keepdims=True)) for c in range(4)]
        a_mat = [jnp.exp(dmat[c] - m_j[c]) * s_qk[c] for c in range(4)]
        a_int = [jnp.exp(inter[c] - m_j[c]) for c in range(4)]
        av = [f32dot(a_mat[c].astype(MXU_DTYPE), v_ext[d]) for c, (d, hh) in enumerate(chains)]
        num = [av[c][:, :LANES] + a_int[c] * qcn[c][:, :LANES] for c in range(4)]
        den = [av[c][:, LANES:] + a_int[c] * qcn[c][:, LANES:] for c in range(4)]
        h_out = [num[c] / jnp.maximum(jnp.abs(den[c]), jnp.exp(-m_j[c])) for c in range(4)]

        b_last = [b_rep[c][(L - 1 if d == 0 else 0):(L if d == 0 else 1), :] for c, (d, hh) in enumerate(chains)]
        g_row = [b_last[c] - b_row[c] + li_row[c] for c in range(4)]
        g_rep = [b_last[c] - b_rep[c] + li_rep[c] for c in range(4)]
        m_new = [jnp.maximum(b_last[c] + m_st[c], jnp.max(g_row[c], axis=-1, keepdims=True)) for c in range(4)]
        w_rep = [jnp.exp(g_rep[c] - m_new[c]) for c in range(4)]
        decay = [jnp.exp(b_last[c] + m_st[c] - m_new[c]) for c in range(4)]
        wv_ext = [jnp.concatenate([w_rep[c] * vc[d].astype(F32), w_rep[c]], axis=1).astype(MXU_DTYPE)
                  for c, (d, hh) in enumerate(chains)]
        upd = [f32dot(kt[d], wv_ext[c]) for c, (d, hh) in enumerate(chains)]
        for c, (d, hh) in enumerate(chains):
            cn_ref[hh * 2 + d] = jnp.concatenate([decay[c], decay[c]], axis=1) * cn[c] + upd[c]
            m_ref[hh * 2 + d] = m_new[c]
        hf_ref[pl.ds(t0s[0], L), :] = jnp.where(head_lanes[0], h_out[0], h_out[1])
        hb_ref[pl.ds(t0s[1], L), :] = jnp.where(head_lanes[0], h_out[2], h_out[3])
        return carry

    lax.fori_loop(0, nc, chunk_step, 0)

    ng = ng_ref[...]
    lane = lax.broadcasted_iota(I32, (L, LANES), 1)
    upper = lane >= hd

    def out_chunk(ci, carry):
        t0s = [pl.multiple_of((ci * conv_unroll + r) * L, L) for r in range(conv_unroll)]
        hs = [hf_ref[pl.ds(t0, L), :] + hb_ref[pl.ds(t0, L), :] for t0 in t0s]
        sq = [h * h for h in hs]
        ms_lo = [jnp.sum(jnp.where(upper, 0.0, s), axis=-1, keepdims=True) / hd for s in sq]
        ms_hi = [jnp.sum(jnp.where(upper, s, 0.0), axis=-1, keepdims=True) / hd for s in sq]
        ys = [h * lax.rsqrt(jnp.where(upper, hi, lo) + EPS) * ng for h, hi, lo in zip(hs, ms_hi, ms_lo)]
        for t0, y in zip(t0s, ys):
            y_ref[pl.ds(t0, L), :] = (y * jax.nn.sigmoid(o_ref[pl.ds(t0, L), :].astype(F32))).astype(y_ref.dtype)
        return carry

    lax.fori_loop(0, nc // conv_unroll, out_chunk, 0)


def _mlstm(proj3, gates3, conv_w, gate_bias_row, norm_g_row, n_heads, cols):
    b, seq, _ = proj3.shape
    qb, kb, vb, ob = cols
    n_pairs = n_heads // 2
    width = n_heads * HEAD_DIM
    blk = lambda base: pl.BlockSpec((None, seq, LANES), lambda bi, p, base=base: (bi, 0, base + p))
    k_conv_base = width // LANES
    return pl.pallas_call(
        functools.partial(_mlstm_body, seq=seq, n_heads=n_heads), grid=(b, n_pairs),
        in_specs=[blk(qb), blk(kb), blk(vb), blk(ob),
                  pl.BlockSpec((None, seq, LANES), lambda bi, p: (bi, 0, p)),
                  pl.BlockSpec((MLSTM_CONV, LANES), lambda bi, p: (0, p)),
                  pl.BlockSpec((MLSTM_CONV, LANES), lambda bi, p: (0, k_conv_base + p)),
                  pl.BlockSpec((1, LANES), lambda bi, p: (0, p)),
                  pl.BlockSpec((1, LANES), lambda bi, p: (0, p))],
        out_specs=pl.BlockSpec((None, seq, LANES), lambda bi, p: (bi, 0, p)),
        out_shape=jax.ShapeDtypeStruct((b, seq, width), MXU_DTYPE),
        scratch_shapes=[pltpu.VMEM((seq + 2 * CONV_HALO, LANES), F32), pltpu.VMEM((seq + 2 * CONV_HALO, LANES), F32),
                        pltpu.VMEM((seq, LANES), F32), pltpu.VMEM((seq, LANES), F32),
                        pltpu.VMEM((LANES, seq), F32), pltpu.VMEM((LANES, seq), F32),
                        pltpu.VMEM((seq, LANES), F32), pltpu.VMEM((seq, LANES), F32),
                        pltpu.VMEM((4, LANES, 2 * LANES), F32),
                        pltpu.VMEM((4, 1, LANES), F32)],
        compiler_params=_cparams("parallel", "parallel"), name="mlstm",
    )(proj3, proj3, proj3, proj3, gates3, conv_w, conv_w, gate_bias_row, norm_g_row)


def _group_mean_sq(x, gmat):
    hi, lo = _split2(x * x)
    ssq = jnp.dot(hi, gmat, preferred_element_type=F32) + jnp.dot(lo, gmat, preferred_element_type=F32)
    return ssq * (1.0 / HEAD_DIM)


def _norm_rope(x, g_row, gmat, cos_t, sin_t):
    width = x.shape[1]
    xn = x * lax.rsqrt(_group_mean_sq(x, gmat) + EPS) * g_row
    lane = lax.broadcasted_iota(I32, x.shape, 1)
    partner = jnp.where(lane % 2 == 0, pltpu.roll(xn, width - 1, 1), pltpu.roll(xn, 1, 1))
    return xn * cos_t + partner * sin_t


def _qkprep_body(q_ref, k_ref, v_ref, qg_ref, kg_ref, gq_ref, gk_ref, cos_ref, sin_ref, qo_ref, kt_ref, vo_ref,
                 *, n_q_heads, n_kv_heads):
    hd = HEAD_DIM
    cos2, sin2 = cos_ref[...], sin_ref[...]
    reps = n_q_heads * hd // LANES
    q = _norm_rope(q_ref[...].astype(F32), qg_ref[...], gq_ref[...],
                   jnp.concatenate([cos2] * reps, axis=1), jnp.concatenate([sin2] * reps, axis=1))
    q = q * (hd ** -0.5 * LOG2_E)
    for h in range(n_q_heads):
        qo_ref[h] = q[:, h * hd:(h + 1) * hd].astype(qo_ref.dtype)
    k = _norm_rope(k_ref[...].astype(F32), kg_ref[...], gk_ref[...], cos2, sin2)
    kt = k.T
    v = v_ref[...].astype(F32)
    kv_width = v.shape[1]
    lane = lax.broadcasted_iota(I32, v.shape, 1)
    for h in range(n_kv_heads):
        kt_ref[h] = kt[h * hd:(h + 1) * hd, :].astype(kt_ref.dtype)
        vh = v if h == 0 else pltpu.roll(v, kv_width - h * hd, 1)
        vo_ref[h] = jnp.where(lane < hd, vh, jnp.where(lane == hd, 1.0, 0.0)).astype(vo_ref.dtype)


def _qkprep(proj3, q_gain_row, k_gain_row, gq, gk, cos2, sin2, n_q_heads, n_kv_heads, cols, ts):
    b, seq, _ = proj3.shape
    qcol, kcol, vcol = cols
    qw, kw = n_q_heads * HEAD_DIM, n_kv_heads * HEAD_DIM
    fixed = lambda bi, i: (0, 0)
    return pl.pallas_call(
        functools.partial(_qkprep_body, n_q_heads=n_q_heads, n_kv_heads=n_kv_heads), grid=(b, seq // ts),
        in_specs=[pl.BlockSpec((None, ts, qw), lambda bi, i: (bi, i, qcol)),
                  pl.BlockSpec((None, ts, kw), lambda bi, i: (bi, i, kcol)),
                  pl.BlockSpec((None, ts, kw), lambda bi, i: (bi, i, vcol)),
                  pl.BlockSpec((1, qw), fixed), pl.BlockSpec((1, kw), fixed),
                  pl.BlockSpec((qw, qw), fixed), pl.BlockSpec((kw, kw), fixed),
                  pl.BlockSpec((ts, kw), lambda bi, i: (i, 0)), pl.BlockSpec((ts, kw), lambda bi, i: (i, 0))],
        out_specs=[pl.BlockSpec((None, n_q_heads, ts, HEAD_DIM), lambda bi, i: (bi, 0, i, 0)),
                   pl.BlockSpec((None, n_kv_heads, HEAD_DIM, ts), lambda bi, i: (bi, 0, 0, i)),
                   pl.BlockSpec((None, n_kv_heads, ts, kw), lambda bi, i: (bi, 0, i, 0))],
        out_shape=[jax.ShapeDtypeStruct((b, n_q_heads, seq, HEAD_DIM), MXU_DTYPE),
                   jax.ShapeDtypeStruct((b, n_kv_heads, HEAD_DIM, seq), MXU_DTYPE),
                   jax.ShapeDtypeStruct((b, n_kv_heads, seq, kw), MXU_DTYPE)],
        compiler_params=_cparams("parallel", "parallel"), name="qk_prep",
    )(proj3, proj3, proj3, q_gain_row, k_gain_row, gq, gk, cos2, sin2)


def _flash_body(q_ref, kt_ref, v_ref, o_ref, *, seq, tq, tk, group):
    hd = HEAD_DIM
    rows = group * tq
    q = q_ref[...].reshape(rows, hd)
    m = jnp.full((rows, 1), NEG_INF, F32)
    acc = jnp.zeros((rows, v_ref.shape[1]), F32)
    for kc in range(seq // tk):
        s = jnp.dot(q, kt_ref[:, kc * tk:(kc + 1) * tk], preferred_element_type=F32)
        m_new = jnp.maximum(m, jnp.max(s, axis=-1, keepdims=True))
        p = jnp.exp2(s - m_new)
        acc = jnp.exp2(m - m_new) * acc + jnp.dot(p.astype(v_ref.dtype), v_ref[kc * tk:(kc + 1) * tk, :],
                                                  preferred_element_type=F32)
        m = m_new
    o = acc[:, :hd] / acc[:, hd:hd + 1]
    o_ref[...] = jnp.concatenate([o[g * tq:(g + 1) * tq] for g in range(group)], axis=1).astype(o_ref.dtype)


def _flash_attention(q4, kt4, v4, tq, tk):
    b, n_q_heads, seq, hd = q4.shape
    n_kv_heads = kt4.shape[1]
    group = n_q_heads // n_kv_heads
    return pl.pallas_call(
        functools.partial(_flash_body, seq=seq, tq=tq, tk=tk, group=group), grid=(b, n_kv_heads, seq // tq),
        in_specs=[pl.BlockSpec((None, group, tq, hd), lambda bi, h, i: (bi, h, i, 0)),
                  pl.BlockSpec((None, None, hd, seq), lambda bi, h, i: (bi, h, 0, 0)),
                  pl.BlockSpec((None, None, seq, v4.shape[3]), lambda bi, h, i: (bi, h, 0, 0))],
        out_specs=pl.BlockSpec((None, tq, group * hd), lambda bi, h, i: (bi, i, h)),
        out_shape=jax.ShapeDtypeStruct((b, seq, n_q_heads * hd), MXU_DTYPE),
        compiler_params=_cparams("parallel", "parallel", "parallel"), name="flash_gqa",
    )(q4, kt4, v4)


def _kvproj_body(m_ref, w_ref, o_ref):
    o_ref[...] = _mm(m_ref[...], w_ref[...]).astype(o_ref.dtype)


def _kvproj(mem2d, wkv, tm):
    t, d = mem2d.shape
    n = wkv.shape[1]
    return pl.pallas_call(
        _kvproj_body, grid=(t // tm,),
        in_specs=[pl.BlockSpec((tm, d), lambda i: (i, 0)), pl.BlockSpec((d, n), lambda i: (0, 0))],
        out_specs=pl.BlockSpec((tm, n), lambda i: (i, 0)), out_shape=jax.ShapeDtypeStruct((t, n), MXU_DTYPE),
        compiler_params=_cparams("parallel"), name="mem_kv_proj",
    )(mem2d, wkv)


def _crossattn_body(x_ref, yp_ref, ym_ref, ya_ref, wp_ref, wm_ref, wa_ref, kv_ref, gm_ref, wq_ref, wo_ref, gf_ref,
                    rwt_ref, xo_ref, hf_ref, lg_ref, *, ts):
    x = (x_ref[...] + _mm(yp_ref[...], wp_ref[...]) + _mm(ym_ref[...], wm_ref[...])
         + _mm(ya_ref[...], wa_ref[...]))
    q = _mm(_rms(x, gm_ref[...]), wq_ref[...])
    ca_width = q.shape[1]
    dh = ca_width // CA_HEADS
    kv = kv_ref[...]
    hs = range(CA_HEADS)
    s = [_mm_nt(q[:, h * dh:(h + 1) * dh], kv[:, h * dh:(h + 1) * dh]) * (dh ** -0.5) for h in hs]
    e = [jnp.exp(s[h] - jnp.max(s[h], axis=-1, keepdims=True)) for h in hs]
    p = [e[h] / jnp.sum(e[h], axis=-1, keepdims=True) for h in hs]
    heads = [_mm(p[h], kv[:, ca_width + h * dh:ca_width + (h + 1) * dh]) for h in hs]
    x2 = x + _mm(jnp.concatenate(heads, axis=1), wo_ref[...])
    xo_ref[...] = x2
    hf = _rms(x2, gf_ref[...])
    _to_token_tiles(hf_ref, hf, ts)
    lg_ref[...] = _mm_nt(rwt_ref[...], hf)


def _crossattn(x3, yp, ym, ya, w_out, kv3, g_mem, wq, wo, g_ffn, router_wt, ts):
    b, seq, d = x3.shape
    n_mem, kvw = kv3.shape[1], kv3.shape[2]
    ne = router_wt.shape[0]
    fixed = lambda bi, i: (0, 0)
    tile = lambda w: pl.BlockSpec((None, ts, w), lambda bi, i: (bi, i, 0))
    wp, wm, wa = yp.shape[2], ym.shape[2], ya.shape[2]
    return pl.pallas_call(
        functools.partial(_crossattn_body, ts=ts), grid=(b, seq // ts),
        in_specs=[tile(d), tile(wp), tile(wm), tile(wa),
                  pl.BlockSpec((wp, d), fixed), pl.BlockSpec((wm, d), fixed), pl.BlockSpec((wa, d), fixed),
                  pl.BlockSpec((None, n_mem, kvw), lambda bi, i: (bi, 0, 0)),
                  pl.BlockSpec((1, d), fixed), pl.BlockSpec(wq.shape, fixed), pl.BlockSpec(wo.shape, fixed),
                  pl.BlockSpec((1, d), fixed), pl.BlockSpec((ne, d), fixed)],
        out_specs=[pl.BlockSpec((None, ts, d), lambda bi, i: (bi, i, 0)),
                   pl.BlockSpec((None, ts * SUBLANES, LANES), lambda bi, i: (bi, i, 0)),
                   pl.BlockSpec((None, ne, ts), lambda bi, i: (bi, 0, i))],
        out_shape=[jax.ShapeDtypeStruct((b, seq, d), F32),
                   jax.ShapeDtypeStruct((b, seq * SUBLANES, LANES), F32),
                   jax.ShapeDtypeStruct((b, ne, seq), F32)],
        compiler_params=_cparams("parallel", "parallel"), name="cross_attn",
    )(x3, yp, ym, ya, w_out[:wp], w_out[wp:wp + wm], w_out[wp + wm:], kv3, g_mem, wq, wo, g_ffn, router_wt)


def _excl_prefix(flags, upper_excl, ones):
    rows, seq = flags.shape
    off = jnp.zeros((rows, LANES), F32)
    out = []
    for c in range(seq // LANES):
        xc = flags[:, c * LANES:(c + 1) * LANES].astype(MXU_DTYPE)
        out.append(jnp.dot(xc, upper_excl, preferred_element_type=F32) + off)
        off = off + jnp.dot(xc, ones, preferred_element_type=F32)
    return jnp.concatenate(out, axis=1)


def _router_body(lg_ref, idx_ref, gate_ref, *, seq, cap):
    lg = lg_ref[...]
    ne = lg.shape[0]
    ex = jnp.exp(lg - jnp.max(lg, axis=0, keepdims=True))
    aff = ex / jnp.sum(ex, axis=0, keepdims=True)
    bits = pltpu.bitcast(aff, I32)

    thr = jnp.zeros((ne, 1), I32)
    for bit in range(30, -1, -1):
        cand = thr | (1 << bit)
        cnt = jnp.sum((bits >= cand).astype(F32), axis=1, keepdims=True)
        thr = jnp.where(cnt >= cap, cand, thr)

    r_i = lax.broadcasted_iota(I32, (LANES, LANES), 0)
    c_i = lax.broadcasted_iota(I32, (LANES, LANES), 1)
    upper_excl = (r_i < c_i).astype(MXU_DTYPE)
    ones = jnp.ones((LANES, LANES), MXU_DTYPE)

    gt = bits > thr
    eq = bits == thr
    n_gt = jnp.sum(gt.astype(F32), axis=1, keepdims=True)
    tie_rank = _excl_prefix(eq.astype(F32), upper_excl, ones)
    sel = gt | (eq & (tie_rank < cap - n_gt))
    pos = _excl_prefix(sel.astype(F32), upper_excl, ones).astype(I32)

    n_iota = lax.broadcasted_iota(I32, (ne, seq), 1)
    seq_bits = (seq - 1).bit_length()
    flag_bit = 2 * seq_bits
    packed = jnp.where(sel, n_iota | ((n_iota - pos) << seq_bits) | (1 << flag_bit), 0)
    gbits = jnp.where(sel, bits, 0)
    for k in range(seq_bits):
        sh = 1 << k
        moved_p = pltpu.roll(packed, seq - sh, 1)
        moved_g = pltpu.roll(gbits, seq - sh, 1)
        take = ((moved_p >> flag_bit) & 1) * ((moved_p >> (seq_bits + k)) & 1) == 1
        stay = ((packed >> flag_bit) & 1) * (1 - ((packed >> (seq_bits + k)) & 1)) == 1
        packed = jnp.where(take, moved_p, jnp.where(stay, packed, 0))
        gbits = jnp.where(take, moved_g, jnp.where(stay, gbits, 0))
    idx_ref[...] = packed[:, :cap] & (seq - 1)
    gate_ref[...] = pltpu.bitcast(gbits[:, :cap], F32)


def _router(logits_t, cap):
    b, ne, seq = logits_t.shape
    return pl.pallas_call(
        functools.partial(_router_body, seq=seq, cap=cap), grid=(b,),
        in_specs=[pl.BlockSpec((None, ne, seq), lambda bi: (bi, 0, 0))],
        out_specs=[pl.BlockSpec((None, ne, cap), lambda bi: (bi, 0, 0)),
                   pl.BlockSpec((None, ne, cap), lambda bi: (bi, 0, 0))],
        out_shape=[jax.ShapeDtypeStruct((b, ne, cap), I32), jax.ShapeDtypeStruct((b, ne, cap), F32)],
        compiler_params=_cparams("parallel"), name="ec_router",
    )(logits_t)


def _moe_body(*refs, seq, cap, ne, n_fc, stride, b_off, cast, chained):
    idx_hbm, gate_hbm, h_hbm, wg_ref, wu_ref, wd_ref = refs[:6]
    prev_hbm = refs[6] if chained else None
    refs = refs[6 + (1 if chained else 0):]
    out_hbm = refs[0]
    w_outs = refs[1:4] if cast else ()
    idx_s, gate_s, xbuf, xbf, ybuf, yacc, acc, gsem, isem, osem, psem = refs[1 + len(w_outs):]

    def prev_copy():
        return pltpu.make_async_copy(prev_hbm, out_hbm.at[pl.ds(0, b_off)], psem)
    b = pl.program_id(0)
    e = pl.program_id(1)
    c = pl.program_id(2)
    last = pl.num_programs(0) * ne - 1
    step = b * ne + e
    nstep = jnp.minimum(step + 1, last)
    xslot, nxslot = lax.rem(step, 2), lax.rem(step + 1, 2)
    islot, nislot = lax.rem(step, N_IDX_SLOTS), lax.rem(step + 1, N_IDX_SLOTS)
    tile = SUBLANES
    per = cap // n_fc

    def idx_copies(s, sl):
        src = pl.ds(pl.multiple_of((b_off * ne + s) * stride, stride), stride)
        dst = pl.ds(pl.multiple_of(sl * stride, stride), stride)
        return (pltpu.make_async_copy(idx_hbm.at[src], idx_s.at[dst], isem.at[0, sl]),
                pltpu.make_async_copy(gate_hbm.at[src], gate_s.at[dst], isem.at[1, sl]))

    def row_copy(src_row, k, sl):
        return pltpu.make_async_copy(h_hbm.at[pl.ds(src_row, tile), :],
                                     xbuf.at[sl, pl.ds(k * tile, tile), :], gsem.at[sl])

    def start_rows(s, isl, sl, k0, n):
        base = (b_off + s // ne) * seq
        ibase = isl * stride + k0
        for k in range(n):
            src_row = pl.multiple_of((base + idx_s[ibase + k]) * tile, tile)
            row_copy(src_row, k0 + k, sl).start(priority=GATHER_DMA_PRIORITY)

    def wait_gather(sl):
        pltpu.make_async_copy(h_hbm.at[pl.ds(0, cap * tile), :], xbuf.at[sl], gsem.at[sl]).wait()

    @pl.when(c == 0)
    def _():
        @pl.when(step == 0)
        def _():
            for cp in idx_copies(0, 0):
                cp.start()
            for cp in idx_copies(0, 0):
                cp.wait()
            for cp in idx_copies(jnp.minimum(1, last), 1):
                cp.start()
            if chained:
                prev_copy().start()

            def body(j, carry):
                start_rows(0, 0, 0, j * SUBLANES, SUBLANES)
                return carry
            lax.fori_loop(0, cap // SUBLANES, body, 0)

        for cp in idx_copies(0, nislot):
            cp.wait()
        for cp in idx_copies(jnp.minimum(step + 2, last), lax.rem(step + 2, N_IDX_SLOTS)):
            cp.start()
        wait_gather(xslot)
        xbf[...] = _from_token_tiles(xbuf.at[xslot], cap).astype(xbf.dtype)

        @pl.when(e == 0)
        def _():
            acc[...] = jnp.zeros(acc.shape, F32)

    start_rows(nstep, nislot, nxslot, pl.multiple_of(c * per, per), per)

    x = xbf[...]
    fc = wg_ref.shape[1]
    sub = min(fc, MOE_SUB_COLS)
    y = None
    for j in range(fc // sub):
        cols = slice(j * sub, (j + 1) * sub)
        wg, wu, wd = wg_ref[:, cols], wu_ref[:, cols], wd_ref[cols, :]
        if cast:
            wg, wu, wd = wg.astype(MXU_DTYPE), wu.astype(MXU_DTYPE), wd.astype(MXU_DTYPE)
            w_outs[0][:, cols] = wg
            w_outs[1][:, cols] = wu
            w_outs[2][cols, :] = wd
        g = jnp.dot(x, wg, preferred_element_type=F32)
        u = jnp.dot(x, wu, preferred_element_type=F32)
        yj = _mm(g * jax.nn.sigmoid(g) * u, wd)
        y = yj if y is None else y + yj

    if n_fc == 1:
        _to_token_tiles(ybuf, y, cap)
    else:
        @pl.when(c == 0)
        def _():
            yacc[...] = y

        @pl.when(c > 0)
        def _():
            yacc[...] += y

    @pl.when(c == n_fc - 1)
    def _():
        if n_fc > 1:
            _to_token_tiles(ybuf, yacc[...], cap)

        ibase = islot * stride
        for k0 in range(0, cap, SCATTER_GROUP):
            rows = [pl.multiple_of(idx_s[ibase + k0 + j] * tile, tile) for j in range(SCATTER_GROUP)]
            new = [acc[pl.ds(rows[j], tile), :] + gate_s[ibase + k0 + j] * ybuf[(k0 + j) * tile:(k0 + j + 1) * tile, :]
                   for j in range(SCATTER_GROUP)]
            for j in range(SCATTER_GROUP):
                acc[pl.ds(rows[j], tile), :] = new[j]

        @pl.when(e == ne - 1)
        def _():
            cp = pltpu.make_async_copy(acc, out_hbm.at[b_off + b], osem)
            cp.start()
            cp.wait()

        @pl.when(step == last)
        def _():
            wait_gather(nxslot)
            for cp in idx_copies(0, lax.rem(step + 2, N_IDX_SLOTS)):
                cp.wait()
            if chained:
                prev_copy().wait()


def _moe(idx, gate, h_tiles, w_gate, w_up, w_down, up_block_off, seq, fc, b_off, n_b, prev_out, cast, layer=None):
    _, ne, cap = idx.shape
    d, f = w_down.shape[-1], w_down.shape[-2]
    n_fc = f // fc
    lead = () if layer is None else (layer,)
    w_spec = lambda shape, imap: pl.BlockSpec((None,) * (len(lead) + 1) + shape,
                                               lambda b, e, c: lead + imap(e, c))
    chained = prev_out is not None
    stride = -(-cap // SMEM_1D_TILE) * SMEM_1D_TILE
    flat = lambda a: jnp.pad(a, ((0, 0), (0, 0), (0, stride - cap))).reshape(-1)
    any_spec = pl.BlockSpec(memory_space=pl.ANY)
    assert chained == (b_off > 0)
    out_shape = [jax.ShapeDtypeStruct((b_off + n_b, seq * SUBLANES, LANES), F32)]
    out_specs = [any_spec]
    if cast:
        out_shape += [jax.ShapeDtypeStruct((ne, d, f), MXU_DTYPE), jax.ShapeDtypeStruct((ne, d, f), MXU_DTYPE),
                      jax.ShapeDtypeStruct((ne, f, d), MXU_DTYPE)]
        out_specs += [pl.BlockSpec((None, d, fc), lambda b, e, c: (e, 0, c)),
                      pl.BlockSpec((None, d, fc), lambda b, e, c: (e, 0, c)),
                      pl.BlockSpec((None, fc, d), lambda b, e, c: (e, c, 0))]
    return pl.pallas_call(
        functools.partial(_moe_body, seq=seq, cap=cap, ne=ne, n_fc=n_fc, stride=stride, b_off=b_off, cast=cast,
                          chained=chained),
        grid=(n_b, ne, n_fc),
        in_specs=[any_spec, any_spec, any_spec,
                  w_spec((d, fc), lambda e, c: (e, 0, c)),
                  w_spec((d, fc), lambda e, c: (e, 0, up_block_off + c)),
                  w_spec((fc, d), lambda e, c: (e, c, 0))] + ([any_spec] if chained else []),
        out_specs=out_specs, out_shape=out_shape,
        scratch_shapes=[pltpu.SMEM((N_IDX_SLOTS * stride,), I32), pltpu.SMEM((N_IDX_SLOTS * stride,), F32),
                        pltpu.VMEM((2, cap * SUBLANES, LANES), F32), pltpu.VMEM((cap, d), MXU_DTYPE),
                        pltpu.VMEM((cap * SUBLANES, LANES), F32), pltpu.VMEM((cap, d), F32),
                        pltpu.VMEM((seq * SUBLANES, LANES), F32),
                        pltpu.SemaphoreType.DMA((2,)), pltpu.SemaphoreType.DMA((2, N_IDX_SLOTS)),
                        pltpu.SemaphoreType.DMA, pltpu.SemaphoreType.DMA],
        compiler_params=_cparams("arbitrary", "arbitrary", "arbitrary"),
        name="ec_moe_cast" if cast else "ec_moe",
    )(flat(idx), flat(gate), h_tiles, w_gate, w_up, w_down, *([prev_out] if chained else []))


def _final_body(x_ref, moe_ref, g_ref, o_ref, *, tm):
    o_ref[...] = _rms(x_ref[...] + _from_token_tiles(moe_ref, tm), g_ref[...])


def _final_norm(x2d, moe_tiles, g, tm):
    t, d = x2d.shape
    row = lambda i: (i, 0)
    return pl.pallas_call(
        functools.partial(_final_body, tm=tm), grid=(t // tm,),
        in_specs=[pl.BlockSpec((tm, d), row), pl.BlockSpec((tm * SUBLANES, LANES), row),
                  pl.BlockSpec((1, d), lambda i: (0, 0))],
        out_specs=pl.BlockSpec((tm, d), row), out_shape=jax.ShapeDtypeStruct((t, d), F32),
        compiler_params=_cparams("parallel"), name="final_norm",
    )(x2d, moe_tiles, g)


def _rope_tables(seq):
    rows = seq // GRID_W
    row_ids = jnp.repeat(jnp.arange(rows), GRID_W).astype(F32)
    col_ids = jnp.tile(jnp.arange(GRID_W), rows).astype(F32)
    n_freq = HEAD_DIM // 4
    inv_freq = ROPE_THETA ** (-jnp.arange(n_freq, dtype=F32) / n_freq)
    ang = jnp.concatenate([row_ids[:, None] * inv_freq, col_ids[:, None] * inv_freq], axis=-1)
    cos, sin = jnp.cos(ang), jnp.sin(ang)
    cos_i = jnp.repeat(cos, 2, axis=1)
    sin_i = jnp.stack([-sin, sin], axis=-1).reshape(seq, HEAD_DIM)
    return jnp.tile(cos_i, (1, LANES // HEAD_DIM)), jnp.tile(sin_i, (1, LANES // HEAD_DIM))


def _group_matrix(width):
    gid = jnp.arange(width) // HEAD_DIM
    return (gid[:, None] == gid[None, :]).astype(MXU_DTYPE)


def _block_diag(w):
    g, c, _ = w.shape
    eye = jnp.eye(g, dtype=w.dtype)
    return (eye[:, None, :, None] * w[:, :, None, :]).reshape(g * c, g * c)


def _pick_tile(n, pref):
    t = min(n, pref)
    while n % t:
        t //= 2
    return t


def kernel(x, mem, norm_mix_g, w_in, pool_w, pool_scale, mlstm_conv_w, mlstm_gate_b, mlstm_norm_g, q_norm_g,
           k_norm_g, w_out, norm_mem_g, ca_wq, ca_wkv, ca_wo, norm_ffn_g, router_w, expert_w_gu, expert_w_down,
           final_norm_g):
    batch, seq, d = x.shape
    depth = w_in.shape[0]
    n_mem = mem.shape[1]
    pool_width = pool_scale.shape[1]
    m_width = mlstm_norm_g.shape[1]
    m_heads = m_width // HEAD_DIM
    n_gate = 2 * m_heads
    kv_width = (w_in.shape[2] - pool_width - 4 * m_width - 2 * n_gate) // 6
    q_width = 4 * kv_width
    n_q_heads, n_kv_heads = q_width // HEAD_DIM, kv_width // HEAD_DIM
    ne = router_w.shape[2]
    cap = EC_CAPACITY_FACTOR * seq // ne
    f_exp = expert_w_down.shape[2]
    t = batch * seq

    o_pool = 0
    o_mq = o_pool + pool_width
    o_mi = o_mq + 4 * m_width
    o_aq = o_mi + 2 * n_gate
    o_ak = o_aq + q_width
    c_pool = q_width // pool_width
    c_mq = (q_width + pool_width) // LANES
    m_cols = tuple(c_mq + j * (m_width // LANES) for j in range(4))
    c_ak = (q_width + pool_width + 4 * m_width) // kv_width
    cos2, sin2 = _rope_tables(seq)
    gq, gk = _group_matrix(q_width), _group_matrix(kv_width)
    n_pairs = m_heads // 2
    gate_cols = [[kind * n_gate + dr * m_heads + 2 * p + hh for kind in range(2) for dr in range(2) for hh in range(2)]
                 for p in range(n_pairs)]

    assert kv_width == LANES and MLSTM_CHUNK == LANES and m_heads % 2 == 0
    tm = _pick_tile(t, TOKEN_TILE)
    ts = _pick_tile(seq, TOKEN_TILE)
    x2d = x.reshape(t, d)
    mem2d = mem.reshape(batch * n_mem, d)
    moe_tiles = None
    for layer in range(depth):
        wl = w_in[layer]
        w_main = jnp.concatenate([wl[:, o_aq:o_ak], wl[:, o_pool:o_mi], wl[:, o_ak:]], axis=1).astype(MXU_DTYPE)
        w_gate = jnp.concatenate(
            [jnp.pad(wl[:, o_mi:o_aq][:, jnp.array(cols)], ((0, 0), (0, LANES - len(cols)))) for cols in gate_cols],
            axis=1).astype(MXU_DTYPE)
        gb_row = jnp.concatenate(
            [jnp.pad(mlstm_gate_b[layer][jnp.array(cols)], (0, LANES - len(cols))) for cols in gate_cols])[None]
        x2d, proj, gates = _inproj(x2d, moe_tiles, norm_mix_g[layer][None], w_main, w_gate, tm)
        proj3 = proj.reshape(batch, seq, proj.shape[1])
        gates3 = gates.reshape(batch, seq, gates.shape[1])

        y_pool = _pool_mixer(proj3, _block_diag(pool_w[layer]).astype(MXU_DTYPE), pool_scale[layer][None], c_pool,
                             _pick_tile(seq, POOL_TILE))
        y_mlstm = _mlstm(proj3, gates3, mlstm_conv_w[layer], gb_row, mlstm_norm_g[layer][None], m_heads, m_cols)
        q4, kt4, v4 = _qkprep(proj3, jnp.tile(q_norm_g[layer], n_q_heads)[None],
                              jnp.tile(k_norm_g[layer], n_kv_heads)[None], gq, gk, cos2, sin2,
                              n_q_heads, n_kv_heads, (0, c_ak, c_ak + 1), ts)
        y_attn = _flash_attention(q4, kt4, v4, _pick_tile(seq, FLASH_Q_TILE), _pick_tile(seq, FLASH_KV_TILE))

        kv = _kvproj(mem2d, ca_wkv[layer].astype(MXU_DTYPE), _pick_tile(batch * n_mem, TOKEN_TILE))
        x3, h_tiles, logits_t = _crossattn(
            x2d.reshape(batch, seq, d), y_pool, y_mlstm, y_attn, w_out[layer].astype(MXU_DTYPE),
            kv.reshape(batch, n_mem, kv.shape[1]), norm_mem_g[layer][None],
            ca_wq[layer].astype(MXU_DTYPE), ca_wo[layer].astype(MXU_DTYPE), norm_ffn_g[layer][None],
            router_w[layer].T.astype(MXU_DTYPE), ts)
        x2d = x3.reshape(t, d)

        idx, gate = _router(logits_t, cap)
        h_flat = h_tiles.reshape(t * SUBLANES, LANES)
        fc_cast = _pick_tile(f_exp, MOE_F_CHUNK_CAST)
        moe_tiles, wg_c, wu_c, wd_c = _moe(idx, gate, h_flat, expert_w_gu, expert_w_gu, expert_w_down,
                                           f_exp // fc_cast, seq, fc_cast, 0, 1, None, True, layer)
        if batch > 1:
            moe_tiles, = _moe(idx, gate, h_flat, wg_c, wu_c, wd_c, 0, seq, _pick_tile(f_exp, MOE_F_CHUNK), 1,
                              batch - 1, moe_tiles, False)
        moe_tiles = moe_tiles.reshape(t * SUBLANES, LANES)
    out = _final_norm(x2d, moe_tiles, final_norm_g[None], tm)
    return out.reshape(batch, seq, d)
```

```python
import functools
import math

import jax
import jax.numpy as jnp
from jax import lax
from jax.experimental import pallas as pl
from jax.experimental.pallas import tpu as pltpu

F32 = jnp.float32
I32 = jnp.int32
MXU_DTYPE = jnp.bfloat16

LANES = 128
SUBLANES = 8
VMEM_LIMIT_BYTES = 56 * 1024 * 1024

TOKEN_TILE = 1024
POOL_TILE = 256
FLASH_Q_TILE = 512
FLASH_KV_TILE = 512
MOE_F_CHUNK = 1024
MOE_F_CHUNK_CAST = 512

EPS = 1e-6
GRID_W = 64
HEAD_DIM = 64
POOL_WINDOWS = (2, 4, 8, 16)
POOL_HALO = 8
MLSTM_CHUNK = 128
MLSTM_CONV = 5
CONV_HALO = 8
ROPE_THETA = 10000.0
CA_HEADS = 4
N_EXPERTS = 16
EC_CAPACITY_FACTOR = 2
SMEM_1D_TILE = 1024
N_IDX_SLOTS = 3
MOE_SUB_COLS = 512
GATHER_DMA_PRIORITY = 1
SCATTER_GROUP = 16
NEG_INF = float("-inf")
LOG2_E = math.log2(math.e)


def _cparams(*sem):
    return pltpu.CompilerParams(dimension_semantics=sem, vmem_limit_bytes=VMEM_LIMIT_BYTES)


def _mm(a, b):
    return jnp.dot(a.astype(MXU_DTYPE), b.astype(MXU_DTYPE), preferred_element_type=F32)


def _mm_nt(a, b):
    return lax.dot_general(a.astype(MXU_DTYPE), b.astype(MXU_DTYPE), (((1,), (1,)), ((), ())),
                           preferred_element_type=F32)


def _split2(x):
    hi = x.astype(MXU_DTYPE)
    return hi, (x - hi.astype(F32)).astype(MXU_DTYPE)


def _rms(x, g):
    return x * lax.rsqrt(jnp.mean(x * x, axis=-1, keepdims=True) + EPS) * g


def _from_token_tiles(ref, rows):
    return jnp.concatenate([ref[pl.ds(s, rows, stride=SUBLANES), :] for s in range(SUBLANES)], axis=1)


def _to_token_tiles(ref, val, rows):
    for s in range(SUBLANES):
        ref[pl.ds(s, rows, stride=SUBLANES), :] = val[:, s * LANES:(s + 1) * LANES]


def _inproj_body(x_ref, g_ref, w_ref, wg_ref, o_ref, og_ref):
    hb = _rms(x_ref[...], g_ref[...]).astype(MXU_DTYPE)
    o_ref[...] = jnp.dot(hb, w_ref[...], preferred_element_type=F32).astype(o_ref.dtype)
    og_ref[...] = jnp.dot(hb, wg_ref[...], preferred_element_type=F32)


def _inproj_merge_body(x_ref, moe_ref, g_ref, w_ref, wg_ref, xo_ref, o_ref, og_ref, *, tm):
    x = x_ref[...] + _from_token_tiles(moe_ref, tm)
    xo_ref[...] = x
    hb = _rms(x, g_ref[...]).astype(MXU_DTYPE)
    o_ref[...] = jnp.dot(hb, w_ref[...], preferred_element_type=F32).astype(o_ref.dtype)
    og_ref[...] = jnp.dot(hb, wg_ref[...], preferred_element_type=F32)


def _inproj(x2d, moe_tiles, g, w_main, w_gate, tm):
    t, d = x2d.shape
    n = w_main.shape[1]
    grid = (t // tm,)
    row = lambda i: (i, 0)
    fixed = lambda i: (0, 0)
    gw = w_gate.shape[1]
    w_specs = [pl.BlockSpec((1, d), fixed), pl.BlockSpec((d, n), fixed), pl.BlockSpec((d, gw), fixed)]
    o_specs = [pl.BlockSpec((tm, n), row), pl.BlockSpec((tm, gw), row)]
    o_shapes = [jax.ShapeDtypeStruct((t, n), MXU_DTYPE), jax.ShapeDtypeStruct((t, gw), F32)]
    if moe_tiles is None:
        proj, gates = pl.pallas_call(
            _inproj_body, grid=grid, in_specs=[pl.BlockSpec((tm, d), row)] + w_specs, out_specs=o_specs,
            out_shape=o_shapes, compiler_params=_cparams("parallel"), name="inproj",
        )(x2d, g, w_main, w_gate)
        return x2d, proj, gates
    xo, proj, gates = pl.pallas_call(
        functools.partial(_inproj_merge_body, tm=tm), grid=grid,
        in_specs=[pl.BlockSpec((tm, d), row), pl.BlockSpec((tm * SUBLANES, LANES), row)] + w_specs,
        out_specs=[pl.BlockSpec((tm, d), row)] + o_specs,
        out_shape=[jax.ShapeDtypeStruct((t, d), F32)] + o_shapes,
        compiler_params=_cparams("parallel"), name="inproj_merge",
    )(x2d, moe_tiles, g, w_main, w_gate)
    return xo, proj, gates


def _pool_body(u_ref, w_ref, sc_ref, o_ref, upad_ref, *, seq, tc):
    i = pl.program_id(1)

    @pl.when(i == 0)
    def _():
        zeros = jnp.zeros((POOL_HALO, LANES), F32)
        for hf in range(2):
            upad_ref[hf, 0:POOL_HALO, :] = zeros
            upad_ref[hf, POOL_HALO + seq:POOL_HALO + seq + POOL_HALO, :] = zeros
            upad_ref[hf, POOL_HALO:POOL_HALO + seq, :] = u_ref[:, hf * LANES:(hf + 1) * LANES].astype(F32)

    t0 = pl.multiple_of(i * tc, tc)
    t = t0 + lax.broadcasted_iota(I32, (tc, LANES), 0)
    lane = lax.broadcasted_iota(I32, (tc, LANES), 1)
    upper = lane >= (LANES // 2)
    halves = []
    for hf in range(2):
        w_lo, w_hi = POOL_WINDOWS[2 * hf], POOL_WINDOWS[2 * hf + 1]
        acc = jnp.zeros((tc, LANES), F32)
        for j in range(-(w_hi // 2), w_hi // 2):
            piece = upad_ref[hf, pl.ds(t0 + (POOL_HALO + j), tc), :]
            if -(w_lo // 2) <= j < w_lo // 2:
                acc = acc + piece
            else:
                acc = acc + jnp.where(upper, piece, 0.0)
        half_w = jnp.where(upper, w_hi // 2, w_lo // 2)
        cnt = jnp.minimum(t + half_w, seq) - jnp.maximum(t - half_w, 0)
        centre = upad_ref[hf, pl.ds(t0 + POOL_HALO, tc), :]
        halves.append(acc / cnt.astype(F32) - centre)
    d = jnp.concatenate(halves, axis=1)
    o_ref[...] = (_mm(d, w_ref[...]) * sc_ref[...]).astype(o_ref.dtype)


def _pool_mixer(proj3, w_bd, scale, col_block, tc):
    b, seq, _ = proj3.shape
    pw = w_bd.shape[0]
    return pl.pallas_call(
        functools.partial(_pool_body, seq=seq, tc=tc), grid=(b, seq // tc),
        in_specs=[pl.BlockSpec((None, seq, pw), lambda bi, i: (bi, 0, col_block)),
                  pl.BlockSpec((pw, pw), lambda bi, i: (0, 0)),
                  pl.BlockSpec((1, pw), lambda bi, i: (0, 0))],
        out_specs=pl.BlockSpec((None, tc, pw), lambda bi, i: (bi, i, 0)),
        out_shape=jax.ShapeDtypeStruct((b, seq, pw), MXU_DTYPE),
        scratch_shapes=[pltpu.VMEM((pw // LANES, seq + 2 * POOL_HALO, LANES), F32)],
        compiler_params=_cparams("parallel", "arbitrary"), name="pool_mixer",
    )(proj3, w_bd, scale)


def _mlstm_body(q_ref, k_ref, v_ref, o_ref, gt_ref, cwq_ref, cwk_ref, gb_ref, ng_ref, y_ref,
                qpad_ref, kpad_ref, qc_ref, kc_ref, kt_ref, gtt_ref, hf_ref, hb_ref, cn_ref, m_ref,
                *, seq, n_heads):
    L = MLSTM_CHUNK
    nc = seq // L
    hd = HEAD_DIM
    gbias = gb_ref[...]

    zeros = jnp.zeros((CONV_HALO, LANES), F32)
    for pad_ref, src_ref in ((qpad_ref, q_ref), (kpad_ref, k_ref)):
        pad_ref[0:CONV_HALO, :] = zeros
        pad_ref[CONV_HALO + seq:CONV_HALO + seq + CONV_HALO, :] = zeros
        pad_ref[CONV_HALO:CONV_HALO + seq, :] = src_ref[...].astype(F32)

    conv_unroll = 4 if nc % 4 == 0 else 1

    def conv_chunk(ci, carry):
        t0s = [pl.multiple_of((ci * conv_unroll + r) * L, L) for r in range(conv_unroll)]
        jobs = [(t0, pad_ref, cw_ref) for t0 in t0s for pad_ref, cw_ref in ((qpad_ref, cwq_ref), (kpad_ref, cwk_ref))]
        accs = [jnp.zeros((L, LANES), F32) for _ in jobs]
        for j in range(MLSTM_CONV):
            off = j - MLSTM_CONV // 2
            pieces = [pad_ref[pl.ds(t0 + (CONV_HALO + off), L), :] for t0, pad_ref, _ in jobs]
            accs = [acc + piece * cw_ref[j:j + 1, :] for acc, piece, (_, _, cw_ref) in zip(accs, pieces, jobs)]
        acts = [acc * jax.nn.sigmoid(acc) for acc in accs]
        gts = [(gt_ref[pl.ds(t0, L), :] + gbias).T for t0 in t0s]
        for r, t0 in enumerate(t0s):
            k_act = acts[2 * r + 1] * (1.0 / math.sqrt(hd))
            qc_ref[pl.ds(t0, L), :] = acts[2 * r]
            kc_ref[pl.ds(t0, L), :] = k_act
            kt_ref[:, pl.ds(t0, L)] = k_act.T
            gtt_ref[:, pl.ds(t0, L)] = gts[r]
        return carry

    lax.fori_loop(0, nc // conv_unroll, conv_chunk, 0)

    cn_ref[...] = jnp.zeros(cn_ref.shape, F32)
    m_ref[...] = jnp.zeros(m_ref.shape, F32)

    row = lax.broadcasted_iota(I32, (L, L), 0)
    col = lax.broadcasted_iota(I32, (L, L), 1)
    tri = ((col <= row).astype(MXU_DTYPE), (col >= row).astype(MXU_DTYPE))
    tri_t = (tri[1], tri[0])
    keep = (col <= row, col >= row)
    gate_lane = lax.broadcasted_iota(I32, (L, LANES), 1)
    gate_sub = lax.broadcasted_iota(I32, (LANES, L), 0)
    is_i, is_f = gate_lane < 4, (gate_lane >= 4) & (gate_lane < 8)
    is_i_t, is_f_t = gate_sub < 4, (gate_sub >= 4) & (gate_sub < 8)
    head_lanes = (gate_lane < hd, gate_lane >= hd)
    sel_r = lax.broadcasted_iota(I32, (LANES, 2 * LANES), 0)
    sel_c = lax.broadcasted_iota(I32, (LANES, 2 * LANES), 1)
    sel = [[(sel_r == jnp.where(sel_c < LANES, 4 + d * 2 + hh, d * 2 + hh)).astype(MXU_DTYPE) for hh in range(2)]
           for d in range(2)]
    ones_ext = jnp.ones((L, LANES), MXU_DTYPE)

    def cumulative(lf, left, d):
        hi, lo = _split2(lf)
        mat = tri[d] if left else tri_t[d]
        mm = (lambda x: jnp.dot(mat, x, preferred_element_type=F32)) if left else (
            lambda x: jnp.dot(x, mat, preferred_element_type=F32))
        return mm(hi) + mm(lo)

    chains = [(d, hh) for d in range(2) for hh in range(2)]
    f32dot = functools.partial(jnp.dot, preferred_element_type=F32)

    def chunk_step(it, carry):
        t0s = [pl.multiple_of((it if d == 0 else nc - 1 - it) * L, L) for d in range(2)]
        gpre = [gt_ref[pl.ds(t0, L), :] + gbias for t0 in t0s]
        gpre_t = [gtt_ref[:, pl.ds(t0, L)] for t0 in t0s]
        gm = [jnp.where(is_i, gpre[d], cumulative(jnp.where(is_f, jax.nn.log_sigmoid(gpre[d]), 0.0), True, d))
              for d in range(2)]
        gmt = [jnp.where(is_i_t, gpre_t[d], cumulative(jnp.where(is_f_t, jax.nn.log_sigmoid(gpre_t[d]), 0.0), False, d))
               for d in range(2)]
        gm2 = [_split2(g) for g in gm]
        qc = [qc_ref[pl.ds(t0, L), :] for t0 in t0s]
        kc = [kc_ref[pl.ds(t0, L), :].astype(MXU_DTYPE) for t0 in t0s]
        kt = [kt_ref[:, pl.ds(t0, L)].astype(MXU_DTYPE) for t0 in t0s]
        vc = [v_ref[pl.ds(t0, L), :] for t0 in t0s]
        v_ext = [jnp.concatenate([v.astype(MXU_DTYPE), ones_ext], axis=1) for v in vc]

        rep = [f32dot(gm2[d][0], sel[d][hh]) + f32dot(gm2[d][1], sel[d][hh]) for d, hh in chains]
        b_rep = [r[:, :LANES] for r in rep]
        li_rep = [r[:, LANES:] for r in rep]
        li_row = [gmt[d][d * 2 + hh:d * 2 + hh + 1, :] for d, hh in chains]
        b_row = [gmt[d][4 + d * 2 + hh:5 + d * 2 + hh, :] for d, hh in chains]
        m_st = [m_ref[hh * 2 + d] for d, hh in chains]
        cn = [cn_ref[hh * 2 + d] for d, hh in chains]
        qm = [jnp.where(head_lanes[hh], qc[d], 0.0).astype(MXU_DTYPE) for d, hh in chains]
        s_qk = [lax.dot_general(qm[c], kc[d], (((1,), (1,)), ((), ())), preferred_element_type=F32)
                for c, (d, hh) in enumerate(chains)]
        qcn = [f32dot(qm[c], cn[c].astype(MXU_DTYPE)) for c in range(4)]

        dmat = [jnp.where(keep[d], b_rep[c] - b_row[c] + li_row[c], NEG_INF) for c, (d, hh) in enumerate(chains)]
        inter = [b_rep[c] + m_st[c] for c in range(4)]
        m_j = [jnp.maximum(inter[c], jnp.max(dmat[c], axis=-1, keepdims=True)) for c in range(4)]
        a_mat = [jnp.exp(dmat[c] - m_j[c]) * s_qk[c] for c in range(4)]
        a_int = [jnp.exp(inter[c] - m_j[c]) for c in range(4)]
        av = [f32dot(a_mat[c].astype(MXU_DTYPE), v_ext[d]) for c, (d, hh) in enumerate(chains)]
        num = [av[c][:, :LANES] + a_int[c] * qcn[c][:, :LANES] for c in range(4)]
        den = [av[c][:, LANES:] + a_int[c] * qcn[c][:, LANES:] for c in range(4)]
        h_out = [num[c] / jnp.maximum(jnp.abs(den[c]), jnp.exp(-m_j[c])) for c in range(4)]

        b_last = [b_rep[c][(L - 1 if d == 0 else 0):(L if d == 0 else 1), :] for c, (d, hh) in enumerate(chains)]
        g_row = [b_last[c] - b_row[c] + li_row[c] for c in range(4)]
        g_rep = [b_last[c] - b_rep[c] + li_rep[c] for c in range(4)]
        m_new = [jnp.maximum(b_last[c] + m_st[c], jnp.max(g_row[c], axis=-1, keepdims=True)) for c in range(4)]
        w_rep = [jnp.exp(g_rep[c] - m_new[c]) for c in range(4)]
        decay = [jnp.exp(b_last[c] + m_st[c] - m_new[c]) for c in range(4)]
        wv_ext = [jnp.concatenate([w_rep[c] * vc[d].astype(F32), w_rep[c]], axis=1).astype(MXU_DTYPE)
                  for c, (d, hh) in enumerate(chains)]
        upd = [f32dot(kt[d], wv_ext[c]) for c, (d, hh) in enumerate(chains)]
        for c, (d, hh) in enumerate(chains):
            cn_ref[hh * 2 + d] = jnp.concatenate([decay[c], decay[c]], axis=1) * cn[c] + upd[c]
            m_ref[hh * 2 + d] = m_new[c]
        hf_ref[pl.ds(t0s[0], L), :] = jnp.where(head_lanes[0], h_out[0], h_out[1])
        hb_ref[pl.ds(t0s[1], L), :] = jnp.where(head_lanes[0], h_out[2], h_out[3])
        return carry

    lax.fori_loop(0, nc, chunk_step, 0)

    ng = ng_ref[...]
    lane = lax.broadcasted_iota(I32, (L, LANES), 1)
    upper = lane >= hd

    def out_chunk(ci, carry):
        t0s = [pl.multiple_of((ci * conv_unroll + r) * L, L) for r in range(conv_unroll)]
        hs = [hf_ref[pl.ds(t0, L), :] + hb_ref[pl.ds(t0, L), :] for t0 in t0s]
        sq = [h * h for h in hs]
        ms_lo = [jnp.sum(jnp.where(upper, 0.0, s), axis=-1, keepdims=True) / hd for s in sq]
        ms_hi = [jnp.sum(jnp.where(upper, s, 0.0), axis=-1, keepdims=True) / hd for s in sq]
        ys = [h * lax.rsqrt(jnp.where(upper, hi, lo) + EPS) * ng for h, hi, lo in zip(hs, ms_hi, ms_lo)]
        for t0, y in zip(t0s, ys):
            y_ref[pl.ds(t0, L), :] = (y * jax.nn.sigmoid(o_ref[pl.ds(t0, L), :].astype(F32))).astype(y_ref.dtype)
        return carry

    lax.fori_loop(0, nc // conv_unroll, out_chunk, 0)


def _mlstm(proj3, gates3, conv_w, gate_bias_row, norm_g_row, n_heads, cols):
    b, seq, _ = proj3.shape
    qb, kb, vb, ob = cols
    n_pairs = n_heads // 2
    width = n_heads * HEAD_DIM
    blk = lambda base: pl.BlockSpec((None, seq, LANES), lambda bi, p, base=base: (bi, 0, base + p))
    k_conv_base = width // LANES
    return pl.pallas_call(
        functools.partial(_mlstm_body, seq=seq, n_heads=n_heads), grid=(b, n_pairs),
        in_specs=[blk(qb), blk(kb), blk(vb), blk(ob),
                  pl.BlockSpec((None, seq, LANES), lambda bi, p: (bi, 0, p)),
                  pl.BlockSpec((MLSTM_CONV, LANES), lambda bi, p: (0, p)),
                  pl.BlockSpec((MLSTM_CONV, LANES), lambda bi, p: (0, k_conv_base + p)),
                  pl.BlockSpec((1, LANES), lambda bi, p: (0, p)),
                  pl.BlockSpec((1, LANES), lambda bi, p: (0, p))],
        out_specs=pl.BlockSpec((None, seq, LANES), lambda bi, p: (bi, 0, p)),
        out_shape=jax.ShapeDtypeStruct((b, seq, width), MXU_DTYPE),
        scratch_shapes=[pltpu.VMEM((seq + 2 * CONV_HALO, LANES), F32), pltpu.VMEM((seq + 2 * CONV_HALO, LANES), F32),
                        pltpu.VMEM((seq, LANES), F32), pltpu.VMEM((seq, LANES), F32),
                        pltpu.VMEM((LANES, seq), F32), pltpu.VMEM((LANES, seq), F32),
                        pltpu.VMEM((seq, LANES), F32), pltpu.VMEM((seq, LANES), F32),
                        pltpu.VMEM((4, LANES, 2 * LANES), F32),
                        pltpu.VMEM((4, 1, LANES), F32)],
        compiler_params=_cparams("parallel", "parallel"), name="mlstm",
    )(proj3, proj3, proj3, proj3, gates3, conv_w, conv_w, gate_bias_row, norm_g_row)


def _group_mean_sq(x, gmat):
    hi, lo = _split2(x * x)
    ssq = jnp.dot(hi, gmat, preferred_element_type=F32) + jnp.dot(lo, gmat, preferred_element_type=F32)
    return ssq * (1.0 / HEAD_DIM)


def _norm_rope(x, g_row, gmat, cos_t, sin_t):
    width = x.shape[1]
    xn = x * lax.rsqrt(_group_mean_sq(x, gmat) + EPS) * g_row
    lane = lax.broadcasted_iota(I32, x.shape, 1)
    partner = jnp.where(lane % 2 == 0, pltpu.roll(xn, width - 1, 1), pltpu.roll(xn, 1, 1))
    return xn * cos_t + partner * sin_t


def _qkprep_body(q_ref, k_ref, v_ref, qg_ref, kg_ref, gq_ref, gk_ref, cos_ref, sin_ref, qo_ref, kt_ref, vo_ref,
                 *, n_q_heads, n_kv_heads):
    hd = HEAD_DIM
    cos2, sin2 = cos_ref[...], sin_ref[...]
    reps = n_q_heads * hd // LANES
    q = _norm_rope(q_ref[...].astype(F32), qg_ref[...], gq_ref[...],
                   jnp.concatenate([cos2] * reps, axis=1), jnp.concatenate([sin2] * reps, axis=1))
    q = q * (hd ** -0.5 * LOG2_E)
    for h in range(n_q_heads):
        qo_ref[h] = q[:, h * hd:(h + 1) * hd].astype(qo_ref.dtype)
    k = _norm_rope(k_ref[...].astype(F32), kg_ref[...], gk_ref[...], cos2, sin2)
    kt = k.T
    v = v_ref[...].astype(F32)
    kv_width = v.shape[1]
    lane = lax.broadcasted_iota(I32, v.shape, 1)
    for h in range(n_kv_heads):
        kt_ref[h] = kt[h * hd:(h + 1) * hd, :].astype(kt_ref.dtype)
        vh = v if h == 0 else pltpu.roll(v, kv_width - h * hd, 1)
        vo_ref[h] = jnp.where(lane < hd, vh, jnp.where(lane == hd, 1.0, 0.0)).astype(vo_ref.dtype)


def _qkprep(proj3, q_gain_row, k_gain_row, gq, gk, cos2, sin2, n_q_heads, n_kv_heads, cols, ts):
    b, seq, _ = proj3.shape
    qcol, kcol, vcol = cols
    qw, kw = n_q_heads * HEAD_DIM, n_kv_heads * HEAD_DIM
    fixed = lambda bi, i: (0, 0)
    return pl.pallas_call(
        functools.partial(_qkprep_body, n_q_heads=n_q_heads, n_kv_heads=n_kv_heads), grid=(b, seq // ts),
        in_specs=[pl.BlockSpec((None, ts, qw), lambda bi, i: (bi, i, qcol)),
                  pl.BlockSpec((None, ts, kw), lambda bi, i: (bi, i, kcol)),
                  pl.BlockSpec((None, ts, kw), lambda bi, i: (bi, i, vcol)),
                  pl.BlockSpec((1, qw), fixed), pl.BlockSpec((1, kw), fixed),
                  pl.BlockSpec((qw, qw), fixed), pl.BlockSpec((kw, kw), fixed),
                  pl.BlockSpec((ts, kw), lambda bi, i: (i, 0)), pl.BlockSpec((ts, kw), lambda bi, i: (i, 0))],
        out_specs=[pl.BlockSpec((None, n_q_heads, ts, HEAD_DIM), lambda bi, i: (bi, 0, i, 0)),
                   pl.BlockSpec((None, n_kv_heads, HEAD_DIM, ts), lambda bi, i: (bi, 0, 0, i)),
                   pl.BlockSpec((None, n_kv_heads, ts, kw), lambda bi, i: (bi, 0, i, 0))],
        out_shape=[jax.ShapeDtypeStruct((b, n_q_heads, seq, HEAD_DIM), MXU_DTYPE),
                   jax.ShapeDtypeStruct((b, n_kv_heads, HEAD_DIM, seq), MXU_DTYPE),
                   jax.ShapeDtypeStruct((b, n_kv_heads, seq, kw), MXU_DTYPE)],
        compiler_params=_cparams("parallel", "parallel"), name="qk_prep",
    )(proj3, proj3, proj3, q_gain_row, k_gain_row, gq, gk, cos2, sin2)


def _flash_body(q_ref, kt_ref, v_ref, o_ref, *, seq, tq, tk, group):
    hd = HEAD_DIM
    rows = group * tq
    q = q_ref[...].reshape(rows, hd)
    m = jnp.full((rows, 1), NEG_INF, F32)
    acc = jnp.zeros((rows, v_ref.shape[1]), F32)
    for kc in range(seq // tk):
        s = jnp.dot(q, kt_ref[:, kc * tk:(kc + 1) * tk], preferred_element_type=F32)
        m_new = jnp.maximum(m, jnp.max(s, axis=-1, keepdims=True))
        p = jnp.exp2(s - m_new)
        acc = jnp.exp2(m - m_new) * acc + jnp.dot(p.astype(v_ref.dtype), v_ref[kc * tk:(kc + 1) * tk, :],
                                                  preferred_element_type=F32)
        m = m_new
    o = acc[:, :hd] / acc[:, hd:hd + 1]
    o_ref[...] = jnp.concatenate([o[g * tq:(g + 1) * tq] for g in range(group)], axis=1).astype(o_ref.dtype)


def _flash_attention(q4, kt4, v4, tq, tk):
    b, n_q_heads, seq, hd = q4.shape
    n_kv_heads = kt4.shape[1]
    group = n_q_heads // n_kv_heads
    return pl.pallas_call(
        functools.partial(_flash_body, seq=seq, tq=tq, tk=tk, group=group), grid=(b, n_kv_heads, seq // tq),
        in_specs=[pl.BlockSpec((None, group, tq, hd), lambda bi, h, i: (bi, h, i, 0)),
                  pl.BlockSpec((None, None, hd, seq), lambda bi, h, i: (bi, h, 0, 0)),
                  pl.BlockSpec((None, None, seq, v4.shape[3]), lambda bi, h, i: (bi, h, 0, 0))],
        out_specs=pl.BlockSpec((None, tq, group * hd), lambda bi, h, i: (bi, i, h)),
        out_shape=jax.ShapeDtypeStruct((b, seq, n_q_heads * hd), MXU_DTYPE),
        compiler_params=_cparams("parallel", "parallel", "parallel"), name="flash_gqa",
    )(q4, kt4, v4)


def _kvproj_body(m_ref, w_ref, o_ref):
    o_ref[...] = _mm(m_ref[...], w_ref[...]).astype(o_ref.dtype)


def _kvproj(mem2d, wkv, tm):
    t, d = mem2d.shape
    n = wkv.shape[1]
    return pl.pallas_call(
        _kvproj_body, grid=(t // tm,),
        in_specs=[pl.BlockSpec((tm, d), lambda i: (i, 0)), pl.BlockSpec((d, n), lambda i: (0, 0))],
        out_specs=pl.BlockSpec((tm, n), lambda i: (i, 0)), out_shape=jax.ShapeDtypeStruct((t, n), MXU_DTYPE),
        compiler_params=_cparams("parallel"), name="mem_kv_proj",
    )(mem2d, wkv)


def _crossattn_body(x_ref, yp_ref, ym_ref, ya_ref, wp_ref, wm_ref, wa_ref, kv_ref, gm_ref, wq_ref, wo_ref, gf_ref,
                    rwt_ref, xo_ref, hf_ref, lg_ref, *, ts):
    x = (x_ref[...] + _mm(yp_ref[...], wp_ref[...]) + _mm(ym_ref[...], wm_ref[...])
         + _mm(ya_ref[...], wa_ref[...]))
    q = _mm(_rms(x, gm_ref[...]), wq_ref[...])
    ca_width = q.shape[1]
    dh = ca_width // CA_HEADS
    kv = kv_ref[...]
    hs = range(CA_HEADS)
    s = [_mm_nt(q[:, h * dh:(h + 1) * dh], kv[:, h * dh:(h + 1) * dh]) * (dh ** -0.5) for h in hs]
    e = [jnp.exp(s[h] - jnp.max(s[h], axis=-1, keepdims=True)) for h in hs]
    p = [e[h] / jnp.sum(e[h], axis=-1, keepdims=True) for h in hs]
    heads = [_mm(p[h], kv[:, ca_width + h * dh:ca_width + (h + 1) * dh]) for h in hs]
    x2 = x + _mm(jnp.concatenate(heads, axis=1), wo_ref[...])
    xo_ref[...] = x2
    hf = _rms(x2, gf_ref[...])
    _to_token_tiles(hf_ref, hf, ts)
    lg_ref[...] = _mm_nt(rwt_ref[...], hf)


def _crossattn(x3, yp, ym, ya, w_out, kv3, g_mem, wq, wo, g_ffn, router_wt, ts):
    b, seq, d = x3.shape
    n_mem, kvw = kv3.shape[1], kv3.shape[2]
    ne = router_wt.shape[0]
    fixed = lambda bi, i: (0, 0)
    tile = lambda w: pl.BlockSpec((None, ts, w), lambda bi, i: (bi, i, 0))
    wp, wm, wa = yp.shape[2], ym.shape[2], ya.shape[2]
    return pl.pallas_call(
        functools.partial(_crossattn_body, ts=ts), grid=(b, seq // ts),
        in_specs=[tile(d), tile(wp), tile(wm), tile(wa),
                  pl.BlockSpec((wp, d), fixed), pl.BlockSpec((wm, d), fixed), pl.BlockSpec((wa, d), fixed),
                  pl.BlockSpec((None, n_mem, kvw), lambda bi, i: (bi, 0, 0)),
                  pl.BlockSpec((1, d), fixed), pl.BlockSpec(wq.shape, fixed), pl.BlockSpec(wo.shape, fixed),
                  pl.BlockSpec((1, d), fixed), pl.BlockSpec((ne, d), fixed)],
        out_specs=[pl.BlockSpec((None, ts, d), lambda bi, i: (bi, i, 0)),
                   pl.BlockSpec((None, ts * SUBLANES, LANES), lambda bi, i: (bi, i, 0)),
                   pl.BlockSpec((None, ne, ts), lambda bi, i: (bi, 0, i))],
        out_shape=[jax.ShapeDtypeStruct((b, seq, d), F32),
                   jax.ShapeDtypeStruct((b, seq * SUBLANES, LANES), F32),
                   jax.ShapeDtypeStruct((b, ne, seq), F32)],
        compiler_params=_cparams("parallel", "parallel"), name="cross_attn",
    )(x3, yp, ym, ya, w_out[:wp], w_out[wp:wp + wm], w_out[wp + wm:], kv3, g_mem, wq, wo, g_ffn, router_wt)


def _excl_prefix(flags, upper_excl, ones):
    rows, seq = flags.shape
    off = jnp.zeros((rows, LANES), F32)
    out = []
    for c in range(seq // LANES):
        xc = flags[:, c * LANES:(c + 1) * LANES].astype(MXU_DTYPE)
        out.append(jnp.dot(xc, upper_excl, preferred_element_type=F32) + off)
        off = off + jnp.dot(xc, ones, preferred_element_type=F32)
    return jnp.concatenate(out, axis=1)


def _router_body(lg_ref, idx_ref, gate_ref, *, seq, cap):
    lg = lg_ref[...]
    ne = lg.shape[0]
    ex = jnp.exp(lg - jnp.max(lg, axis=0, keepdims=True))
    aff = ex / jnp.sum(ex, axis=0, keepdims=True)
    bits = pltpu.bitcast(aff, I32)

    thr = jnp.zeros((ne, 1), I32)
    for bit in range(30, -1, -1):
        cand = thr | (1 << bit)
        cnt = jnp.sum((bits >= cand).astype(F32), axis=1, keepdims=True)
        thr = jnp.where(cnt >= cap, cand, thr)

    r_i = lax.broadcasted_iota(I32, (LANES, LANES), 0)
    c_i = lax.broadcasted_iota(I32, (LANES, LANES), 1)
    upper_excl = (r_i < c_i).astype(MXU_DTYPE)
    ones = jnp.ones((LANES, LANES), MXU_DTYPE)

    gt = bits > thr
    eq = bits == thr
    n_gt = jnp.sum(gt.astype(F32), axis=1, keepdims=True)
    tie_rank = _excl_prefix(eq.astype(F32), upper_excl, ones)
    sel = gt | (eq & (tie_rank < cap - n_gt))
    pos = _excl_prefix(sel.astype(F32), upper_excl, ones).astype(I32)

    n_iota = lax.broadcasted_iota(I32, (ne, seq), 1)
    seq_bits = (seq - 1).bit_length()
    flag_bit = 2 * seq_bits
    packed = jnp.where(sel, n_iota | ((n_iota - pos) << seq_bits) | (1 << flag_bit), 0)
    gbits = jnp.where(sel, bits, 0)
    for k in range(seq_bits):
        sh = 1 << k
        moved_p = pltpu.roll(packed, seq - sh, 1)
        moved_g = pltpu.roll(gbits, seq - sh, 1)
        take = ((moved_p >> flag_bit) & 1) * ((moved_p >> (seq_bits + k)) & 1) == 1
        stay = ((packed >> flag_bit) & 1) * (1 - ((packed >> (seq_bits + k)) & 1)) == 1
        packed = jnp.where(take, moved_p, jnp.where(stay, packed, 0))
        gbits = jnp.where(take, moved_g, jnp.where(stay, gbits, 0))
    idx_ref[...] = packed[:, :cap] & (seq - 1)
    gate_ref[...] = pltpu.bitcast(gbits[:, :cap], F32)


def _router(logits_t, cap):
    b, ne, seq = logits_t.shape
    return pl.pallas_call(
        functools.partial(_router_body, seq=seq, cap=cap), grid=(b,),
        in_specs=[pl.BlockSpec((None, ne, seq), lambda bi: (bi, 0, 0))],
        out_specs=[pl.BlockSpec((None, ne, cap), lambda bi: (bi, 0, 0)),
                   pl.BlockSpec((None, ne, cap), lambda bi: (bi, 0, 0))],
        out_shape=[jax.ShapeDtypeStruct((b, ne, cap), I32), jax.ShapeDtypeStruct((b, ne, cap), F32)],
        compiler_params=_cparams("parallel"), name="ec_router",
    )(logits_t)


def _moe_body(*refs, seq, cap, ne, n_fc, stride, b_off, cast, chained):
    idx_hbm, gate_hbm, h_hbm, wg_ref, wu_ref, wd_ref = refs[:6]
    prev_hbm = refs[6] if chained else None
    refs = refs[6 + (1 if chained else 0):]
    out_hbm = refs[0]
    w_outs = refs[1:4] if cast else ()
    idx_s, gate_s, xbuf, xbf, ybuf, yacc, acc, gsem, isem, osem, psem = refs[1 + len(w_outs):]

    def prev_copy():
        return pltpu.make_async_copy(prev_hbm, out_hbm.at[pl.ds(0, b_off)], psem)
    b = pl.program_id(0)
    e = pl.program_id(1)
    c = pl.program_id(2)
    last = pl.num_programs(0) * ne - 1
    step = b * ne + e
    nstep = jnp.minimum(step + 1, last)
    xslot, nxslot = lax.rem(step, 2), lax.rem(step + 1, 2)
    islot, nislot = lax.rem(step, N_IDX_SLOTS), lax.rem(step + 1, N_IDX_SLOTS)
    tile = SUBLANES
    per = cap // n_fc

    def idx_copies(s, sl):
        src = pl.ds(pl.multiple_of((b_off * ne + s) * stride, stride), stride)
        dst = pl.ds(pl.multiple_of(sl * stride, stride), stride)
        return (pltpu.make_async_copy(idx_hbm.at[src], idx_s.at[dst], isem.at[0, sl]),
                pltpu.make_async_copy(gate_hbm.at[src], gate_s.at[dst], isem.at[1, sl]))

    def row_copy(src_row, k, sl):
        return pltpu.make_async_copy(h_hbm.at[pl.ds(src_row, tile), :],
                                     xbuf.at[sl, pl.ds(k * tile, tile), :], gsem.at[sl])

    def start_rows(s, isl, sl, k0, n):
        base = (b_off + s // ne) * seq
        ibase = isl * stride + k0
        for k in range(n):
            src_row = pl.multiple_of((base + idx_s[ibase + k]) * tile, tile)
            row_copy(src_row, k0 + k, sl).start(priority=GATHER_DMA_PRIORITY)

    def wait_gather(sl):
        pltpu.make_async_copy(h_hbm.at[pl.ds(0, cap * tile), :], xbuf.at[sl], gsem.at[sl]).wait()

    @pl.when(c == 0)
    def _():
        @pl.when(step == 0)
        def _():
            for cp in idx_copies(0, 0):
                cp.start()
            for cp in idx_copies(0, 0):
                cp.wait()
            for cp in idx_copies(jnp.minimum(1, last), 1):
                cp.start()
            if chained:
                prev_copy().start()

            def body(j, carry):
                start_rows(0, 0, 0, j * SUBLANES, SUBLANES)
                return carry
            lax.fori_loop(0, cap // SUBLANES, body, 0)

        for cp in idx_copies(0, nislot):
            cp.wait()
        for cp in idx_copies(jnp.minimum(step + 2, last), lax.rem(step + 2, N_IDX_SLOTS)):
            cp.start()
        wait_gather(xslot)
        xbf[...] = _from_token_tiles(xbuf.at[xslot], cap).astype(xbf.dtype)

        @pl.when(e == 0)
        def _():
            acc[...] = jnp.zeros(acc.shape, F32)

    start_rows(nstep, nislot, nxslot, pl.multiple_of(c * per, per), per)

    x = xbf[...]
    fc = wg_ref.shape[1]
    sub = min(fc, MOE_SUB_COLS)
    y = None
    for j in range(fc // sub):
        cols = slice(j * sub, (j + 1) * sub)
        wg, wu, wd = wg_ref[:, cols], wu_ref[:, cols], wd_ref[cols, :]
        if cast:
            wg, wu, wd = wg.astype(MXU_DTYPE), wu.astype(MXU_DTYPE), wd.astype(MXU_DTYPE)
            w_outs[0][:, cols] = wg
            w_outs[1][:, cols] = wu
            w_outs[2][cols, :] = wd
        g = jnp.dot(x, wg, preferred_element_type=F32)
        u = jnp.dot(x, wu, preferred_element_type=F32)
        yj = _mm(g * jax.nn.sigmoid(g) * u, wd)
        y = yj if y is None else y + yj

    if n_fc == 1:
        _to_token_tiles(ybuf, y, cap)
    else:
        @pl.when(c == 0)
        def _():
            yacc[...] = y

        @pl.when(c > 0)
        def _():
            yacc[...] += y

    @pl.when(c == n_fc - 1)
    def _():
        if n_fc > 1:
            _to_token_tiles(ybuf, yacc[...], cap)

        ibase = islot * stride
        for k0 in range(0, cap, SCATTER_GROUP):
            rows = [pl.multiple_of(idx_s[ibase + k0 + j] * tile, tile) for j in range(SCATTER_GROUP)]
            new = [acc[pl.ds(rows[j], tile), :] + gate_s[ibase + k0 + j] * ybuf[(k0 + j) * tile:(k0 + j + 1) * tile, :]
                   for j in range(SCATTER_GROUP)]
            for j in range(SCATTER_GROUP):
                acc[pl.ds(rows[j], tile), :] = new[j]

        @pl.when(e == ne - 1)
        def _():
            cp = pltpu.make_async_copy(acc, out_hbm.at[b_off + b], osem)
            cp.start()
            cp.wait()

        @pl.when(step == last)
        def _():
            wait_gather(nxslot)
            for cp in idx_copies(0, lax.rem(step + 2, N_IDX_SLOTS)):
                cp.wait()
            if chained:
                prev_copy().wait()


def _moe(idx, gate, h_tiles, w_gate, w_up, w_down, up_block_off, seq, fc, b_off, n_b, prev_out, cast, layer=None):
    _, ne, cap = idx.shape
    d, f = w_down.shape[-1], w_down.shape[-2]
    n_fc = f // fc
    lead = () if layer is None else (layer,)
    w_spec = lambda shape, imap: pl.BlockSpec((None,) * (len(lead) + 1) + shape,
                                               lambda b, e, c: lead + imap(e, c))
    chained = prev_out is not None
    stride = -(-cap // SMEM_1D_TILE) * SMEM_1D_TILE
    flat = lambda a: jnp.pad(a, ((0, 0), (0, 0), (0, stride - cap))).reshape(-1)
    any_spec = pl.BlockSpec(memory_space=pl.ANY)
    assert chained == (b_off > 0)
    out_shape = [jax.ShapeDtypeStruct((b_off + n_b, seq * SUBLANES, LANES), F32)]
    out_specs = [any_spec]
    if cast:
        out_shape += [jax.ShapeDtypeStruct((ne, d, f), MXU_DTYPE), jax.ShapeDtypeStruct((ne, d, f), MXU_DTYPE),
                      jax.ShapeDtypeStruct((ne, f, d), MXU_DTYPE)]
        out_specs += [pl.BlockSpec((None, d, fc), lambda b, e, c: (e, 0, c)),
                      pl.BlockSpec((None, d, fc), lambda b, e, c: (e, 0, c)),
                      pl.BlockSpec((None, fc, d), lambda b, e, c: (e, c, 0))]
    return pl.pallas_call(
        functools.partial(_moe_body, seq=seq, cap=cap, ne=ne, n_fc=n_fc, stride=stride, b_off=b_off, cast=cast,
                          chained=chained),
        grid=(n_b, ne, n_fc),
        in_specs=[any_spec, any_spec, any_spec,
                  w_spec((d, fc), lambda e, c: (e, 0, c)),
                  w_spec((d, fc), lambda e, c: (e, 0, up_block_off + c)),
                  w_spec((fc, d), lambda e, c: (e, c, 0))] + ([any_spec] if chained else []),
        out_specs=out_specs, out_shape=out_shape,
        scratch_shapes=[pltpu.SMEM((N_IDX_SLOTS * stride,), I32), pltpu.SMEM((N_IDX_SLOTS * stride,), F32),
                        pltpu.VMEM((2, cap * SUBLANES, LANES), F32), pltpu.VMEM((cap, d), MXU_DTYPE),
                        pltpu.VMEM((cap * SUBLANES, LANES), F32), pltpu.VMEM((cap, d), F32),
                        pltpu.VMEM((seq * SUBLANES, LANES), F32),
                        pltpu.SemaphoreType.DMA((2,)), pltpu.SemaphoreType.DMA((2, N_IDX_SLOTS)),
                        pltpu.SemaphoreType.DMA, pltpu.SemaphoreType.DMA],
        compiler_params=_cparams("arbitrary", "arbitrary", "arbitrary"),
        name="ec_moe_cast" if cast else "ec_moe",
    )(flat(idx), flat(gate), h_tiles, w_gate, w_up, w_down, *([prev_out] if chained else []))


def _final_body(x_ref, moe_ref, g_ref, o_ref, *, tm):
    o_ref[...] = _rms(x_ref[...] + _from_token_tiles(moe_ref, tm), g_ref[...])


def _final_norm(x2d, moe_tiles, g, tm):
    t, d = x2d.shape
    row = lambda i: (i, 0)
    return pl.pallas_call(
        functools.partial(_final_body, tm=tm), grid=(t // tm,),
        in_specs=[pl.BlockSpec((tm, d), row), pl.BlockSpec((tm * SUBLANES, LANES), row),
                  pl.BlockSpec((1, d), lambda i: (0, 0))],
        out_specs=pl.BlockSpec((tm, d), row), out_shape=jax.ShapeDtypeStruct((t, d), F32),
        compiler_params=_cparams("parallel"), name="final_norm",
    )(x2d, moe_tiles, g)


def _rope_tables(seq):
    rows = seq // GRID_W
    row_ids = jnp.repeat(jnp.arange(rows), GRID_W).astype(F32)
    col_ids = jnp.tile(jnp.arange(GRID_W), rows).astype(F32)
    n_freq = HEAD_DIM // 4
    inv_freq = ROPE_THETA ** (-jnp.arange(n_freq, dtype=F32) / n_freq)
    ang = jnp.concatenate([row_ids[:, None] * inv_freq, col_ids[:, None] * inv_freq], axis=-1)
    cos, sin = jnp.cos(ang), jnp.sin(ang)
    cos_i = jnp.repeat(cos, 2, axis=1)
    sin_i = jnp.stack([-sin, sin], axis=-1).reshape(seq, HEAD_DIM)
    return jnp.tile(cos_i, (1, LANES // HEAD_DIM)), jnp.tile(sin_i, (1, LANES // HEAD_DIM))


def _group_matrix(width):
    gid = jnp.arange(width) // HEAD_DIM
    return (gid[:, None] == gid[None, :]).astype(MXU_DTYPE)


def _block_diag(w):
    g, c, _ = w.shape
    eye = jnp.eye(g, dtype=w.dtype)
    return (eye[:, None, :, None] * w[:, :, None, :]).reshape(g * c, g * c)


def _pick_tile(n, pref):
    t = min(n, pref)
    while n % t:
        t //= 2
    return t


def kernel(x, mem, norm_mix_g, w_in, pool_w, pool_scale, mlstm_conv_w, mlstm_gate_b, mlstm_norm_g, q_norm_g,
           k_norm_g, w_out, norm_mem_g, ca_wq, ca_wkv, ca_wo, norm_ffn_g, router_w, expert_w_gu, expert_w_down,
           final_norm_g):
    batch, seq, d = x.shape
    depth = w_in.shape[0]
    n_mem = mem.shape[1]
    pool_width = pool_scale.shape[1]
    m_width = mlstm_norm_g.shape[1]
    m_heads = m_width // HEAD_DIM
    n_gate = 2 * m_heads
    kv_width = (w_in.shape[2] - pool_width - 4 * m_width - 2 * n_gate) // 6
    q_width = 4 * kv_width
    n_q_heads, n_kv_heads = q_width // HEAD_DIM, kv_width // HEAD_DIM
    ne = router_w.shape[2]
    cap = EC_CAPACITY_FACTOR * seq // ne
    f_exp = expert_w_down.shape[2]
    t = batch * seq

    o_pool = 0
    o_mq = o_pool + pool_width
    o_mi = o_mq + 4 * m_width
    o_aq = o_mi + 2 * n_gate
    o_ak = o_aq + q_width
    c_pool = q_width // pool_width
    c_mq = (q_width + pool_width) // LANES
    m_cols = tuple(c_mq + j * (m_width // LANES) for j in range(4))
    c_ak = (q_width + pool_width + 4 * m_width) // kv_width
    cos2, sin2 = _rope_tables(seq)
    gq, gk = _group_matrix(q_width), _group_matrix(kv_width)
    n_pairs = m_heads // 2
    gate_cols = [[kind * n_gate + dr * m_heads + 2 * p + hh for kind in range(2) for dr in range(2) for hh in range(2)]
                 for p in range(n_pairs)]

    assert kv_width == LANES and MLSTM_CHUNK == LANES and m_heads % 2 == 0
    tm = _pick_tile(t, TOKEN_TILE)
    ts = _pick_tile(seq, TOKEN_TILE)
    x2d = x.reshape(t, d)
    mem2d = mem.reshape(batch * n_mem, d)
    moe_tiles = None
    for layer in range(depth):
        wl = w_in[layer]
        w_main = jnp.concatenate([wl[:, o_aq:o_ak], wl[:, o_pool:o_mi], wl[:, o_ak:]], axis=1).astype(MXU_DTYPE)
        w_gate = jnp.concatenate(
            [jnp.pad(wl[:, o_mi:o_aq][:, jnp.array(cols)], ((0, 0), (0, LANES - len(cols)))) for cols in gate_cols],
            axis=1).astype(MXU_DTYPE)
        gb_row = jnp.concatenate(
            [jnp.pad(mlstm_gate_b[layer][jnp.array(cols)], (0, LANES - len(cols))) for cols in gate_cols])[None]
        x2d, proj, gates = _inproj(x2d, moe_tiles, norm_mix_g[layer][None], w_main, w_gate, tm)
        proj3 = proj.reshape(batch, seq, proj.shape[1])
        gates3 = gates.reshape(batch, seq, gates.shape[1])

        y_pool = _pool_mixer(proj3, _block_diag(pool_w[layer]).astype(MXU_DTYPE), pool_scale[layer][None], c_pool,
                             _pick_tile(seq, POOL_TILE))
        y_mlstm = _mlstm(proj3, gates3, mlstm_conv_w[layer], gb_row, mlstm_norm_g[layer][None], m_heads, m_cols)
        q4, kt4, v4 = _qkprep(proj3, jnp.tile(q_norm_g[layer], n_q_heads)[None],
                              jnp.tile(k_norm_g[layer], n_kv_heads)[None], gq, gk, cos2, sin2,
                              n_q_heads, n_kv_heads, (0, c_ak, c_ak + 1), ts)
        y_attn = _flash_attention(q4, kt4, v4, _pick_tile(seq, FLASH_Q_TILE), _pick_tile(seq, FLASH_KV_TILE))

        kv = _kvproj(mem2d, ca_wkv[layer].astype(MXU_DTYPE), _pick_tile(batch * n_mem, TOKEN_TILE))
        x3, h_tiles, logits_t = _crossattn(
            x2d.reshape(batch, seq, d), y_pool, y_mlstm, y_attn, w_out[layer].astype(MXU_DTYPE),
            kv.reshape(batch, n_mem, kv.shape[1]), norm_mem_g[layer][None],
            ca_wq[layer].astype(MXU_DTYPE), ca_wo[layer].astype(MXU_DTYPE), norm_ffn_g[layer][None],
            router_w[layer].T.astype(MXU_DTYPE), ts)
        x2d = x3.reshape(t, d)

        idx, gate = _router(logits_t, cap)
        h_flat = h_tiles.reshape(t * SUBLANES, LANES)
        fc_cast = _pick_tile(f_exp, MOE_F_CHUNK_CAST)
        moe_tiles, wg_c, wu_c, wd_c = _moe(idx, gate, h_flat, expert_w_gu, expert_w_gu, expert_w_down,
                                           f_exp // fc_cast, seq, fc_cast, 0, 1, None, True, layer)
        if batch > 1:
            moe_tiles, = _moe(idx, gate, h_flat, wg_c, wu_c, wd_c, 0, seq, _pick_tile(f_exp, MOE_F_CHUNK), 1,
                              batch - 1, moe_tiles, False)
        moe_tiles = moe_tiles.reshape(t * SUBLANES, LANES)
    out = _final_norm(x2d, moe_tiles, final_norm_g[None], tm)
    return out.reshape(batch, seq, d)
```
